```python
import jax, jax.numpy as jnp
from jax import lax
import numpy as np

D_MODEL = 1024
BATCH = 8
SEQ = 8192
DEPTH = 2

GRID_W = 64
CTX_LEN = 256
NAT_HEADS = 8
NAT_HEAD_DIM = 64
NAT_WIDTH = NAT_HEADS * NAT_HEAD_DIM
NAT_KH = 8
NAT_KW = 16
HGRN_HEADS = 4
HGRN_KEY_DIM = 128
HGRN_VAL_DIM = 128
HGRN_QK_WIDTH = HGRN_HEADS * HGRN_KEY_DIM
HGRN_V_WIDTH = HGRN_HEADS * HGRN_VAL_DIM
HGRN_CHUNK = 64
AB_IN_WIDTH = 3 * NAT_WIDTH + 3 * HGRN_QK_WIDTH + 2 * HGRN_V_WIDTH
AB_OUT_WIDTH = NAT_WIDTH + HGRN_V_WIDTH
CONV_WIDTH = 31
N_EXPERTS = 32
TOP_K = 4
D_FF_EXPERT = D_MODEL
SWIGLU_LIMIT = 7.0
SWIGLU_ALPHA = 1.702
ROPE_BASE = 10000.0
NORM_EPS = 1e-6
MASK_VALUE = -1e30

kernel_name = "hybrid_nat_hgrn2_conformer_moe_dit"


def _rmsnorm(x, g):
    xf = x.astype(jnp.float32)
    y = xf * lax.rsqrt(jnp.mean(xf * xf, axis=-1, keepdims=True) + NORM_EPS)
    return (y * g.astype(jnp.float32)).astype(x.dtype)


def _layernorm(x, g, b):
    xf = x.astype(jnp.float32)
    mu = jnp.mean(xf, axis=-1, keepdims=True)
    var = jnp.mean(jnp.square(xf - mu), axis=-1, keepdims=True)
    y = (xf - mu) * lax.rsqrt(var + NORM_EPS)
    return (y * g.astype(jnp.float32) + b.astype(jnp.float32)).astype(x.dtype)


def _modulate(x, shift, scale):
    return x * (1 + scale) + shift


def _heads(a, n_heads):
    return a.reshape(a.shape[:-1] + (n_heads, a.shape[-1] // n_heads))


def _rotate(xh, pos):
    n = xh.shape[-1] // 2
    inv_freq = ROPE_BASE ** (-jnp.arange(n, dtype=jnp.float32) / n)
    ang = pos.astype(jnp.float32)[:, None] * inv_freq[None, :]
    cos = jnp.cos(ang)[None, :, None, :]
    sin = jnp.sin(ang)[None, :, None, :]
    x1 = xh[..., :n].astype(jnp.float32)
    x2 = xh[..., n:].astype(jnp.float32)
    return jnp.concatenate([x1 * cos - x2 * sin, x2 * cos + x1 * sin], axis=-1).astype(xh.dtype)


def _axial_rope(x, row_pos, col_pos):
    half = x.shape[-1] // 2
    return jnp.concatenate([_rotate(x[..., :half], row_pos), _rotate(x[..., half:], col_pos)], axis=-1)


def _neighbourhood_attention(q, k, v, q_free, k_ctx, v_ctx, rpb, rows):
    B, S, H, dh = q.shape
    kh = min(NAT_KH, rows)
    n_cb = GRID_W // NAT_KW
    band = 2 * NAT_KW
    q_cols = np.arange(GRID_W).reshape(n_cb, NAT_KW)
    win_c = np.clip(q_cols - NAT_KW // 2, 0, GRID_W - NAT_KW)
    band_c = np.clip(np.arange(n_cb) * NAT_KW - NAT_KW // 2, 0, GRID_W - band)
    key_cols = band_c[:, None] + np.arange(band)[None, :]
    kc = key_cols[:, None, :]
    col_mask = (kc >= win_c[..., None]) & (kc < win_c[..., None] + NAT_KW)
    col_bias_idx = np.clip(kc - q_cols[..., None] + NAT_KW - 1, 0, 2 * NAT_KW - 2)
    kg = k.reshape(B, rows, GRID_W, H, dh)
    vg = v.reshape(B, rows, GRID_W, H, dh)
    n_loc = kh * band

    def row_block(args):
        r, q_r, qf_r = args
        rs = jnp.clip(r - kh // 2, 0, rows - kh)
        k_blk = lax.dynamic_slice_in_dim(kg, rs, kh, axis=1)[:, :, key_cols]
        v_blk = lax.dynamic_slice_in_dim(vg, rs, kh, axis=1)[:, :, key_cols]
        bias = jnp.take(rpb, rs + jnp.arange(kh) - r + NAT_KH - 1, axis=1)[:, :, col_bias_idx]
        bias = bias.transpose(0, 2, 3, 1, 4).astype(jnp.float32)
        q_blk = q_r.reshape(B, n_cb, NAT_KW, H, dh)
        s_loc = jnp.einsum('bcqhd,bicjhd->bhcqij', q_blk, k_blk).astype(jnp.float32) + bias[None]
        s_loc = jnp.where(col_mask[None, None, :, :, None, :], s_loc, MASK_VALUE)
        s_ctx = jnp.einsum('bcqhd,blhd->bhcql', qf_r.reshape(B, n_cb, NAT_KW, H, dh), k_ctx).astype(jnp.float32)
        s_all = jnp.concatenate([s_loc.reshape(B, H, n_cb, NAT_KW, n_loc), s_ctx], axis=-1)
        p = jax.nn.softmax(s_all, axis=-1).astype(v.dtype)
        out = (jnp.einsum('bhcqij,bicjhd->bcqhd', p[..., :n_loc].reshape(B, H, n_cb, NAT_KW, kh, band), v_blk)
               + jnp.einsum('bhcql,blhd->bcqhd', p[..., n_loc:], v_ctx))
        return out.reshape(B, GRID_W, H, dh)

    def to_rows(a):
        return a.reshape(B, rows, GRID_W, H, dh).transpose(1, 0, 2, 3, 4)

    out = lax.map(row_block, (jnp.arange(rows, dtype=jnp.int32), to_rows(q), to_rows(q_free)))
    return out.transpose(1, 0, 2, 3, 4).reshape(B, S, H, dh)


def _context_attention(q, k, v):
    s = jnp.einsum('blhd,bmhd->bhlm', q, k).astype(jnp.float32)
    p = jax.nn.softmax(s, axis=-1).astype(v.dtype)
    return jnp.einsum('bhlm,bmhd->blhd', p, v)


def _hgrn_query(q_raw):
    return _heads(jax.nn.silu(q_raw), HGRN_HEADS) * (HGRN_KEY_DIM ** -0.5)


def _hgrn_gates(f_raw, lb):
    f = lb + (1 - lb) * jax.nn.sigmoid(f_raw.astype(jnp.float32))
    f = _heads(f, HGRN_HEADS)
    return 1 - f, jnp.log(f)


def _flip_if(a, rev):
    return jnp.flip(a, axis=1) if rev else a


def _gla_chunked(q, k, v, log_f, s0):
    B, T, H, dk = k.shape
    dv = v.shape[-1]
    nc = T // HGRN_CHUNK

    def to_chunks(a):
        return a.reshape(B, nc, HGRN_CHUNK, H, a.shape[-1]).transpose(1, 0, 3, 2, 4)

    with_out = q is not None
    xs = (to_chunks(k), to_chunks(v), to_chunks(log_f)) + ((to_chunks(q),) if with_out else ())
    lower = np.tril(np.ones((HGRN_CHUNK, HGRN_CHUNK), dtype=bool))[None, None, :, :, None]

    def step(state, chunk):
        k_c, v_c, lf_c = chunk[0], chunk[1], chunk[2]
        b = jnp.cumsum(lf_c, axis=2)
        b_end = b[:, :, -1:, :]
        state_new = (jnp.exp(b_end)[:, :, 0, :, None] * state
                     + jnp.einsum('bhsk,bhsv->bhkv', k_c * jnp.exp(b_end - b), v_c))
        if not with_out:
            return state_new, None
        q_c = chunk[3]
        decay = jnp.exp(jnp.where(lower, b[:, :, :, None, :] - b[:, :, None, :, :], -jnp.inf))
        scores = jnp.einsum('bhtk,bhsk,bhtsk->bhts', q_c, k_c, decay)
        o_c = (jnp.einsum('bhts,bhsv->bhtv', scores, v_c)
               + jnp.einsum('bhtk,bhkv->bhtv', q_c * jnp.exp(b), state))
        return state_new, o_c

    s_final, o = lax.scan(step, s0, xs)
    if not with_out:
        return None, s_final
    return o.transpose(1, 0, 3, 2, 4).reshape(B, T, H, dv), s_final


def _ab_mixer(xm, xm_ctx, w_in, w_out, q_norm_g, k_norm_g, rpb, lb, o_norm_g, row_pos, col_pos, rows, ctx_out):
    B, S, _ = xm.shape
    L = xm_ctx.shape[1]
    splits = np.cumsum([NAT_WIDTH] * 3 + [HGRN_QK_WIDTH] * 3 + [HGRN_V_WIDTH] * 2)[:-1].tolist()
    nq, nk, nv, hq, hf_fwd, hf_bwd, hi, hg = jnp.split(xm @ w_in, splits, axis=-1)
    cq, ck, cv, chq, chf_fwd, chf_bwd, chi, chg = jnp.split(xm_ctx @ w_in, splits, axis=-1)

    scale = NAT_HEAD_DIM ** -0.5
    q = _rmsnorm(_heads(nq, NAT_HEADS), q_norm_g) * scale
    k = _rmsnorm(_heads(nk, NAT_HEADS), k_norm_g)
    v = _heads(nv, NAT_HEADS)
    k_c = _rmsnorm(_heads(ck, NAT_HEADS), k_norm_g)
    v_c = _heads(cv, NAT_HEADS)
    nat = _neighbourhood_attention(_axial_rope(q, row_pos, col_pos), _axial_rope(k, row_pos, col_pos), v,
                                   q, k_c, v_c, rpb, rows)

    s0 = jnp.zeros((B, HGRN_HEADS, HGRN_KEY_DIM, HGRN_VAL_DIM), jnp.float32)
    q_l = _hgrn_query(hq)
    i_l = _heads(hi, HGRN_HEADS)
    i_c = _heads(chi, HGRN_HEADS)
    q_ch = _hgrn_query(chq) if ctx_out else None
    o_l = jnp.zeros((B, S, HGRN_HEADS, HGRN_VAL_DIM), jnp.float32)
    o_c = jnp.zeros((B, L, HGRN_HEADS, HGRN_VAL_DIM), jnp.float32)
    for d, (f_lat, f_ctx) in enumerate(((hf_fwd, chf_fwd), (hf_bwd, chf_bwd))):
        rev = d == 1
        k_lat, lf_lat = _hgrn_gates(f_lat, lb[d])
        k_cx, lf_cx = _hgrn_gates(f_ctx, lb[d])
        oc, s_ctx = _gla_chunked(_flip_if(q_ch, rev) if ctx_out else None, _flip_if(k_cx, rev),
                                 _flip_if(i_c, rev), _flip_if(lf_cx, rev), s0)
        ol, _ = _gla_chunked(_flip_if(q_l, rev), _flip_if(k_lat, rev), _flip_if(i_l, rev),
                             _flip_if(lf_lat, rev), s_ctx)
        o_l = o_l + _flip_if(ol, rev)
        if ctx_out:
            o_c = o_c + _flip_if(oc, rev)

    def gated(o, g):
        y = _rmsnorm(o.astype(xm.dtype), o_norm_g) * jax.nn.silu(_heads(g, HGRN_HEADS))
        return y.reshape(o.shape[0], o.shape[1], HGRN_V_WIDTH)

    y = jnp.concatenate([nat.reshape(B, S, NAT_WIDTH), gated(o_l, hg)], axis=-1) @ w_out
    if not ctx_out:
        return y, None
    q_c = _rmsnorm(_heads(cq, NAT_HEADS), q_norm_g) * scale
    nat_c = _context_attention(q_c, k_c, v_c)
    y_ctx = jnp.concatenate([nat_c.reshape(B, L, NAT_WIDTH), gated(o_c, chg)], axis=-1) @ w_out
    return y, y_ctx


def _depthwise_conv(x, w, b):
    C = x.shape[-1]
    pad = CONV_WIDTH // 2
    y = lax.conv_general_dilated(x, w[:, None, :].astype(x.dtype), window_strides=(1,), padding=[(pad, pad)],
                                 dimension_numbers=('NWC', 'WIO', 'NWC'), feature_group_count=C)
    return y + b


def _conformer_conv(xm, w1, b1, w_dw, b_dw, ln_g, ln_b, w2, b2):
    h = xm @ w1 + b1
    a, gate = jnp.split(h, 2, axis=-1)
    h = a * jax.nn.sigmoid(gate)
    h = _depthwise_conv(h, w_dw, b_dw)
    h = jax.nn.silu(_layernorm(h, ln_g, ln_b))
    return h @ w2 + b2


def _moe(x2d, router_w, router_b, w1, b1, w2, b2):
    logits = (x2d @ router_w + router_b).astype(jnp.float32)
    top_vals, top_idx = lax.top_k(logits, TOP_K)
    top_w = jax.nn.softmax(top_vals, axis=-1)
    combine = jnp.sum(jax.nn.one_hot(top_idx, N_EXPERTS, dtype=jnp.float32) * top_w[..., None], axis=1)

    def expert(acc, p):
        w1e, b1e, w2e, b2e, g_e = p
        h = x2d @ w1e + b1e
        x_glu, x_lin = jnp.split(h, 2, axis=-1)
        x_glu = jnp.minimum(x_glu, SWIGLU_LIMIT)
        x_lin = jnp.clip(x_lin, -SWIGLU_LIMIT, SWIGLU_LIMIT)
        act = x_glu * jax.nn.sigmoid(SWIGLU_ALPHA * x_glu) * (x_lin + 1)
        return acc + g_e[:, None] * (act @ w2e + b2e), None

    out, _ = lax.scan(expert, jnp.zeros_like(x2d), (w1, b1, w2, b2, combine.T.astype(x2d.dtype)))
    return out


def setup_inputs(seed: int = 0) -> dict:
    key = jax.random.key(seed)
    ks = iter(jax.random.split(key, 32))
    n_ab = (DEPTH + 1) // 2
    n_c = DEPTH // 2
    D = D_MODEL

    def nrm(shape, scale):
        return jax.random.normal(next(ks), shape, jnp.float32) * scale

    return {
        'x': nrm((BATCH, SEQ, D), 1.0),
        'c': nrm((BATCH, D), 1.0),
        'ctx': nrm((BATCH, CTX_LEN, D), 1.0),
        'c_ctx': nrm((D,), 1.0),
        'ada_w': nrm((DEPTH, D, 6 * D), 0.5 * D ** -0.5),
        'ada_b': nrm((DEPTH, 6 * D), 0.02),
        'norm1_g': 1.0 + nrm((DEPTH, D), 0.02),
        'norm2_g': 1.0 + nrm((DEPTH, D), 0.02),
        'ab_w_in': nrm((n_ab, D, AB_IN_WIDTH), D ** -0.5),
        'ab_w_out': nrm((n_ab, AB_OUT_WIDTH, D), AB_OUT_WIDTH ** -0.5),
        'nat_q_norm': 1.0 + nrm((n_ab, NAT_HEAD_DIM), 0.02),
        'nat_k_norm': 1.0 + nrm((n_ab, NAT_HEAD_DIM), 0.02),
        'nat_rpb': nrm((n_ab, NAT_HEADS, 2 * NAT_KH - 1, 2 * NAT_KW - 1), 0.5),
        'hgrn_lb': nrm((2, n_ab + 1, HGRN_QK_WIDTH), 0.5),
        'hgrn_o_norm': 1.0 + nrm((n_ab, HGRN_VAL_DIM), 0.02),
        'conv_w1': nrm((n_c, D, 2 * D), D ** -0.5),
        'conv_b1': nrm((n_c, 2 * D), 0.02),
        'conv_dw': nrm((n_c, CONV_WIDTH, D), CONV_WIDTH ** -0.5),
        'conv_dw_b': nrm((n_c, D), 0.02),
        'conv_ln_g': 1.0 + nrm((n_c, D), 0.02),
        'conv_ln_b': nrm((n_c, D), 0.02),
        'conv_w2': nrm((n_c, D, D), D ** -0.5),
        'conv_b2': nrm((n_c, D), 0.02),
        'router_w': nrm((DEPTH, D, N_EXPERTS), D ** -0.5),
        'router_b': nrm((DEPTH, N_EXPERTS), 0.01),
        'moe_w1': nrm((DEPTH, N_EXPERTS, D, 2 * D_FF_EXPERT), D ** -0.5),
        'moe_b1': nrm((DEPTH, N_EXPERTS, 2 * D_FF_EXPERT), 0.02),
        'moe_w2': nrm((DEPTH, N_EXPERTS, D_FF_EXPERT, D), D_FF_EXPERT ** -0.5),
        'moe_b2': nrm((DEPTH, N_EXPERTS, D), 0.02),
    }


def reference(x, c, ctx, c_ctx, ada_w, ada_b, norm1_g, norm2_g, ab_w_in, ab_w_out, nat_q_norm, nat_k_norm,
              nat_rpb, hgrn_lb, hgrn_o_norm, conv_w1, conv_b1, conv_dw, conv_dw_b, conv_ln_g, conv_ln_b,
              conv_w2, conv_b2, router_w, router_b, moe_w1, moe_b1, moe_w2, moe_b2):
    B, S, D = x.shape
    L = ctx.shape[1]
    rows = S // GRID_W
    pos = jnp.arange(S, dtype=jnp.int32)
    row_pos, col_pos = pos // GRID_W, pos % GRID_W
    lb_all = jnp.cumsum(jax.nn.softmax(hgrn_lb.astype(jnp.float32), axis=1), axis=1)
    silu_c = jax.nn.silu(c)
    silu_cc = jax.nn.silu(c_ctx)
    h, h_ctx = x, ctx
    for layer in range(DEPTH):
        j = layer // 2
        is_ab = layer % 2 == 0
        ctx_after = any(l % 2 == 0 for l in range(layer + 1, DEPTH))
        mod = jnp.split((silu_c @ ada_w[layer] + ada_b[layer])[:, None, :], 6, axis=-1)
        xm = _modulate(_rmsnorm(h, norm1_g[layer]), mod[0], mod[1])
        if is_ab or ctx_after:
            cmod = jnp.split(silu_cc @ ada_w[layer] + ada_b[layer], 6, axis=-1)
            xm_ctx = _modulate(_rmsnorm(h_ctx, norm1_g[layer]), cmod[0], cmod[1])
        if is_ab:
            y, y_ctx = _ab_mixer(xm, xm_ctx, ab_w_in[j], ab_w_out[j], nat_q_norm[j], nat_k_norm[j], nat_rpb[j],
                                 lb_all[:, j], hgrn_o_norm[j], row_pos, col_pos, rows, ctx_after)
        else:
            conv_p = (conv_w1[j], conv_b1[j], conv_dw[j], conv_dw_b[j], conv_ln_g[j], conv_ln_b[j], conv_w2[j], conv_b2[j])
            y = _conformer_conv(xm, *conv_p)
            y_ctx = _conformer_conv(xm_ctx, *conv_p) if ctx_after else None
        h = h + mod[2] * y
        moe_p = (router_w[layer], router_b[layer], moe_w1[layer], moe_b1[layer], moe_w2[layer], moe_b2[layer])
        if ctx_after:
            h_ctx = h_ctx + cmod[2] * y_ctx
            xm2 = jnp.concatenate([_modulate(_rmsnorm(h, norm2_g[layer]), mod[3], mod[4]),
                                   _modulate(_rmsnorm(h_ctx, norm2_g[layer]), cmod[3], cmod[4])], axis=1)
            out2 = _moe(xm2.reshape(-1, D), *moe_p).reshape(B, S + L, D)
            h = h + mod[5] * out2[:, :S]
            h_ctx = h_ctx + cmod[5] * out2[:, S:]
        else:
            xm2 = _modulate(_rmsnorm(h, norm2_g[layer]), mod[3], mod[4])
            h = h + mod[5] * _moe(xm2.reshape(-1, D), *moe_p).reshape(B, S, D)
    return h
```

```python
import functools

import numpy as np
import jax
import jax.numpy as jnp
from jax import lax
from jax.experimental import pallas as pl
from jax.experimental.pallas import tpu as pltpu

F32 = jnp.float32
BF16 = jnp.bfloat16
U32 = jnp.uint32
I32 = jnp.int32
HIGHEST = lax.Precision.HIGHEST

GRID_W = 64
NAT_HEADS = 8
NAT_HEAD_DIM = 64
NAT_WIDTH = NAT_HEADS * NAT_HEAD_DIM
NAT_KH = 8
NAT_KW = 16
HGRN_HEADS = 4
HGRN_DIM = 128
HGRN_WIDTH = HGRN_HEADS * HGRN_DIM
HGRN_CHUNK = 64
HGRN_SUB = 16
CONV_WIDTH = 31
CONV_HALO = 16
N_EXPERTS = 32
TOP_K = 4
SWIGLU_LIMIT = 7.0
SWIGLU_ALPHA = 1.702
ROPE_BASE = 10000.0
NORM_EPS = 1e-6
MASK_VALUE = -1e30
EXP_CLAMP = 80.0

LANES = 128
MOD_ROWS = 16
VMEM_LIMIT = 56 * 1024 * 1024
NT = (((1,), (1,)), ((), ()))


def _params(sem, vmem=None):
    return pltpu.CompilerParams(dimension_semantics=sem, vmem_limit_bytes=vmem)


def _dot(a, b):
    return jnp.dot(a, b, preferred_element_type=F32)


def _dot_nt(a, b):
    return lax.dot_general(a, b, NT, preferred_element_type=F32)


def _silu(x):
    return x * jax.nn.sigmoid(x)


def _rms(x, g):
    return x * lax.rsqrt(jnp.mean(x * x, axis=-1, keepdims=True) + NORM_EPS) * g


def _mod_kernel(cc_ref, w_ref, b_ref, o_ref):
    cc = cc_ref[...]
    o_ref[0] = jnp.dot(_silu(cc), w_ref[0], precision=HIGHEST, preferred_element_type=F32) + b_ref[0]


def _modulation(cc, ada_w, ada_b):
    depth, d, n = ada_w.shape
    tn = n // 4
    return pl.pallas_call(
        _mod_kernel,
        grid=(depth, n // tn),
        in_specs=[pl.BlockSpec((MOD_ROWS, d), lambda l, j: (0, 0)),
                  pl.BlockSpec((1, d, tn), lambda l, j: (l, 0, j)),
                  pl.BlockSpec((1, 1, tn), lambda l, j: (l, 0, j))],
        out_specs=pl.BlockSpec((1, MOD_ROWS, tn), lambda l, j: (l, 0, j)),
        out_shape=jax.ShapeDtypeStruct((depth, MOD_ROWS, n), F32),
        compiler_params=_params(("arbitrary", "arbitrary")),
    )(cc, ada_w, ada_b.reshape(depth, 1, n))


def _inproj_kernel(x_ref, sh_ref, sc_ref, g_ref, w_ref, cos_ref, sin_ref, qg_ref, kg_ref, bd_ref, lbf_ref, lbb_ref,
                   qa_ref, qb_ref, k_ref, v_ref, hq_ref, kf_ref, lff_ref, kb_ref, lfb_ref, hi_ref, sg_ref):
    x = x_ref[0]
    xm = (_rms(x, g_ref[...]) * (1.0 + sc_ref[0]) + sh_ref[0]).astype(BF16)
    wd = NAT_WIDTH

    def proj(i):
        return _dot(xm, w_ref[:, i * wd:(i + 1) * wd])

    def head_norm(y, g):
        sq = y * y
        hi = sq.astype(BF16)
        lo = (sq - hi.astype(F32)).astype(BF16)
        ss = _dot(hi, bd_ref[...]) + _dot(lo, bd_ref[...])
        return y * lax.rsqrt(ss * (1.0 / NAT_HEAD_DIM) + NORM_EPS) * g

    cos = cos_ref[...]
    sin = sin_ref[...]
    lane = lax.broadcasted_iota(I32, cos.shape, 1)
    first = (lane % 32) < 16

    def rope(y):
        outs = []
        for gi in range(wd // LANES):
            yg = y[:, gi * LANES:(gi + 1) * LANES]
            partner = jnp.where(first, pltpu.roll(yg, LANES - 16, 1), pltpu.roll(yg, 16, 1))
            outs.append(yg * cos + partner * sin)
        return jnp.concatenate(outs, axis=-1)

    qf = head_norm(proj(0), qg_ref[...])
    qa_ref[0] = rope(qf).astype(BF16)
    qb_ref[0] = qf.astype(BF16)
    k_ref[0] = rope(head_norm(proj(1), kg_ref[...])).astype(BF16)
    v_ref[0] = proj(2).astype(BF16)
    hq_ref[0] = (_silu(proj(3)) * (HGRN_DIM ** -0.5)).astype(BF16)
    for i, lb_ref, kk_ref, lf_ref in ((4, lbf_ref, kf_ref, lff_ref), (5, lbb_ref, kb_ref, lfb_ref)):
        lb = lb_ref[...]
        f = lb + (1.0 - lb) * jax.nn.sigmoid(proj(i))
        kk_ref[0] = (1.0 - f).astype(BF16)
        lf_ref[0] = jnp.log(f)
    hi_ref[0] = proj(6).astype(BF16)
    sg_ref[0] = _silu(proj(7)).astype(BF16)


def _inproj(x, shift, scale, g, w_bf, cos, sin, qg, kg, bd, lbf, lbb, tm):
    b, t, d = x.shape
    wd = NAT_WIDTH
    tok = lambda bi, ti: (bi, ti, 0)
    vec = lambda bi, ti: (bi, 0, 0)
    const = lambda bi, ti: (0, 0)
    out_dtypes = (BF16, BF16, BF16, BF16, BF16, BF16, F32, BF16, F32, BF16, BF16)
    return pl.pallas_call(
        _inproj_kernel,
        grid=(b, t // tm),
        in_specs=[pl.BlockSpec((1, tm, d), tok), pl.BlockSpec((1, 1, d), vec), pl.BlockSpec((1, 1, d), vec),
                  pl.BlockSpec((1, d), const), pl.BlockSpec(w_bf.shape, const),
                  pl.BlockSpec((tm, LANES), lambda bi, ti: (ti, 0)), pl.BlockSpec((tm, LANES), lambda bi, ti: (ti, 0)),
                  pl.BlockSpec((1, wd), const), pl.BlockSpec((1, wd), const), pl.BlockSpec((wd, wd), const),
                  pl.BlockSpec((1, wd), const), pl.BlockSpec((1, wd), const)],
        out_specs=[pl.BlockSpec((1, tm, wd), tok)] * len(out_dtypes),
        out_shape=[jax.ShapeDtypeStruct((b, t, wd), dt) for dt in out_dtypes],
        compiler_params=_params(("arbitrary", "arbitrary"), VMEM_LIMIT),
    )(x, shift, scale, g, w_bf, cos, sin, qg, kg, bd, lbf, lbb)


def _nat_kernel(qa_ref, qb_ref, kp_ref, kc_ref, kn_ref, vp_ref, vc_ref, vn_ref, kx_ref, vx_ref, bias_ref,
                o_ref, kbuf, vbuf, *, rows, rb):
    i = pl.program_id(1)
    blk = rb * GRID_W
    for s, (kr, vr) in enumerate(((kp_ref, vp_ref), (kc_ref, vc_ref), (kn_ref, vn_ref))):
        kbuf[s * blk:(s + 1) * blk, :] = kr[0]
        vbuf[s * blk:(s + 1) * blk, :] = vr[0]
    lane = lax.broadcasted_iota(I32, (GRID_W, LANES), 1)
    kwin = NAT_KH * GRID_W

    def row_body(j, carry):
        r = i * rb + j
        rs = jnp.clip(r - NAT_KH // 2, 0, rows - NAT_KH)
        start = pl.multiple_of((rs - i * rb + rb) * GRID_W, GRID_W)
        cls = rs - r + NAT_KH - 1
        q0 = pl.multiple_of(j * GRID_W, GRID_W)
        for p in range(NAT_HEADS // 2):
            ls = slice(p * LANES, (p + 1) * LANES)
            k2 = kbuf[pl.ds(start, kwin), ls]
            v2 = vbuf[pl.ds(start, kwin), ls]
            kx2 = kx_ref[0, :, ls]
            vx2 = vx_ref[0, :, ls]
            q2 = qa_ref[0, pl.ds(q0, GRID_W), ls]
            qf2 = qb_ref[0, pl.ds(q0, GRID_W), ls]
            outs = []
            for hh in range(2):
                sel = (lane // NAT_HEAD_DIM) == hh
                qm = jnp.where(sel, q2, jnp.zeros_like(q2))
                qfm = jnp.where(sel, qf2, jnp.zeros_like(qf2))
                s1 = _dot_nt(qm, k2) + bias_ref[cls, 2 * p + hh]
                s2 = _dot_nt(qfm, kx2)
                m = jnp.maximum(jnp.max(s1, axis=-1, keepdims=True), jnp.max(s2, axis=-1, keepdims=True))
                p1 = jnp.exp(s1 - m)
                p2 = jnp.exp(s2 - m)
                den = jnp.sum(p1, axis=-1, keepdims=True) + jnp.sum(p2, axis=-1, keepdims=True)
                o = _dot(p1.astype(BF16), v2) + _dot(p2.astype(BF16), vx2)
                outs.append(o / den)
            o_ref[0, pl.ds(q0, GRID_W), ls] = jnp.where(lane < NAT_HEAD_DIM, outs[0], outs[1]).astype(BF16)
        return carry

    lax.fori_loop(0, rb, row_body, 0)


def _nat_attention(qa, qb, k, v, kx, vx, bias, rows):
    b, s, wd = qa.shape
    rb = NAT_KH
    blk = rb * GRID_W
    nb = rows // rb
    cur = lambda bi, i: (bi, i, 0)
    prv = lambda bi, i: (bi, jnp.maximum(i - 1, 0), 0)
    nxt = lambda bi, i: (bi, jnp.minimum(i + 1, nb - 1), 0)
    ctx = lambda bi, i: (bi, 0, 0)
    tile = lambda im: pl.BlockSpec((1, blk, wd), im)
    return pl.pallas_call(
        functools.partial(_nat_kernel, rows=rows, rb=rb),
        grid=(b, nb),
        in_specs=[tile(cur), tile(cur), tile(prv), tile(cur), tile(nxt), tile(prv), tile(cur), tile(nxt),
                  pl.BlockSpec((1,) + kx.shape[1:], ctx), pl.BlockSpec((1,) + vx.shape[1:], ctx),
                  pl.BlockSpec(bias.shape, lambda bi, i: (0, 0, 0, 0))],
        out_specs=tile(cur),
        out_shape=jax.ShapeDtypeStruct((b, s, wd), BF16),
        scratch_shapes=[pltpu.VMEM((3 * blk, wd), BF16), pltpu.VMEM((3 * blk, wd), BF16)],
        compiler_params=_params(("arbitrary", "arbitrary"), VMEM_LIMIT),
    )(qa, qb, k, k, k, v, v, v, kx, vx, bias)


def _nat_bias_table(rpb):
    qc = np.arange(GRID_W)
    wc = np.clip(qc - NAT_KW // 2, 0, GRID_W - NAT_KW)
    kc = np.arange(GRID_W)
    valid = (kc[None, :] >= wc[:, None]) & (kc[None, :] < wc[:, None] + NAT_KW)
    cidx = np.clip(kc[None, :] - qc[:, None] + NAT_KW - 1, 0, 2 * NAT_KW - 2)
    ridx = np.arange(NAT_KH)[:, None] + np.arange(NAT_KH)[None, :]
    tab = rpb.astype(F32)[:, ridx][:, :, :, cidx]
    tab = jnp.where(valid[None, None, None], tab, MASK_VALUE)
    tab = tab.transpose(1, 0, 3, 2, 4)
    return tab.reshape(NAT_KH, NAT_HEADS, GRID_W, NAT_KH * GRID_W)


def _hgrn_kernel(*refs, rev, with_out, cb):
    if with_out:
        q_ref, kk_ref, lf_ref, v_ref, s0_ref, tri_ref, o_ref, sf_ref, st = refs
    else:
        kk_ref, lf_ref, v_ref, s0_ref, tri_ref, sf_ref, st = refs
    i = pl.program_id(1)
    c_sz = HGRN_CHUNK

    @pl.when(i == 0)
    def _():
        st[...] = s0_ref[0]

    tri = tri_ref[...]
    keep = tri > 0.5

    def chunk(ci, carry):
        c = (cb - 1 - ci) if rev else ci
        r0 = pl.multiple_of(c * c_sz, c_sz)
        bsum = jnp.dot(tri, lf_ref[0, pl.ds(r0, c_sz), :], precision=HIGHEST, preferred_element_type=F32)
        for h in range(HGRN_HEADS):
            ls = slice(h * HGRN_DIM, (h + 1) * HGRN_DIM)
            bh = bsum[:, ls]
            tot = bh[0:1] if rev else bh[c_sz - 1:c_sz]
            kh = kk_ref[0, pl.ds(r0, c_sz), ls].astype(F32)
            vh = v_ref[0, pl.ds(r0, c_sz), ls]
            state = st[h]
            if with_out:
                qh = q_ref[0, pl.ds(r0, c_sz), ls].astype(F32)
                blocks = []
                for sb in range(c_sz // HGRN_SUB):
                    lo = sb * HGRN_SUB
                    ref_row = lo + HGRN_SUB - 1 if rev else lo
                    cref = bh[ref_row:ref_row + 1]
                    f1 = jnp.exp(bh[lo:lo + HGRN_SUB] - cref)
                    f2 = jnp.exp(jnp.minimum(cref - bh, EXP_CLAMP))
                    blocks.append(_dot_nt((qh[lo:lo + HGRN_SUB] * f1).astype(BF16), (kh * f2).astype(BF16)))
                scores = jnp.where(keep, jnp.concatenate(blocks, axis=0), 0.0)
                o = _dot(scores.astype(BF16), vh) + _dot_nt((qh * jnp.exp(bh)).astype(BF16), state.astype(BF16))
                o_ref[0, pl.ds(r0, c_sz), ls] = o.astype(BF16)
            kd = (kh * jnp.exp(tot - bh)).astype(BF16)
            vt = vh.astype(F32).T.astype(BF16)
            st[h] = state * jnp.exp(tot) + _dot(vt, kd)
        return carry

    lax.fori_loop(0, cb, chunk, 0)

    @pl.when(i == pl.num_programs(1) - 1)
    def _():
        sf_ref[0] = st[...]


def _hgrn_scan(q, kk, lf, v, s0, tri, rev, with_out):
    b, t, wd = kk.shape
    nchunk = t // HGRN_CHUNK
    cb = min(8, nchunk)
    nblk = nchunk // cb
    tb = cb * HGRN_CHUNK
    tok = (lambda bi, i: (bi, nblk - 1 - i, 0)) if rev else (lambda bi, i: (bi, i, 0))
    st_map = lambda bi, i: (bi, 0, 0, 0)
    st_shape = (b, HGRN_HEADS, HGRN_DIM, HGRN_DIM)
    tile = pl.BlockSpec((1, tb, wd), tok)
    st_spec = pl.BlockSpec((1,) + st_shape[1:], st_map)
    ins = ([q] if with_out else []) + [kk, lf, v, s0, tri]
    in_specs = [tile] * (len(ins) - 2) + [st_spec, pl.BlockSpec(tri.shape, lambda bi, i: (0, 0))]
    out_specs = ([tile] if with_out else []) + [st_spec]
    out_shape = ([jax.ShapeDtypeStruct((b, t, wd), BF16)] if with_out else []) + [jax.ShapeDtypeStruct(st_shape, F32)]
    res = pl.pallas_call(
        functools.partial(_hgrn_kernel, rev=rev, with_out=with_out, cb=cb),
        grid=(b, nblk),
        in_specs=in_specs, out_specs=out_specs, out_shape=out_shape,
        scratch_shapes=[pltpu.VMEM(st_shape[1:], F32)],
        compiler_params=_params(("arbitrary", "arbitrary"), VMEM_LIMIT),
    )(*ins)
    return res if with_out else (None, res[0])


def _pack_bf16_pairs(x):
    half = x.shape[-1] // 2
    lo = pltpu.bitcast(x[:, :half].astype(BF16).astype(F32), U32) >> 16
    hi = pltpu.bitcast(x[:, half:].astype(BF16).astype(F32), U32) & jnp.uint32(0xFFFF0000)
    return hi | lo


def _unpack_bf16_pairs(w):
    lo = pltpu.bitcast(w << 16, F32)
    hi = pltpu.bitcast(w & jnp.uint32(0xFFFF0000), F32)
    return lo, hi


def _route_tail(h1, g2_ref, sh2_ref, sc2_ref, rwt_ref, rb_ref, su_ref, cnt,
                h1_ref, xw_ref, idx_ref, rank_ref, wgt_ref, cnt_ref):
    first = jnp.logical_and(pl.program_id(0) == 0, pl.program_id(1) == 0)

    @pl.when(first)
    def _():
        cnt[...] = jnp.zeros_like(cnt)

    h1_ref[0] = h1
    xm2 = _rms(h1, g2_ref[...]) * (1.0 + sc2_ref[0]) + sh2_ref[0]
    xw_ref[...] = _pack_bf16_pairs(xm2)
    logits = lax.dot_general(rwt_ref[...], xm2, NT, precision=HIGHEST, preferred_element_type=F32) + rb_ref[...]
    ne, tm = logits.shape
    eid = lax.broadcasted_iota(I32, (ne, tm), 0).astype(F32)
    vals, hots = [], []
    for k in range(TOP_K):
        m = jnp.max(logits, axis=0, keepdims=True)
        sel = jnp.min(jnp.where(logits == m, eid, float(ne)), axis=0, keepdims=True)
        hot = eid == sel
        logits = jnp.where(hot, -jnp.inf, logits)
        idx_ref[k:k + 1, :] = sel.astype(I32)
        vals.append(m)
        hots.append(hot)
    es = [jnp.exp(vv - vals[0]) for vv in vals]
    den = es[0] + es[1] + es[2] + es[3]
    onehot = jnp.zeros((ne, tm), F32)
    for k in range(TOP_K):
        wgt_ref[k:k + 1, :] = es[k] / den
        onehot = onehot + hots[k].astype(F32)
    before = _dot(onehot.astype(BF16), su_ref[...]) + cnt[...]
    for k in range(TOP_K):
        rank_ref[k:k + 1, :] = jnp.sum(jnp.where(hots[k], before, 0.0), axis=0, keepdims=True).astype(I32)
    cnt[...] = cnt[...] + jnp.sum(onehot, axis=1, keepdims=True)
    cnt_ref[...] = jnp.broadcast_to(cnt[...], cnt_ref.shape)


def _tail_specs(b, s, d, tm):
    nt = s // tm
    n = b * s
    vec = lambda bi, ti: (bi, 0, 0)
    const = lambda bi, ti: (0, 0)
    in_specs = [pl.BlockSpec((1, d), const), pl.BlockSpec((1, 1, d), vec), pl.BlockSpec((1, 1, d), vec),
                pl.BlockSpec((N_EXPERTS, d), const), pl.BlockSpec((N_EXPERTS, 1), const), pl.BlockSpec((tm, tm), const)]
    flat = lambda bi, ti: (0, bi * nt + ti)
    out_specs = [pl.BlockSpec((1, tm, d), lambda bi, ti: (bi, ti, 0)),
                 pl.BlockSpec((tm, d // 2), lambda bi, ti: (bi * nt + ti, 0)),
                 pl.BlockSpec((TOP_K, tm), flat), pl.BlockSpec((TOP_K, tm), flat), pl.BlockSpec((TOP_K, tm), flat),
                 pl.BlockSpec((N_EXPERTS, LANES), const)]
    out_shape = [jax.ShapeDtypeStruct((b, s, d), F32), jax.ShapeDtypeStruct((n, d // 2), U32),
                 jax.ShapeDtypeStruct((TOP_K, n), I32), jax.ShapeDtypeStruct((TOP_K, n), I32),
                 jax.ShapeDtypeStruct((TOP_K, n), F32), jax.ShapeDtypeStruct((N_EXPERTS, LANES), F32)]
    return in_specs, out_specs, out_shape


def _outproj_kernel(nat_ref, of_ref, ob_ref, sg_ref, x_ref, gate_ref, on_ref, w_ref,
                    g2_ref, sh2_ref, sc2_ref, rwt_ref, rb_ref, su_ref,
                    h1_ref, xw_ref, idx_ref, rank_ref, wgt_ref, cnt_ref, cnt):
    o = of_ref[0].astype(F32) + ob_ref[0].astype(F32)
    gated = []
    for h in range(HGRN_HEADS):
        ls = slice(h * HGRN_DIM, (h + 1) * HGRN_DIM)
        gated.append(_rms(o[:, ls], on_ref[...]) * sg_ref[0, :, ls].astype(F32))
    gated = jnp.concatenate(gated, axis=-1).astype(BF16)
    y = _dot(nat_ref[0], w_ref[:NAT_WIDTH, :]) + _dot(gated, w_ref[NAT_WIDTH:, :])
    h1 = x_ref[0] + gate_ref[0] * y
    _route_tail(h1, g2_ref, sh2_ref, sc2_ref, rwt_ref, rb_ref, su_ref, cnt,
                h1_ref, xw_ref, idx_ref, rank_ref, wgt_ref, cnt_ref)


def _outproj_route(nat, of, ob, sg, x, gate, on_g, w_bf, tail_args, tm):
    b, s, d = x.shape
    tok = lambda bi, ti: (bi, ti, 0)
    vec = lambda bi, ti: (bi, 0, 0)
    const = lambda bi, ti: (0, 0)
    t_in, out_specs, out_shape = _tail_specs(b, s, d, tm)
    wide = pl.BlockSpec((1, tm, NAT_WIDTH), tok)
    return pl.pallas_call(
        _outproj_kernel,
        grid=(b, s // tm),
        in_specs=[wide, wide, wide, wide, pl.BlockSpec((1, tm, d), tok), pl.BlockSpec((1, 1, d), vec),
                  pl.BlockSpec((1, HGRN_DIM), const), pl.BlockSpec(w_bf.shape, const)] + t_in,
        out_specs=out_specs, out_shape=out_shape,
        scratch_shapes=[pltpu.VMEM((N_EXPERTS, 1), F32)],
        compiler_params=_params(("arbitrary", "arbitrary"), VMEM_LIMIT),
    )(nat, of, ob, sg, x, gate, on_g, w_bf, *tail_args)


def _conf_in_kernel(x_ref, sh_ref, sc_ref, g_ref, w_ref, b_ref, u_ref):
    d = x_ref.shape[-1]
    xm = (_rms(x_ref[0], g_ref[...]) * (1.0 + sc_ref[0]) + sh_ref[0]).astype(BF16)
    a = _dot(xm, w_ref[:, :d]) + b_ref[:, :d]
    gate = _dot(xm, w_ref[:, d:]) + b_ref[:, d:]
    u_ref[0] = (a * jax.nn.sigmoid(gate)).astype(BF16)


def _conf_in(x, shift, scale, g, w_bf, b1, tm):
    b, s, d = x.shape
    tok = lambda bi, ti: (bi, ti, 0)
    vec = lambda bi, ti: (bi, 0, 0)
    const = lambda bi, ti: (0, 0)
    return pl.pallas_call(
        _conf_in_kernel,
        grid=(b, s // tm),
        in_specs=[pl.BlockSpec((1, tm, d), tok), pl.BlockSpec((1, 1, d), vec), pl.BlockSpec((1, 1, d), vec),
                  pl.BlockSpec((1, d), const), pl.BlockSpec(w_bf.shape, const), pl.BlockSpec((1, 2 * d), const)],
        out_specs=pl.BlockSpec((1, tm, d), tok),
        out_shape=jax.ShapeDtypeStruct((b, s, d), BF16),
        compiler_params=_params(("arbitrary", "arbitrary"), VMEM_LIMIT),
    )(x, shift, scale, g, w_bf, b1)


def _conf_out_kernel(up_ref, uc_ref, un_ref, dw_ref, dwb_ref, lg_ref, lb_ref, w_ref, b2_ref, x_ref, gate_ref,
                     g2_ref, sh2_ref, sc2_ref, rwt_ref, rb_ref, su_ref,
                     h1_ref, xw_ref, idx_ref, rank_ref, wgt_ref, cnt_ref, cnt, ubuf, cbuf, *, rc):
    ti = pl.program_id(1)
    tm, d = uc_ref.shape[1], uc_ref.shape[2]
    hal = CONV_HALO
    prev = up_ref[0].astype(F32)
    nxt = un_ref[0].astype(F32)
    ubuf[0:hal, :] = jnp.where(ti > 0, prev, jnp.zeros_like(prev))
    ubuf[hal:hal + tm, :] = uc_ref[0].astype(F32)
    ubuf[hal + tm:, :] = jnp.where(ti < pl.num_programs(1) - 1, nxt, jnp.zeros_like(nxt))
    base = hal - CONV_WIDTH // 2
    lc = 2 * LANES
    sub = 8
    span = rc + ((base + CONV_WIDTH - 1) // sub) * sub

    def conv_rows(ci, carry):
        r0 = pl.multiple_of(ci * rc, rc)
        for l0 in range(0, d, lc):
            win = ubuf[pl.ds(r0, rc + 2 * hal), l0:l0 + lc]
            acc = jnp.zeros((rc, lc), F32) + dwb_ref[:, l0:l0 + lc]
            for r in range(sub):
                taps = [j for j in range(CONV_WIDTH) if (base + j) % sub == r]
                shifted = win[r:r + span]
                for j in taps:
                    a = (base + j) // sub * sub
                    acc = acc + shifted[a:a + rc] * dw_ref[j:j + 1, l0:l0 + lc]
            cbuf[pl.ds(r0, rc), l0:l0 + lc] = acc
        return carry

    lax.fori_loop(0, tm // rc, conv_rows, 0)
    c = cbuf[...]
    mu = jnp.mean(c, axis=-1, keepdims=True)
    cz = c - mu
    var = jnp.mean(cz * cz, axis=-1, keepdims=True)
    y = _silu(cz * lax.rsqrt(var + NORM_EPS) * lg_ref[...] + lb_ref[...]).astype(BF16)
    y = _dot(y, w_ref[...]) + b2_ref[...]
    h1 = x_ref[0] + gate_ref[0] * y
    _route_tail(h1, g2_ref, sh2_ref, sc2_ref, rwt_ref, rb_ref, su_ref, cnt,
                h1_ref, xw_ref, idx_ref, rank_ref, wgt_ref, cnt_ref)


def _conf_out_route(u, dw, dwb, ln_g, ln_b, w_bf, b2, x, gate, tail_args, tm):
    b, s, d = x.shape
    hal = CONV_HALO
    per = tm // hal
    nh = s // hal
    tok = lambda bi, ti: (bi, ti, 0)
    vec = lambda bi, ti: (bi, 0, 0)
    const = lambda bi, ti: (0, 0)
    t_in, out_specs, out_shape = _tail_specs(b, s, d, tm)
    return pl.pallas_call(
        functools.partial(_conf_out_kernel, rc=32),
        grid=(b, s // tm),
        in_specs=[pl.BlockSpec((1, hal, d), lambda bi, ti: (bi, jnp.maximum(ti * per - 1, 0), 0)),
                  pl.BlockSpec((1, tm, d), tok),
                  pl.BlockSpec((1, hal, d), lambda bi, ti: (bi, jnp.minimum((ti + 1) * per, nh - 1), 0)),
                  pl.BlockSpec(dw.shape, const), pl.BlockSpec((1, d), const), pl.BlockSpec((1, d), const),
                  pl.BlockSpec((1, d), const), pl.BlockSpec(w_bf.shape, const), pl.BlockSpec((1, d), const),
                  pl.BlockSpec((1, tm, d), tok), pl.BlockSpec((1, 1, d), vec)] + t_in,
        out_specs=out_specs, out_shape=out_shape,
        scratch_shapes=[pltpu.VMEM((N_EXPERTS, 1), F32), pltpu.VMEM((tm + 2 * hal, d), F32), pltpu.VMEM((tm, d), F32)],
        compiler_params=_params(("arbitrary", "arbitrary"), VMEM_LIMIT),
    )(u, u, u, dw, dwb, ln_g, ln_b, w_bf, b2, x, gate, *tail_args)


def _moe_scatter_kernel(slot_ref, xw_ref, xs_in_ref, xs_ref, sem, *, ts):
    del xs_in_ref

    def row_copy(n, k):
        return pltpu.make_async_copy(xw_ref.at[pl.ds(n, 1), :], xs_ref.at[pl.ds(slot_ref[0, k, n], 1), :], sem)

    def start(n, carry):
        for k in range(TOP_K):
            row_copy(n, k).start()
        return carry

    def wait(n, carry):
        for k in range(TOP_K):
            row_copy(n, k).wait()
        return carry

    lax.fori_loop(0, ts, start, 0)
    lax.fori_loop(0, ts, wait, 0)


def _moe_scatter(slot3, xw, xs_zero, ts):
    n, d2 = xw.shape
    return pl.pallas_call(
        functools.partial(_moe_scatter_kernel, ts=ts),
        grid=(n // ts,),
        in_specs=[pl.BlockSpec((1, TOP_K, ts), lambda i: (i, 0, 0), memory_space=pltpu.SMEM),
                  pl.BlockSpec((ts, d2), lambda i: (i, 0)),
                  pl.BlockSpec(memory_space=pl.ANY)],
        out_specs=pl.BlockSpec(memory_space=pl.ANY),
        out_shape=jax.ShapeDtypeStruct(xs_zero.shape, U32),
        scratch_shapes=[pltpu.SemaphoreType.DMA],
        input_output_aliases={2: 0},
        compiler_params=_params(("arbitrary",)),
    )(slot3, xw, xs_zero)


def _moe_expert_kernel(te_ref, xs_ref, w1_ref, b1_ref, w2_ref, b2_ref, ys_ref, w1b, w2b):
    i = pl.program_id(0)
    f = w2_ref.shape[1]
    changed = jnp.logical_or(i == 0, te_ref[i] != te_ref[jnp.maximum(i - 1, 0)])

    @pl.when(changed)
    def _():
        w1b[...] = w1_ref[0].astype(BF16)
        w2b[...] = w2_ref[0].astype(BF16)

    lo, hi = _unpack_bf16_pairs(xs_ref[...])
    x = jnp.concatenate([lo, hi], axis=-1).astype(BF16)
    y = jnp.zeros((x.shape[0], w2_ref.shape[2]), F32) + b2_ref[0]
    fc = 512 if f % 512 == 0 else f
    for c in range(f // fc):
        glu = _dot(x, w1b[:, c * fc:(c + 1) * fc]) + b1_ref[0, :, c * fc:(c + 1) * fc]
        lin = _dot(x, w1b[:, f + c * fc:f + (c + 1) * fc]) + b1_ref[0, :, f + c * fc:f + (c + 1) * fc]
        glu = jnp.minimum(glu, SWIGLU_LIMIT)
        lin = jnp.clip(lin, -SWIGLU_LIMIT, SWIGLU_LIMIT)
        act = glu * jax.nn.sigmoid(SWIGLU_ALPHA * glu) * (lin + 1.0)
        y = y + _dot(act.astype(BF16), w2b[c * fc:(c + 1) * fc, :])
    ys_ref[...] = _pack_bf16_pairs(y)


def _moe_experts(tile_expert, xs, w1, b1, w2, b2, tme):
    p, d2 = xs.shape
    ne, d, f2 = w1.shape
    f = f2 // 2
    grid_spec = pltpu.PrefetchScalarGridSpec(
        num_scalar_prefetch=1,
        grid=(p // tme,),
        in_specs=[pl.BlockSpec((tme, d2), lambda i, te: (i, 0)),
                  pl.BlockSpec((1, d, f2), lambda i, te: (te[i], 0, 0)),
                  pl.BlockSpec((1, 1, f2), lambda i, te: (te[i], 0, 0)),
                  pl.BlockSpec((1, f, d), lambda i, te: (te[i], 0, 0)),
                  pl.BlockSpec((1, 1, d), lambda i, te: (te[i], 0, 0))],
        out_specs=pl.BlockSpec((tme, d2), lambda i, te: (i, 0)),
        scratch_shapes=[pltpu.VMEM((d, f2), BF16), pltpu.VMEM((f, d), BF16)],
    )
    return pl.pallas_call(
        _moe_expert_kernel,
        grid_spec=grid_spec,
        out_shape=jax.ShapeDtypeStruct((p, d2), U32),
        compiler_params=_params(("arbitrary",), VMEM_LIMIT),
    )(tile_expert, xs, w1, b1.reshape(ne, 1, f2), w2, b2.reshape(ne, 1, d))


def _moe_combine_kernel(slot_ref, ys_ref, wt_ref, h_ref, gate_ref, o_ref, buf, sem, *, tc):
    def row_copy(n, k):
        return pltpu.make_async_copy(ys_ref.at[pl.ds(slot_ref[0, k, n], 1), :], buf.at[k, pl.ds(n, 1), :], sem)

    def start(n, carry):
        for k in range(TOP_K):
            row_copy(n, k).start()
        return carry

    def wait(n, carry):
        for k in range(TOP_K):
            row_copy(n, k).wait()
        return carry

    lax.fori_loop(0, tc, start, 0)
    lax.fori_loop(0, tc, wait, 0)
    wt = wt_ref[...]
    acc_lo = acc_hi = None
    for k in range(TOP_K):
        lo, hi = _unpack_bf16_pairs(buf[k])
        wk = wt[:, k:k + 1]
        acc_lo = lo * wk if acc_lo is None else acc_lo + lo * wk
        acc_hi = hi * wk if acc_hi is None else acc_hi + hi * wk
    o_ref[0] = h_ref[0] + gate_ref[0] * jnp.concatenate([acc_lo, acc_hi], axis=-1)


def _moe_combine(slot3, ys, wt, h, gate, tc):
    b, s, d = h.shape
    nt = s // tc
    return pl.pallas_call(
        functools.partial(_moe_combine_kernel, tc=tc),
        grid=(b, nt),
        in_specs=[pl.BlockSpec((1, TOP_K, tc), lambda bi, ti: (bi * nt + ti, 0, 0), memory_space=pltpu.SMEM),
                  pl.BlockSpec(memory_space=pl.ANY),
                  pl.BlockSpec((tc, TOP_K), lambda bi, ti: (bi * nt + ti, 0)),
                  pl.BlockSpec((1, tc, d), lambda bi, ti: (bi, ti, 0)),
                  pl.BlockSpec((1, 1, d), lambda bi, ti: (bi, 0, 0))],
        out_specs=pl.BlockSpec((1, tc, d), lambda bi, ti: (bi, ti, 0)),
        out_shape=jax.ShapeDtypeStruct((b, s, d), F32),
        scratch_shapes=[pltpu.VMEM((TOP_K, tc, d // 2), U32), pltpu.SemaphoreType.DMA],
        compiler_params=_params(("arbitrary", "arbitrary"), VMEM_LIMIT),
    )(slot3, ys, wt, h, gate)


def _moe(h1, xw, idx, rank, wgt, counts, gate, w1, b1, w2, b2, tme, tmv):
    n, d2 = xw.shape
    cnt = counts[:, 0].astype(I32)
    cap = ((cnt + tme - 1) // tme) * tme
    ends = jnp.cumsum(cap)
    slot = (ends - cap)[idx] + rank
    p = TOP_K * n + N_EXPERTS * tme
    tile_expert = jnp.minimum(jnp.searchsorted(ends // tme, jnp.arange(p // tme, dtype=I32), side='right'),
                              N_EXPERTS - 1).astype(I32)
    slot3 = slot.reshape(TOP_K, n // tmv, tmv).transpose(1, 0, 2)
    xs = _moe_scatter(slot3, xw, jnp.zeros((p, d2), U32), tmv)
    ys = _moe_experts(tile_expert, xs, w1, b1, w2, b2, tme)
    return _moe_combine(slot3, ys, wgt.T, h1, gate, tmv)


def _rope_tables(s):
    pos = np.arange(s)
    lane = np.arange(LANES)
    dd = lane % NAT_HEAD_DIM
    n = NAT_HEAD_DIM // 4
    inv_freq = ROPE_BASE ** (-(dd % n).astype(np.float64) / n)
    p = np.where((dd // (NAT_HEAD_DIM // 2))[None, :] == 0, (pos // GRID_W)[:, None], (pos % GRID_W)[:, None])
    ang = (p.astype(np.float32) * inv_freq.astype(np.float32)[None, :]).astype(np.float32)
    sign = np.where((dd % (2 * n)) < n, -1.0, 1.0)[None, :]
    return jnp.asarray(np.cos(ang), F32), jnp.asarray(np.sin(ang) * sign, F32)


def _tile(n, want):
    t = min(want, n)
    while n % t:
        t //= 2
    return t


def kernel(x, c, ctx, c_ctx, ada_w, ada_b, norm1_g, norm2_g, ab_w_in, ab_w_out, nat_q_norm, nat_k_norm, nat_rpb, hgrn_lb, hgrn_o_norm, conv_w1, conv_b1, conv_dw, conv_dw_b, conv_ln_g, conv_ln_b, conv_w2, conv_b2, router_w, router_b, moe_w1, moe_b1, moe_w2, moe_b2):
    b, s, d = x.shape
    l = ctx.shape[1]
    rows = s // GRID_W
    assert ada_w.shape[0] == 2 and b < MOD_ROWS and rows >= NAT_KH and rows % NAT_KH == 0
    assert s % HGRN_CHUNK == 0 and l % HGRN_CHUNK == 0 and d % (2 * LANES) == 0
    tm = _tile(s, 512)
    tmv = _tile(s, 256)
    tme = 512 if TOP_K * b * s >= 512 * N_EXPERTS else 128

    cc = jnp.zeros((MOD_ROWS, d), F32).at[:b].set(c).at[b].set(c_ctx)
    mod = _modulation(cc, ada_w, ada_b)

    def mod_vec(layer, i):
        return mod[layer, :b, i * d:(i + 1) * d].reshape(b, 1, d)

    def ctx_vec(i):
        return jnp.broadcast_to(mod[0, b, i * d:(i + 1) * d].reshape(1, 1, d), (b, 1, d))

    strict_upper = jnp.asarray(np.triu(np.ones((tm, tm), np.float32), 1), BF16)

    def tail_args(layer):
        return (norm2_g[layer].reshape(1, d), mod_vec(layer, 3), mod_vec(layer, 4),
                router_w[layer].T, router_b[layer].reshape(N_EXPERTS, 1), strict_upper)

    lb_all = jnp.cumsum(jax.nn.softmax(hgrn_lb.astype(F32), axis=1), axis=1)[:, 0]
    w_in = ab_w_in[0].astype(BF16)
    scale = NAT_HEAD_DIM ** -0.5
    qg = jnp.tile(nat_q_norm[0] * scale, NAT_HEADS).reshape(1, NAT_WIDTH)
    kg = jnp.tile(nat_k_norm[0], NAT_HEADS).reshape(1, NAT_WIDTH)
    head_of = np.arange(NAT_WIDTH) // NAT_HEAD_DIM
    bd = jnp.asarray(head_of[:, None] == head_of[None, :], BF16)
    lbf, lbb = lb_all[0].reshape(1, HGRN_WIDTH), lb_all[1].reshape(1, HGRN_WIDTH)
    cos, sin = _rope_tables(s)
    g1 = norm1_g[0].reshape(1, d)
    lat = _inproj(x, mod_vec(0, 0), mod_vec(0, 1), g1, w_in, cos, sin, qg, kg, bd, lbf, lbb, tm)
    qa, qb, k, v, hq, kf, lff, kb, lfb, hi, sg = lat
    tl = _tile(l, 256)
    cxt = _inproj(ctx, ctx_vec(0), ctx_vec(1), g1, w_in, jnp.ones((l, LANES), F32), jnp.zeros((l, LANES), F32),
                  qg, kg, bd, lbf, lbb, tl)
    _, _, kx, vx, _, kfx, lffx, kbx, lfbx, hix, _ = cxt

    nat = _nat_attention(qa, qb, k, v, kx, vx, _nat_bias_table(nat_rpb[0]), rows)

    tri_f = jnp.asarray(np.tril(np.ones((HGRN_CHUNK, HGRN_CHUNK), np.float32)))
    s0 = jnp.zeros((b, HGRN_HEADS, HGRN_DIM, HGRN_DIM), F32)
    _, sf = _hgrn_scan(None, kfx, lffx, hix, s0, tri_f, False, False)
    of, _ = _hgrn_scan(hq, kf, lff, hi, sf, tri_f, False, True)
    _, sb = _hgrn_scan(None, kbx, lfbx, hix, s0, tri_f.T, True, False)
    ob, _ = _hgrn_scan(hq, kb, lfb, hi, sb, tri_f.T, True, True)

    h1, xw, idx, rank, wgt, counts = _outproj_route(
        nat, of, ob, sg, x, mod_vec(0, 2), hgrn_o_norm[0].reshape(1, HGRN_DIM), ab_w_out[0].astype(BF16),
        tail_args(0), tm)
    h = _moe(h1, xw, idx, rank, wgt, counts, mod_vec(0, 5), moe_w1[0], moe_b1[0], moe_w2[0], moe_b2[0], tme, tmv)

    u = _conf_in(h, mod_vec(1, 0), mod_vec(1, 1), norm1_g[1].reshape(1, d), conv_w1[0].astype(BF16),
                 conv_b1[0].reshape(1, 2 * d), tm)
    h1, xw, idx, rank, wgt, counts = _conf_out_route(
        u, conv_dw[0], conv_dw_b[0].reshape(1, d), conv_ln_g[0].reshape(1, d), conv_ln_b[0].reshape(1, d),
        conv_w2[0].astype(BF16), conv_b2[0].reshape(1, d), h, mod_vec(1, 2), tail_args(1), tm)
    return _moe(h1, xw, idx, rank, wgt, counts, mod_vec(1, 5), moe_w1[1], moe_b1[1], moe_w2[1], moe_b2[1], tme, tmv)
```

```python
import functools

import numpy as np
import jax
import jax.numpy as jnp
from jax import lax
from jax.experimental import pallas as pl
from jax.experimental.pallas import tpu as pltpu

F32 = jnp.float32
BF16 = jnp.bfloat16
U32 = jnp.uint32
I32 = jnp.int32
HIGHEST = lax.Precision.HIGHEST

GRID_W = 64
NAT_HEADS = 8
NAT_HEAD_DIM = 64
NAT_WIDTH = NAT_HEADS * NAT_HEAD_DIM
NAT_KH = 8
NAT_KW = 16
HGRN_HEADS = 4
HGRN_DIM = 128
HGRN_WIDTH = HGRN_HEADS * HGRN_DIM
HGRN_CHUNK = 64
HGRN_SUB = 16
CONV_WIDTH = 31
CONV_HALO = 16
N_EXPERTS = 32
TOP_K = 4
SWIGLU_LIMIT = 7.0
SWIGLU_ALPHA = 1.702
ROPE_BASE = 10000.0
NORM_EPS = 1e-6
MASK_VALUE = -1e30
EXP_CLAMP = 80.0

SUBLANES = 8
ROW_ALIGN = SUBLANES
RUN_CHUNK = 64
LANES = 128
MOD_ROWS = 16
VMEM_LIMIT = 56 * 1024 * 1024
NT = (((1,), (1,)), ((), ()))


def _params(sem, vmem=None):
    return pltpu.CompilerParams(dimension_semantics=sem, vmem_limit_bytes=vmem)


def _dot(a, b):
    return jnp.dot(a, b, preferred_element_type=F32)


def _dot_nt(a, b):
    return lax.dot_general(a, b, NT, preferred_element_type=F32)


def _silu(x):
    return x * jax.nn.sigmoid(x)


def _rms(x, g):
    return x * lax.rsqrt(jnp.mean(x * x, axis=-1, keepdims=True) + NORM_EPS) * g


def _mod_kernel(cc_ref, w_ref, b_ref, o_ref):
    cc = cc_ref[...]
    o_ref[0] = jnp.dot(_silu(cc), w_ref[0], precision=HIGHEST, preferred_element_type=F32) + b_ref[0]


def _modulation(cc, ada_w, ada_b):
    depth, d, n = ada_w.shape
    tn = n // 4
    return pl.pallas_call(
        _mod_kernel,
        grid=(depth, n // tn),
        in_specs=[pl.BlockSpec((MOD_ROWS, d), lambda l, j: (0, 0)),
                  pl.BlockSpec((1, d, tn), lambda l, j: (l, 0, j)),
                  pl.BlockSpec((1, 1, tn), lambda l, j: (l, 0, j))],
        out_specs=pl.BlockSpec((1, MOD_ROWS, tn), lambda l, j: (l, 0, j)),
        out_shape=jax.ShapeDtypeStruct((depth, MOD_ROWS, n), F32),
        compiler_params=_params(("arbitrary", "arbitrary")),
    )(cc, ada_w, ada_b.reshape(depth, 1, n))


def _inproj_kernel(x_ref, sh_ref, sc_ref, g_ref, w_ref, cos_ref, sin_ref, qg_ref, kg_ref, bd_ref, lbf_ref, lbb_ref,
                   qa_ref, qb_ref, k_ref, v_ref, hq_ref, kf_ref, lff_ref, kb_ref, lfb_ref, hi_ref, sg_ref):
    x = x_ref[0]
    xm = (_rms(x, g_ref[...]) * (1.0 + sc_ref[0]) + sh_ref[0]).astype(BF16)
    wd = NAT_WIDTH

    def proj(i):
        return _dot(xm, w_ref[:, i * wd:(i + 1) * wd])

    def head_norm(y, g):
        sq = y * y
        hi = sq.astype(BF16)
        lo = (sq - hi.astype(F32)).astype(BF16)
        ss = _dot(hi, bd_ref[...]) + _dot(lo, bd_ref[...])
        return y * lax.rsqrt(ss * (1.0 / NAT_HEAD_DIM) + NORM_EPS) * g

    cos = cos_ref[...]
    sin = sin_ref[...]
    lane = lax.broadcasted_iota(I32, cos.shape, 1)
    first = (lane % 32) < 16

    def rope(y):
        outs = []
        for gi in range(wd // LANES):
            yg = y[:, gi * LANES:(gi + 1) * LANES]
            partner = jnp.where(first, pltpu.roll(yg, LANES - 16, 1), pltpu.roll(yg, 16, 1))
            outs.append(yg * cos + partner * sin)
        return jnp.concatenate(outs, axis=-1)

    qf = head_norm(proj(0), qg_ref[...])
    qa_ref[0] = rope(qf).astype(BF16)
    qb_ref[0] = qf.astype(BF16)
    k_ref[0] = rope(head_norm(proj(1), kg_ref[...])).astype(BF16)
    v_ref[0] = proj(2).astype(BF16)
    hq_ref[0] = (_silu(proj(3)) * (HGRN_DIM ** -0.5)).astype(BF16)
    for i, lb_ref, kk_ref, lf_ref in ((4, lbf_ref, kf_ref, lff_ref), (5, lbb_ref, kb_ref, lfb_ref)):
        lb = lb_ref[...]
        f = lb + (1.0 - lb) * jax.nn.sigmoid(proj(i))
        kk_ref[0] = (1.0 - f).astype(BF16)
        lf_ref[0] = jnp.log(f)
    hi_ref[0] = proj(6).astype(BF16)
    sg_ref[0] = _silu(proj(7)).astype(BF16)


def _inproj(x, shift, scale, g, w_bf, cos, sin, qg, kg, bd, lbf, lbb, tm):
    b, t, d = x.shape
    wd = NAT_WIDTH
    tok = lambda bi, ti: (bi, ti, 0)
    vec = lambda bi, ti: (bi, 0, 0)
    const = lambda bi, ti: (0, 0)
    out_dtypes = (BF16, BF16, BF16, BF16, BF16, BF16, F32, BF16, F32, BF16, BF16)
    return pl.pallas_call(
        _inproj_kernel,
        grid=(b, t // tm),
        in_specs=[pl.BlockSpec((1, tm, d), tok), pl.BlockSpec((1, 1, d), vec), pl.BlockSpec((1, 1, d), vec),
                  pl.BlockSpec((1, d), const), pl.BlockSpec(w_bf.shape, const),
                  pl.BlockSpec((tm, LANES), lambda bi, ti: (ti, 0)), pl.BlockSpec((tm, LANES), lambda bi, ti: (ti, 0)),
                  pl.BlockSpec((1, wd), const), pl.BlockSpec((1, wd), const), pl.BlockSpec((wd, wd), const),
                  pl.BlockSpec((1, wd), const), pl.BlockSpec((1, wd), const)],
        out_specs=[pl.BlockSpec((1, tm, wd), tok)] * len(out_dtypes),
        out_shape=[jax.ShapeDtypeStruct((b, t, wd), dt) for dt in out_dtypes],
        compiler_params=_params(("arbitrary", "arbitrary"), VMEM_LIMIT),
    )(x, shift, scale, g, w_bf, cos, sin, qg, kg, bd, lbf, lbb)


def _nat_kernel(qa_ref, qb_ref, kp_ref, kc_ref, kn_ref, vp_ref, vc_ref, vn_ref, kx_ref, vx_ref, bias_ref,
                o_ref, kbuf, vbuf, *, rows, rb):
    i = pl.program_id(1)
    blk = rb * GRID_W
    for s, (kr, vr) in enumerate(((kp_ref, vp_ref), (kc_ref, vc_ref), (kn_ref, vn_ref))):
        kbuf[s * blk:(s + 1) * blk, :] = kr[0]
        vbuf[s * blk:(s + 1) * blk, :] = vr[0]
    lane = lax.broadcasted_iota(I32, (GRID_W, LANES), 1)
    kwin = NAT_KH * GRID_W

    def row_body(j, carry):
        r = i * rb + j
        rs = jnp.clip(r - NAT_KH // 2, 0, rows - NAT_KH)
        start = pl.multiple_of((rs - i * rb + rb) * GRID_W, GRID_W)
        cls = rs - r + NAT_KH - 1
        q0 = pl.multiple_of(j * GRID_W, GRID_W)
        for p in range(NAT_HEADS // 2):
            ls = slice(p * LANES, (p + 1) * LANES)
            k2 = kbuf[pl.ds(start, kwin), ls]
            v2 = vbuf[pl.ds(start, kwin), ls]
            kx2 = kx_ref[0, :, ls]
            vx2 = vx_ref[0, :, ls]
            q2 = qa_ref[0, pl.ds(q0, GRID_W), ls]
            qf2 = qb_ref[0, pl.ds(q0, GRID_W), ls]
            outs = []
            for hh in range(2):
                sel = (lane // NAT_HEAD_DIM) == hh
                qm = jnp.where(sel, q2, jnp.zeros_like(q2))
                qfm = jnp.where(sel, qf2, jnp.zeros_like(qf2))
                s1 = _dot_nt(qm, k2) + bias_ref[cls, 2 * p + hh]
                s2 = _dot_nt(qfm, kx2)
                m = jnp.maximum(jnp.max(s1, axis=-1, keepdims=True), jnp.max(s2, axis=-1, keepdims=True))
                p1 = jnp.exp(s1 - m)
                p2 = jnp.exp(s2 - m)
                den = jnp.sum(p1, axis=-1, keepdims=True) + jnp.sum(p2, axis=-1, keepdims=True)
                o = _dot(p1.astype(BF16), v2) + _dot(p2.astype(BF16), vx2)
                outs.append(o / den)
            o_ref[0, pl.ds(q0, GRID_W), ls] = jnp.where(lane < NAT_HEAD_DIM, outs[0], outs[1]).astype(BF16)
        return carry

    lax.fori_loop(0, rb, row_body, 0)


def _nat_attention(qa, qb, k, v, kx, vx, bias, rows):
    b, s, wd = qa.shape
    rb = NAT_KH
    blk = rb * GRID_W
    nb = rows // rb
    cur = lambda bi, i: (bi, i, 0)
    prv = lambda bi, i: (bi, jnp.maximum(i - 1, 0), 0)
    nxt = lambda bi, i: (bi, jnp.minimum(i + 1, nb - 1), 0)
    ctx = lambda bi, i: (bi, 0, 0)
    tile = lambda im: pl.BlockSpec((1, blk, wd), im)
    return pl.pallas_call(
        functools.partial(_nat_kernel, rows=rows, rb=rb),
        grid=(b, nb),
        in_specs=[tile(cur), tile(cur), tile(prv), tile(cur), tile(nxt), tile(prv), tile(cur), tile(nxt),
                  pl.BlockSpec((1,) + kx.shape[1:], ctx), pl.BlockSpec((1,) + vx.shape[1:], ctx),
                  pl.BlockSpec(bias.shape, lambda bi, i: (0, 0, 0, 0))],
        out_specs=tile(cur),
        out_shape=jax.ShapeDtypeStruct((b, s, wd), BF16),
        scratch_shapes=[pltpu.VMEM((3 * blk, wd), BF16), pltpu.VMEM((3 * blk, wd), BF16)],
        compiler_params=_params(("arbitrary", "arbitrary"), VMEM_LIMIT),
    )(qa, qb, k, k, k, v, v, v, kx, vx, bias)


def _nat_bias_table(rpb):
    qc = np.arange(GRID_W)
    wc = np.clip(qc - NAT_KW // 2, 0, GRID_W - NAT_KW)
    kc = np.arange(GRID_W)
    valid = (kc[None, :] >= wc[:, None]) & (kc[None, :] < wc[:, None] + NAT_KW)
    cidx = np.clip(kc[None, :] - qc[:, None] + NAT_KW - 1, 0, 2 * NAT_KW - 2)
    ridx = np.arange(NAT_KH)[:, None] + np.arange(NAT_KH)[None, :]
    tab = rpb.astype(F32)[:, ridx][:, :, :, cidx]
    tab = jnp.where(valid[None, None, None], tab, MASK_VALUE)
    tab = tab.transpose(1, 0, 3, 2, 4)
    return tab.reshape(NAT_KH, NAT_HEADS, GRID_W, NAT_KH * GRID_W)


def _hgrn_kernel(*refs, rev, with_out, cb):
    if with_out:
        q_ref, kk_ref, lf_ref, v_ref, s0_ref, tri_ref, o_ref, sf_ref, st = refs
    else:
        kk_ref, lf_ref, v_ref, s0_ref, tri_ref, sf_ref, st = refs
    i = pl.program_id(1)
    c_sz = HGRN_CHUNK

    @pl.when(i == 0)
    def _():
        st[...] = s0_ref[0]

    tri = tri_ref[...]
    keep = tri > 0.5

    def chunk(ci, carry):
        c = (cb - 1 - ci) if rev else ci
        r0 = pl.multiple_of(c * c_sz, c_sz)
        bsum = jnp.dot(tri, lf_ref[0, pl.ds(r0, c_sz), :], precision=HIGHEST, preferred_element_type=F32)
        for h in range(HGRN_HEADS):
            ls = slice(h * HGRN_DIM, (h + 1) * HGRN_DIM)
            bh = bsum[:, ls]
            tot = bh[0:1] if rev else bh[c_sz - 1:c_sz]
            kh = kk_ref[0, pl.ds(r0, c_sz), ls].astype(F32)
            vh = v_ref[0, pl.ds(r0, c_sz), ls]
            state = st[h]
            if with_out:
                qh = q_ref[0, pl.ds(r0, c_sz), ls].astype(F32)
                blocks = []
                for sb in range(c_sz // HGRN_SUB):
                    lo = sb * HGRN_SUB
                    ref_row = lo + HGRN_SUB - 1 if rev else lo
                    cref = bh[ref_row:ref_row + 1]
                    f1 = jnp.exp(bh[lo:lo + HGRN_SUB] - cref)
                    f2 = jnp.exp(jnp.minimum(cref - bh, EXP_CLAMP))
                    blocks.append(_dot_nt((qh[lo:lo + HGRN_SUB] * f1).astype(BF16), (kh * f2).astype(BF16)))
                scores = jnp.where(keep, jnp.concatenate(blocks, axis=0), 0.0)
                o = _dot(scores.astype(BF16), vh) + _dot_nt((qh * jnp.exp(bh)).astype(BF16), state.astype(BF16))
                o_ref[0, pl.ds(r0, c_sz), ls] = o.astype(BF16)
            kd = (kh * jnp.exp(tot - bh)).astype(BF16)
            vt = vh.astype(F32).T.astype(BF16)
            st[h] = state * jnp.exp(tot) + _dot(vt, kd)
        return carry

    lax.fori_loop(0, cb, chunk, 0, unroll=2)

    @pl.when(i == pl.num_programs(1) - 1)
    def _():
        sf_ref[0] = st[...]


def _hgrn_scan(q, kk, lf, v, s0, tri, rev, with_out):
    b, t, wd = kk.shape
    nchunk = t // HGRN_CHUNK
    cb = min(8, nchunk)
    nblk = nchunk // cb
    tb = cb * HGRN_CHUNK
    tok = (lambda bi, i: (bi, nblk - 1 - i, 0)) if rev else (lambda bi, i: (bi, i, 0))
    st_map = lambda bi, i: (bi, 0, 0, 0)
    st_shape = (b, HGRN_HEADS, HGRN_DIM, HGRN_DIM)
    tile = pl.BlockSpec((1, tb, wd), tok)
    st_spec = pl.BlockSpec((1,) + st_shape[1:], st_map)
    ins = ([q] if with_out else []) + [kk, lf, v, s0, tri]
    in_specs = [tile] * (len(ins) - 2) + [st_spec, pl.BlockSpec(tri.shape, lambda bi, i: (0, 0))]
    out_specs = ([tile] if with_out else []) + [st_spec]
    out_shape = ([jax.ShapeDtypeStruct((b, t, wd), BF16)] if with_out else []) + [jax.ShapeDtypeStruct(st_shape, F32)]
    res = pl.pallas_call(
        functools.partial(_hgrn_kernel, rev=rev, with_out=with_out, cb=cb),
        grid=(b, nblk),
        in_specs=in_specs, out_specs=out_specs, out_shape=out_shape,
        scratch_shapes=[pltpu.VMEM(st_shape[1:], F32)],
        compiler_params=_params(("arbitrary", "arbitrary"), VMEM_LIMIT),
    )(*ins)
    return res if with_out else (None, res[0])


def _pack_bf16_pairs(x):
    half = x.shape[-1] // 2
    lo = pltpu.bitcast(x[:, :half].astype(BF16).astype(F32), U32) >> 16
    hi = pltpu.bitcast(x[:, half:].astype(BF16).astype(F32), U32) & jnp.uint32(0xFFFF0000)
    return hi | lo


def _unpack_bf16_pairs(w):
    lo = pltpu.bitcast(w << 16, F32)
    hi = pltpu.bitcast(w & jnp.uint32(0xFFFF0000), F32)
    return lo, hi


def _route_tail(h1, g2_ref, sh2_ref, sc2_ref, rwt_ref, rb_ref, su_ref,
                h1_ref, xm_ref, pos_ref, wgt_ref, cnt_ref, *, td):
    h1_ref[0] = h1
    xm2 = _rms(h1, g2_ref[...]) * (1.0 + sc2_ref[0]) + sh2_ref[0]
    xm_ref[...] = xm2.astype(BF16)
    logits = lax.dot_general(rwt_ref[...], xm2, NT, precision=HIGHEST, preferred_element_type=F32) + rb_ref[...]
    ne, tm = logits.shape
    eid = lax.broadcasted_iota(I32, (ne, tm), 0).astype(F32)
    vals, hots = [], []
    for k in range(TOP_K):
        m = jnp.max(logits, axis=0, keepdims=True)
        sel = jnp.min(jnp.where(logits == m, eid, float(ne)), axis=0, keepdims=True)
        hot = eid == sel
        logits = jnp.where(hot, -jnp.inf, logits)
        vals.append(m)
        hots.append(hot)
    es = [jnp.exp(vv - vals[0]) for vv in vals]
    den = es[0] + es[1] + es[2] + es[3]
    onehot = jnp.zeros((ne, tm), F32)
    for k in range(TOP_K):
        wgt_ref[k:k + 1, :] = es[k] / den
        onehot = onehot + hots[k].astype(F32)
    prefix = _dot(onehot.astype(BF16), su_ref[...])
    strict_lower = (lax.broadcasted_iota(I32, (ne, ne), 0) > lax.broadcasted_iota(I32, (ne, ne), 1)).astype(BF16)
    lane = lax.broadcasted_iota(I32, (ne, LANES), 1)
    counts = jnp.zeros((ne, LANES), F32)
    starts = []
    for j in range(tm // td):
        cj = jnp.sum(onehot[:, j * td:(j + 1) * td], axis=1, keepdims=True)
        padded = jnp.floor((cj + (ROW_ALIGN - 1)) * (1.0 / ROW_ALIGN)) * ROW_ALIGN
        group_start = _dot(strict_lower, jnp.broadcast_to(padded, (ne, LANES)).astype(BF16))
        starts.append(jnp.broadcast_to(group_start[:, 0:1], (ne, td)))
        counts = counts + jnp.where(lane == j, cj, 0.0)
    row = prefix + jnp.concatenate(starts, axis=1)
    for k in range(TOP_K):
        pos_ref[k:k + 1, :] = jnp.sum(jnp.where(hots[k], row, 0.0), axis=0, keepdims=True).astype(I32)
    cnt_ref[0] = counts


def _tail_specs(b, s, d, tm):
    nt = s // tm
    n = b * s
    vec = lambda bi, ti: (bi, 0, 0)
    const = lambda bi, ti: (0, 0)
    in_specs = [pl.BlockSpec((1, d), const), pl.BlockSpec((1, 1, d), vec), pl.BlockSpec((1, 1, d), vec),
                pl.BlockSpec((N_EXPERTS, d), const), pl.BlockSpec((N_EXPERTS, 1), const), pl.BlockSpec((tm, tm), const)]
    flat = lambda bi, ti: (0, bi * nt + ti)
    out_specs = [pl.BlockSpec((1, tm, d), lambda bi, ti: (bi, ti, 0)),
                 pl.BlockSpec((tm, d), lambda bi, ti: (bi * nt + ti, 0)),
                 pl.BlockSpec((TOP_K, tm), flat), pl.BlockSpec((TOP_K, tm), flat),
                 pl.BlockSpec((1, N_EXPERTS, LANES), lambda bi, ti: (bi * nt + ti, 0, 0))]
    out_shape = [jax.ShapeDtypeStruct((b, s, d), F32), jax.ShapeDtypeStruct((n, d), BF16),
                 jax.ShapeDtypeStruct((TOP_K, n), I32), jax.ShapeDtypeStruct((TOP_K, n), F32),
                 jax.ShapeDtypeStruct((b * nt, N_EXPERTS, LANES), F32)]
    return in_specs, out_specs, out_shape


def _outproj_kernel(nat_ref, of_ref, ob_ref, sg_ref, x_ref, gate_ref, on_ref, w_ref,
                    g2_ref, sh2_ref, sc2_ref, rwt_ref, rb_ref, su_ref,
                    h1_ref, xm_ref, pos_ref, wgt_ref, cnt_ref, *, td):
    o = of_ref[0].astype(F32) + ob_ref[0].astype(F32)
    gated = []
    for h in range(HGRN_HEADS):
        ls = slice(h * HGRN_DIM, (h + 1) * HGRN_DIM)
        gated.append(_rms(o[:, ls], on_ref[...]) * sg_ref[0, :, ls].astype(F32))
    gated = jnp.concatenate(gated, axis=-1).astype(BF16)
    y = _dot(nat_ref[0], w_ref[:NAT_WIDTH, :]) + _dot(gated, w_ref[NAT_WIDTH:, :])
    h1 = x_ref[0] + gate_ref[0] * y
    _route_tail(h1, g2_ref, sh2_ref, sc2_ref, rwt_ref, rb_ref, su_ref,
                h1_ref, xm_ref, pos_ref, wgt_ref, cnt_ref, td=td)


def _outproj_route(nat, of, ob, sg, x, gate, on_g, w_bf, tail_args, tm, td):
    b, s, d = x.shape
    tok = lambda bi, ti: (bi, ti, 0)
    vec = lambda bi, ti: (bi, 0, 0)
    const = lambda bi, ti: (0, 0)
    t_in, out_specs, out_shape = _tail_specs(b, s, d, tm)
    wide = pl.BlockSpec((1, tm, NAT_WIDTH), tok)
    return pl.pallas_call(
        functools.partial(_outproj_kernel, td=td),
        grid=(b, s // tm),
        in_specs=[wide, wide, wide, wide, pl.BlockSpec((1, tm, d), tok), pl.BlockSpec((1, 1, d), vec),
                  pl.BlockSpec((1, HGRN_DIM), const), pl.BlockSpec(w_bf.shape, const)] + t_in,
        out_specs=out_specs, out_shape=out_shape,
        compiler_params=_params(("arbitrary", "arbitrary"), VMEM_LIMIT),
    )(nat, of, ob, sg, x, gate, on_g, w_bf, *tail_args)


def _conf_in_kernel(x_ref, sh_ref, sc_ref, g_ref, w_ref, b_ref, u_ref):
    d = x_ref.shape[-1]
    xm = (_rms(x_ref[0], g_ref[...]) * (1.0 + sc_ref[0]) + sh_ref[0]).astype(BF16)
    a = _dot(xm, w_ref[:, :d]) + b_ref[:, :d]
    gate = _dot(xm, w_ref[:, d:]) + b_ref[:, d:]
    u_ref[0] = (a * jax.nn.sigmoid(gate)).astype(BF16)


def _conf_in(x, shift, scale, g, w_bf, b1, tm):
    b, s, d = x.shape
    tok = lambda bi, ti: (bi, ti, 0)
    vec = lambda bi, ti: (bi, 0, 0)
    const = lambda bi, ti: (0, 0)
    return pl.pallas_call(
        _conf_in_kernel,
        grid=(b, s // tm),
        in_specs=[pl.BlockSpec((1, tm, d), tok), pl.BlockSpec((1, 1, d), vec), pl.BlockSpec((1, 1, d), vec),
                  pl.BlockSpec((1, d), const), pl.BlockSpec(w_bf.shape, const), pl.BlockSpec((1, 2 * d), const)],
        out_specs=pl.BlockSpec((1, tm, d), tok),
        out_shape=jax.ShapeDtypeStruct((b, s, d), BF16),
        compiler_params=_params(("arbitrary", "arbitrary"), VMEM_LIMIT),
    )(x, shift, scale, g, w_bf, b1)


def _conf_out_kernel(up_ref, uc_ref, un_ref, dw_ref, dwb_ref, lg_ref, lb_ref, w_ref, b2_ref, x_ref, gate_ref,
                     g2_ref, sh2_ref, sc2_ref, rwt_ref, rb_ref, su_ref,
                     h1_ref, xm_ref, pos_ref, wgt_ref, cnt_ref, ubuf, cbuf, sbuf, *, rc, td):
    ti = pl.program_id(1)
    tm, d = uc_ref.shape[1], uc_ref.shape[2]
    hal = CONV_HALO
    prev = up_ref[0].astype(F32)
    nxt = un_ref[0].astype(F32)
    ubuf[0:hal, :] = jnp.where(ti > 0, prev, jnp.zeros_like(prev))
    ubuf[hal:hal + tm, :] = uc_ref[0].astype(F32)
    ubuf[hal + tm:, :] = jnp.where(ti < pl.num_programs(1) - 1, nxt, jnp.zeros_like(nxt))
    base = hal - CONV_WIDTH // 2
    lc = sbuf.shape[2]
    sub = sbuf.shape[0] + 1
    srows = sbuf.shape[1]
    bc = 64

    def shift_chunk(r0, nrows, l0):
        win = ubuf[pl.ds(r0, nrows + sub), l0:l0 + lc]
        for r in range(1, sub):
            sbuf[r - 1, pl.ds(r0, nrows), :] = win[r:r + nrows]

    for l0 in range(0, d, lc):
        def shift_rows(ci, carry, l0=l0):
            shift_chunk(pl.multiple_of(ci * bc, bc), bc, l0)
            return carry

        lax.fori_loop(0, srows // bc, shift_rows, 0)
        if srows % bc:
            shift_chunk(srows // bc * bc, srows % bc, l0)

        def conv_rows(ci, carry, l0=l0):
            r0 = pl.multiple_of(ci * rc, rc)
            acc = jnp.zeros((rc, lc), F32) + dwb_ref[:, l0:l0 + lc]
            for j in range(CONV_WIDTH):
                r = (base + j) % sub
                a = pl.multiple_of(r0 + (base + j - r), sub)
                src = ubuf[pl.ds(a, rc), l0:l0 + lc] if r == 0 else sbuf[r - 1, pl.ds(a, rc), :]
                acc = acc + src * dw_ref[j:j + 1, l0:l0 + lc]
            cbuf[pl.ds(r0, rc), l0:l0 + lc] = acc
            return carry

        lax.fori_loop(0, tm // rc, conv_rows, 0)
    c = cbuf[...]
    mu = jnp.mean(c, axis=-1, keepdims=True)
    cz = c - mu
    var = jnp.mean(cz * cz, axis=-1, keepdims=True)
    y = _silu(cz * lax.rsqrt(var + NORM_EPS) * lg_ref[...] + lb_ref[...]).astype(BF16)
    y = _dot(y, w_ref[...]) + b2_ref[...]
    h1 = x_ref[0] + gate_ref[0] * y
    _route_tail(h1, g2_ref, sh2_ref, sc2_ref, rwt_ref, rb_ref, su_ref,
                h1_ref, xm_ref, pos_ref, wgt_ref, cnt_ref, td=td)


def _conf_out_route(u, dw, dwb, ln_g, ln_b, w_bf, b2, x, gate, tail_args, tm, td):
    b, s, d = x.shape
    hal = CONV_HALO
    per = tm // hal
    nh = s // hal
    tok = lambda bi, ti: (bi, ti, 0)
    vec = lambda bi, ti: (bi, 0, 0)
    const = lambda bi, ti: (0, 0)
    t_in, out_specs, out_shape = _tail_specs(b, s, d, tm)
    return pl.pallas_call(
        functools.partial(_conf_out_kernel, rc=32, td=td),
        grid=(b, s // tm),
        in_specs=[pl.BlockSpec((1, hal, d), lambda bi, ti: (bi, jnp.maximum(ti * per - 1, 0), 0)),
                  pl.BlockSpec((1, tm, d), tok),
                  pl.BlockSpec((1, hal, d), lambda bi, ti: (bi, jnp.minimum((ti + 1) * per, nh - 1), 0)),
                  pl.BlockSpec(dw.shape, const), pl.BlockSpec((1, d), const), pl.BlockSpec((1, d), const),
                  pl.BlockSpec((1, d), const), pl.BlockSpec(w_bf.shape, const), pl.BlockSpec((1, d), const),
                  pl.BlockSpec((1, tm, d), tok), pl.BlockSpec((1, 1, d), vec)] + t_in,
        out_specs=out_specs, out_shape=out_shape,
        scratch_shapes=[pltpu.VMEM((tm + 2 * hal, d), F32), pltpu.VMEM((tm, d), F32),
                        pltpu.VMEM((SUBLANES - 1, tm + (CONV_WIDTH // SUBLANES) * SUBLANES, min(d, 2 * LANES)), F32)],
        compiler_params=_params(("arbitrary", "arbitrary"), VMEM_LIMIT),
    )(u, u, u, dw, dwb, ln_g, ln_b, w_bf, b2, x, gate, *tail_args)


def _copy_rows(src_ref, src0, dst_ref, dst0, length, sem, *, wait, src_fixed=False):
    def piece(off, size):
        s = src0 if src_fixed else pl.multiple_of(src0 + off, ROW_ALIGN)
        return pltpu.make_async_copy(src_ref.at[pl.ds(s, size), :],
                                     dst_ref.at[pl.ds(pl.multiple_of(dst0 + off, ROW_ALIGN), size), :], sem)

    def go(copy):
        if wait:
            copy.wait()
        else:
            copy.start()

    def big(c, carry):
        go(piece(c * RUN_CHUNK, RUN_CHUNK))
        return carry

    lax.fori_loop(0, length // RUN_CHUNK, big, 0)
    size = RUN_CHUNK // 2
    while size >= ROW_ALIGN:
        @pl.when((length & size) != 0)
        def _(size=size):
            go(piece(length & ~(2 * size - 1), size))
        size //= 2


def _moe_dispatch_kernel(gap_ref, meta_ref, x_ref, pos_ref, xs_ref, ybuf, zbuf, sem):
    i = pl.program_id(0)
    cap, td = ybuf.shape[0], x_ref.shape[0]
    pos = pos_ref[...]
    rid = lax.broadcasted_iota(I32, (cap, td), 0)
    hit = rid == pos[0:1]
    for k in range(1, TOP_K):
        hit = jnp.logical_or(hit, rid == pos[k:k + 1])
    ybuf[...] = _pack_bf16_pairs(_dot(jnp.where(hit, 1.0, 0.0).astype(BF16), x_ref[...]))
    for wait in (False, True):
        def run(e, carry, wait=wait):
            _copy_rows(ybuf, meta_ref[0, 0, e], xs_ref, meta_ref[0, 2, e], meta_ref[0, 1, e], sem, wait=wait)
            return carry
        lax.fori_loop(0, N_EXPERTS, run, 0)

    @pl.when(i == pl.num_programs(0) - 1)
    def _():
        zbuf[...] = jnp.zeros_like(zbuf)
        for wait in (False, True):
            def fill(e, carry, wait=wait):
                _copy_rows(zbuf, 0, xs_ref, gap_ref[e], gap_ref[N_EXPERTS + e], sem, wait=wait, src_fixed=True)
                return carry
            lax.fori_loop(0, N_EXPERTS, fill, 0)


def _moe_dispatch(gap, meta, xm, pos, p, td):
    n, d = xm.shape
    cap = TOP_K * td + N_EXPERTS * ROW_ALIGN
    grid_spec = pltpu.PrefetchScalarGridSpec(
        num_scalar_prefetch=1,
        grid=(n // td,),
        in_specs=[pl.BlockSpec((1, 3, N_EXPERTS), lambda i, g: (i, 0, 0), memory_space=pltpu.SMEM),
                  pl.BlockSpec((td, d), lambda i, g: (i, 0)),
                  pl.BlockSpec((TOP_K, td), lambda i, g: (0, i))],
        out_specs=pl.BlockSpec(memory_space=pl.ANY),
        scratch_shapes=[pltpu.VMEM((cap, d // 2), U32), pltpu.VMEM((RUN_CHUNK, d // 2), U32), pltpu.SemaphoreType.DMA],
    )
    return pl.pallas_call(
        _moe_dispatch_kernel,
        grid_spec=grid_spec,
        out_shape=jax.ShapeDtypeStruct((p, d // 2), U32),
        compiler_params=_params(("arbitrary",), VMEM_LIMIT),
    )(gap, meta, xm, pos)


def _moe_expert_kernel(te_ref, nv_ref, xs_ref, w1_ref, b1_ref, w2_ref, b2_ref, ys_ref, w1b, w2b):
    i = pl.program_id(0)
    f = w2_ref.shape[2]
    changed = jnp.logical_or(i == 0, te_ref[i] != te_ref[jnp.maximum(i - 1, 0)])

    @pl.when(jnp.logical_and(changed, i < nv_ref[0]))
    def _():
        w1b[...] = w1_ref[0, 0].astype(BF16)
        w2b[...] = w2_ref[0, 0].astype(BF16)

    @pl.when(i < nv_ref[0])
    def _():
        lo, hi = _unpack_bf16_pairs(xs_ref[...])
        x = jnp.concatenate([lo, hi], axis=-1).astype(BF16)
        y = jnp.zeros((x.shape[0], w2_ref.shape[3]), F32) + b2_ref[0, 0]
        fc = 512 if f % 512 == 0 else f
        for c in range(f // fc):
            glu = _dot(x, w1b[:, c * fc:(c + 1) * fc]) + b1_ref[0, 0, :, c * fc:(c + 1) * fc]
            lin = _dot(x, w1b[:, f + c * fc:f + (c + 1) * fc]) + b1_ref[0, 0, :, f + c * fc:f + (c + 1) * fc]
            glu = jnp.minimum(glu, SWIGLU_LIMIT)
            lin = jnp.clip(lin, -SWIGLU_LIMIT, SWIGLU_LIMIT)
            act = glu * jax.nn.sigmoid(SWIGLU_ALPHA * glu) * (lin + 1.0)
            y = y + _dot(act.astype(BF16), w2b[c * fc:(c + 1) * fc, :])
        ys_ref[...] = _pack_bf16_pairs(y)


def _moe_experts(tile_expert, n_valid, xs, layer, w1, b1, w2, b2, tme):
    p, d2 = xs.shape
    _, ne, d, f2 = w1.shape
    f = f2 // 2
    row = lambda i, te, nv: (jnp.minimum(i, nv[0] - 1), 0)
    wsel = lambda i, te, nv: (layer, te[jnp.minimum(i, nv[0] - 1)], 0, 0)
    grid_spec = pltpu.PrefetchScalarGridSpec(
        num_scalar_prefetch=2,
        grid=(p // tme,),
        in_specs=[pl.BlockSpec((tme, d2), row),
                  pl.BlockSpec((1, 1, d, f2), wsel), pl.BlockSpec((1, 1, 1, f2), wsel),
                  pl.BlockSpec((1, 1, f, d), wsel), pl.BlockSpec((1, 1, 1, d), wsel)],
        out_specs=pl.BlockSpec((tme, d2), row),
        scratch_shapes=[pltpu.VMEM((d, f2), BF16), pltpu.VMEM((f, d), BF16)],
    )
    depth = w1.shape[0]
    return pl.pallas_call(
        _moe_expert_kernel,
        grid_spec=grid_spec,
        out_shape=jax.ShapeDtypeStruct((p, d2), U32),
        compiler_params=_params(("arbitrary",), VMEM_LIMIT),
    )(tile_expert, n_valid, xs, w1, b1.reshape(depth, ne, 1, f2), w2, b2.reshape(depth, ne, 1, d))


def _moe_combine_kernel(meta_ref, ys_ref, pos_ref, wt_ref, h_ref, gate_ref, o_ref, ybuf, sem):
    first = jnp.logical_and(pl.program_id(0) == 0, pl.program_id(1) == 0)

    @pl.when(first)
    def _():
        ybuf[...] = jnp.zeros_like(ybuf)

    for wait in (False, True):
        def run(e, carry, wait=wait):
            _copy_rows(ys_ref, meta_ref[0, 2, e], ybuf, meta_ref[0, 0, e], meta_ref[0, 1, e], sem, wait=wait)
            return carry
        lax.fori_loop(0, N_EXPERTS, run, 0)

    cap, td = ybuf.shape[0], pos_ref.shape[0]
    pos = pos_ref[...]
    wt = wt_ref[...]
    cid = lax.broadcasted_iota(I32, (td, cap), 1)
    mix = jnp.zeros((td, cap), F32)
    for k in range(TOP_K):
        mix = mix + jnp.where(cid == pos[:, k:k + 1], wt[:, k:k + 1], 0.0)
    mix_hi = mix.astype(BF16)
    mix_lo = (mix - mix_hi.astype(F32)).astype(BF16)
    lo, hi = _unpack_bf16_pairs(ybuf[...])
    y = jnp.concatenate([lo, hi], axis=-1).astype(BF16)
    o_ref[0] = h_ref[0] + gate_ref[0] * (_dot(mix_hi, y) + _dot(mix_lo, y))


def _moe_combine(meta, ys, pos_t, wt_t, h, gate, td):
    b, s, d = h.shape
    nt = s // td
    cap = TOP_K * td + N_EXPERTS * ROW_ALIGN
    flat = lambda bi, ti: (bi * nt + ti, 0)
    return pl.pallas_call(
        _moe_combine_kernel,
        grid=(b, nt),
        in_specs=[pl.BlockSpec((1, 3, N_EXPERTS), lambda bi, ti: (bi * nt + ti, 0, 0), memory_space=pltpu.SMEM),
                  pl.BlockSpec(memory_space=pl.ANY),
                  pl.BlockSpec((td, TOP_K), flat), pl.BlockSpec((td, TOP_K), flat),
                  pl.BlockSpec((1, td, d), lambda bi, ti: (bi, ti, 0)),
                  pl.BlockSpec((1, 1, d), lambda bi, ti: (bi, 0, 0))],
        out_specs=pl.BlockSpec((1, td, d), lambda bi, ti: (bi, ti, 0)),
        out_shape=jax.ShapeDtypeStruct((b, s, d), F32),
        scratch_shapes=[pltpu.VMEM((cap, d // 2), U32), pltpu.SemaphoreType.DMA],
        compiler_params=_params(("arbitrary", "arbitrary"), VMEM_LIMIT),
    )(meta, ys, pos_t, wt_t, h, gate)


def _moe(h1, xm, pos, wgt, counts, gate, layer, w1, b1, w2, b2, tme, td):
    n, d = xm.shape
    per = (n // td) // counts.shape[0]
    cnt = jnp.round(counts[:, :, :per]).astype(I32).transpose(0, 2, 1).reshape(n // td, N_EXPERTS)
    run = (cnt + ROW_ALIGN - 1) // ROW_ALIGN * ROW_ALIGN
    local = jnp.cumsum(run, axis=1) - run
    tot = jnp.sum(run, axis=0)
    cap = (tot + tme - 1) // tme * tme
    ends = jnp.cumsum(cap)
    start = ends - cap
    glob = start[None, :] + jnp.cumsum(run, axis=0) - run
    meta = jnp.stack([local, run, glob], axis=1)
    gap = jnp.concatenate([start + tot, cap - tot]).astype(I32)
    p = (TOP_K * n + N_EXPERTS * ROW_ALIGN * (n // td) + N_EXPERTS * tme) // tme * tme
    tiles = jnp.arange(p // tme, dtype=I32)
    tile_expert = jnp.minimum(jnp.sum(tiles[:, None] >= (ends // tme)[None, :], axis=1), N_EXPERTS - 1).astype(I32)
    n_valid = (ends[-1:] // tme).astype(I32)
    xs = _moe_dispatch(gap, meta, xm, pos, p, td)
    ys = _moe_experts(tile_expert, n_valid, xs, layer, w1, b1, w2, b2, tme)
    return _moe_combine(meta, ys, pos.T, wgt.T, h1, gate, td)


def _rope_tables(s):
    pos = np.arange(s)
    lane = np.arange(LANES)
    dd = lane % NAT_HEAD_DIM
    n = NAT_HEAD_DIM // 4
    inv_freq = ROPE_BASE ** (-(dd % n).astype(np.float64) / n)
    p = np.where((dd // (NAT_HEAD_DIM // 2))[None, :] == 0, (pos // GRID_W)[:, None], (pos % GRID_W)[:, None])
    ang = (p.astype(np.float32) * inv_freq.astype(np.float32)[None, :]).astype(np.float32)
    sign = np.where((dd % (2 * n)) < n, -1.0, 1.0)[None, :]
    return jnp.asarray(np.cos(ang), F32), jnp.asarray(np.sin(ang) * sign, F32)


def _tile(n, want):
    t = min(want, n)
    while n % t:
        t //= 2
    return t


def kernel(x, c, ctx, c_ctx, ada_w, ada_b, norm1_g, norm2_g, ab_w_in, ab_w_out, nat_q_norm, nat_k_norm, nat_rpb, hgrn_lb, hgrn_o_norm, conv_w1, conv_b1, conv_dw, conv_dw_b, conv_ln_g, conv_ln_b, conv_w2, conv_b2, router_w, router_b, moe_w1, moe_b1, moe_w2, moe_b2):
    b, s, d = x.shape
    l = ctx.shape[1]
    rows = s // GRID_W
    assert ada_w.shape[0] == 2 and b < MOD_ROWS and rows >= NAT_KH and rows % NAT_KH == 0
    assert s % HGRN_CHUNK == 0 and l % HGRN_CHUNK == 0 and d % (2 * LANES) == 0
    tm = _tile(s, 512)
    td = _tile(s, 256)
    tme = 512 if TOP_K * b * s >= 512 * N_EXPERTS else 128

    cc = jnp.zeros((MOD_ROWS, d), F32).at[:b].set(c).at[b].set(c_ctx)
    mod = _modulation(cc, ada_w, ada_b)

    def mod_vec(layer, i):
        return mod[layer, :b, i * d:(i + 1) * d].reshape(b, 1, d)

    def ctx_vec(i):
        return jnp.broadcast_to(mod[0, b, i * d:(i + 1) * d].reshape(1, 1, d), (b, 1, d))

    tok = np.arange(tm)
    strict_upper = jnp.asarray((tok[:, None] < tok[None, :]) & (tok[:, None] // td == tok[None, :] // td), BF16)

    def tail_args(layer):
        return (norm2_g[layer].reshape(1, d), mod_vec(layer, 3), mod_vec(layer, 4),
                router_w[layer].T, router_b[layer].reshape(N_EXPERTS, 1), strict_upper)

    lb_all = jnp.cumsum(jax.nn.softmax(hgrn_lb.astype(F32), axis=1), axis=1)[:, 0]
    w_in = ab_w_in[0].astype(BF16)
    scale = NAT_HEAD_DIM ** -0.5
    qg = jnp.tile(nat_q_norm[0] * scale, NAT_HEADS).reshape(1, NAT_WIDTH)
    kg = jnp.tile(nat_k_norm[0], NAT_HEADS).reshape(1, NAT_WIDTH)
    head_of = np.arange(NAT_WIDTH) // NAT_HEAD_DIM
    bd = jnp.asarray(head_of[:, None] == head_of[None, :], BF16)
    lbf, lbb = lb_all[0].reshape(1, HGRN_WIDTH), lb_all[1].reshape(1, HGRN_WIDTH)
    cos, sin = _rope_tables(s)
    g1 = norm1_g[0].reshape(1, d)
    lat = _inproj(x, mod_vec(0, 0), mod_vec(0, 1), g1, w_in, cos, sin, qg, kg, bd, lbf, lbb, tm)
    qa, qb, k, v, hq, kf, lff, kb, lfb, hi, sg = lat
    tl = _tile(l, 256)
    cxt = _inproj(ctx, ctx_vec(0), ctx_vec(1), g1, w_in, jnp.ones((l, LANES), F32), jnp.zeros((l, LANES), F32),
                  qg, kg, bd, lbf, lbb, tl)
    _, _, kx, vx, _, kfx, lffx, kbx, lfbx, hix, _ = cxt

    nat = _nat_attention(qa, qb, k, v, kx, vx, _nat_bias_table(nat_rpb[0]), rows)

    tri_f = jnp.asarray(np.tril(np.ones((HGRN_CHUNK, HGRN_CHUNK), np.float32)))
    s0 = jnp.zeros((b, HGRN_HEADS, HGRN_DIM, HGRN_DIM), F32)
    _, sf = _hgrn_scan(None, kfx, lffx, hix, s0, tri_f, False, False)
    of, _ = _hgrn_scan(hq, kf, lff, hi, sf, tri_f, False, True)
    _, sb = _hgrn_scan(None, kbx, lfbx, hix, s0, tri_f.T, True, False)
    ob, _ = _hgrn_scan(hq, kb, lfb, hi, sb, tri_f.T, True, True)

    h1, xm, pos, wgt, counts = _outproj_route(
        nat, of, ob, sg, x, mod_vec(0, 2), hgrn_o_norm[0].reshape(1, HGRN_DIM), ab_w_out[0].astype(BF16),
        tail_args(0), tm, td)
    h = _moe(h1, xm, pos, wgt, counts, mod_vec(0, 5), 0, moe_w1, moe_b1, moe_w2, moe_b2, tme, td)

    u = _conf_in(h, mod_vec(1, 0), mod_vec(1, 1), norm1_g[1].reshape(1, d), conv_w1[0].astype(BF16),
                 conv_b1[0].reshape(1, 2 * d), tm)
    h1, xm, pos, wgt, counts = _conf_out_route(
        u, conv_dw[0], conv_dw_b[0].reshape(1, d), conv_ln_g[0].reshape(1, d), conv_ln_b[0].reshape(1, d),
        conv_w2[0].astype(BF16), conv_b2[0].reshape(1, d), h, mod_vec(1, 2), tail_args(1), tm, td)
    return _moe(h1, xm, pos, wgt, counts, mod_vec(1, 5), 1, moe_w1, moe_b1, moe_w2, moe_b2, tme, td)
```

```python
import functools

import numpy as np
import jax
import jax.numpy as jnp
from jax import lax
from jax.experimental import pallas as pl
from jax.experimental.pallas import tpu as pltpu

F32 = jnp.float32
BF16 = jnp.bfloat16
U32 = jnp.uint32
I32 = jnp.int32
HIGHEST = lax.Precision.HIGHEST

GRID_W = 64
NAT_HEADS = 8
NAT_HEAD_DIM = 64
NAT_WIDTH = NAT_HEADS * NAT_HEAD_DIM
NAT_KH = 8
NAT_KW = 16
HGRN_HEADS = 4
HGRN_DIM = 128
HGRN_WIDTH = HGRN_HEADS * HGRN_DIM
HGRN_CHUNK = 64
HGRN_SUB = 16
CONV_WIDTH = 31
CONV_HALO = 16
N_EXPERTS = 32
TOP_K = 4
SWIGLU_LIMIT = 7.0
SWIGLU_ALPHA = 1.702
ROPE_BASE = 10000.0
NORM_EPS = 1e-6
MASK_VALUE = -1e30
EXP_CLAMP = 80.0

SUBLANES = 8
ROW_ALIGN = SUBLANES
RUN_CHUNK = 64
LANES = 128
MOD_ROWS = 16
VMEM_LIMIT = 56 * 1024 * 1024
NT = (((1,), (1,)), ((), ()))


def _params(sem, vmem=None):
    return pltpu.CompilerParams(dimension_semantics=sem, vmem_limit_bytes=vmem)


def _dot(a, b):
    return jnp.dot(a, b, preferred_element_type=F32)


def _dot_nt(a, b):
    return lax.dot_general(a, b, NT, preferred_element_type=F32)


def _silu(x):
    return x * jax.nn.sigmoid(x)


def _rms(x, g):
    return x * lax.rsqrt(jnp.mean(x * x, axis=-1, keepdims=True) + NORM_EPS) * g


def _mod_kernel(cc_ref, w_ref, b_ref, o_ref):
    cc = cc_ref[...]
    o_ref[0] = jnp.dot(_silu(cc), w_ref[0], precision=HIGHEST, preferred_element_type=F32) + b_ref[0]


def _modulation(cc, ada_w, ada_b):
    depth, d, n = ada_w.shape
    tn = n // 4
    return pl.pallas_call(
        _mod_kernel,
        grid=(depth, n // tn),
        in_specs=[pl.BlockSpec((MOD_ROWS, d), lambda l, j: (0, 0)),
                  pl.BlockSpec((1, d, tn), lambda l, j: (l, 0, j)),
                  pl.BlockSpec((1, 1, tn), lambda l, j: (l, 0, j))],
        out_specs=pl.BlockSpec((1, MOD_ROWS, tn), lambda l, j: (l, 0, j)),
        out_shape=jax.ShapeDtypeStruct((depth, MOD_ROWS, n), F32),
        compiler_params=_params(("arbitrary", "arbitrary")),
    )(cc, ada_w, ada_b.reshape(depth, 1, n))


def _inproj_kernel(x_ref, sh_ref, sc_ref, g_ref, w_ref, cos_ref, sin_ref, qg_ref, kg_ref, bd_ref, lbf_ref, lbb_ref,
                   qa_ref, qb_ref, k_ref, v_ref, hq_ref, kf_ref, lff_ref, kb_ref, lfb_ref, hi_ref, sg_ref):
    x = x_ref[0]
    xm = (_rms(x, g_ref[...]) * (1.0 + sc_ref[0]) + sh_ref[0]).astype(BF16)
    wd = NAT_WIDTH

    def proj(i):
        return _dot(xm, w_ref[:, i * wd:(i + 1) * wd])

    def head_norm(y, g):
        sq = y * y
        hi = sq.astype(BF16)
        lo = (sq - hi.astype(F32)).astype(BF16)
        ss = _dot(hi, bd_ref[...]) + _dot(lo, bd_ref[...])
        return y * lax.rsqrt(ss * (1.0 / NAT_HEAD_DIM) + NORM_EPS) * g

    cos = cos_ref[...]
    sin = sin_ref[...]
    lane = lax.broadcasted_iota(I32, cos.shape, 1)
    first = (lane % 32) < 16

    def rope(y):
        outs = []
        for gi in range(wd // LANES):
            yg = y[:, gi * LANES:(gi + 1) * LANES]
            partner = jnp.where(first, pltpu.roll(yg, LANES - 16, 1), pltpu.roll(yg, 16, 1))
            outs.append(yg * cos + partner * sin)
        return jnp.concatenate(outs, axis=-1)

    qf = head_norm(proj(0), qg_ref[...])
    qa_ref[0] = rope(qf).astype(BF16)
    qb_ref[0] = qf.astype(BF16)
    k_ref[0] = rope(head_norm(proj(1), kg_ref[...])).astype(BF16)
    v_ref[0] = proj(2).astype(BF16)
    hq_ref[0] = (_silu(proj(3)) * (HGRN_DIM ** -0.5)).astype(BF16)
    for i, lb_ref, kk_ref, lf_ref in ((4, lbf_ref, kf_ref, lff_ref), (5, lbb_ref, kb_ref, lfb_ref)):
        lb = lb_ref[...]
        f = lb + (1.0 - lb) * jax.nn.sigmoid(proj(i))
        kk_ref[0] = (1.0 - f).astype(BF16)
        lf_ref[0] = jnp.log(f)
    hi_ref[0] = proj(6).astype(BF16)
    sg_ref[0] = _silu(proj(7)).astype(BF16)


def _inproj(x, shift, scale, g, w_bf, cos, sin, qg, kg, bd, lbf, lbb, tm):
    b, t, d = x.shape
    wd = NAT_WIDTH
    tok = lambda bi, ti: (bi, ti, 0)
    vec = lambda bi, ti: (bi, 0, 0)
    const = lambda bi, ti: (0, 0)
    out_dtypes = (BF16, BF16, BF16, BF16, BF16, BF16, F32, BF16, F32, BF16, BF16)
    return pl.pallas_call(
        _inproj_kernel,
        grid=(b, t // tm),
        in_specs=[pl.BlockSpec((1, tm, d), tok), pl.BlockSpec((1, 1, d), vec), pl.BlockSpec((1, 1, d), vec),
                  pl.BlockSpec((1, d), const), pl.BlockSpec(w_bf.shape, const),
                  pl.BlockSpec((tm, LANES), lambda bi, ti: (ti, 0)), pl.BlockSpec((tm, LANES), lambda bi, ti: (ti, 0)),
                  pl.BlockSpec((1, wd), const), pl.BlockSpec((1, wd), const), pl.BlockSpec((wd, wd), const),
                  pl.BlockSpec((1, wd), const), pl.BlockSpec((1, wd), const)],
        out_specs=[pl.BlockSpec((1, tm, wd), tok)] * len(out_dtypes),
        out_shape=[jax.ShapeDtypeStruct((b, t, wd), dt) for dt in out_dtypes],
        compiler_params=_params(("arbitrary", "arbitrary"), VMEM_LIMIT),
    )(x, shift, scale, g, w_bf, cos, sin, qg, kg, bd, lbf, lbb)


def _nat_kernel(qa_ref, qb_ref, kp_ref, kc_ref, kn_ref, vp_ref, vc_ref, vn_ref, kx_ref, vx_ref, bias_ref,
                o_ref, kbuf, vbuf, s1_scr, s2_scr, p1_scr, p2_scr, *, rows, rb):
    i = pl.program_id(1)
    blk = rb * GRID_W
    for s, (kr, vr) in enumerate(((kp_ref, vp_ref), (kc_ref, vc_ref), (kn_ref, vn_ref))):
        kbuf[s * blk:(s + 1) * blk, :] = kr[0]
        vbuf[s * blk:(s + 1) * blk, :] = vr[0]
    lane = lax.broadcasted_iota(I32, (GRID_W, LANES), 1)
    low = lane < NAT_HEAD_DIM
    kwin = NAT_KH * GRID_W
    pair = 2 * GRID_W
    sm_rows = 16

    def stack_heads(q2):
        q2 = q2.astype(F32)
        return jnp.concatenate([jnp.where(low, q2, 0.0), jnp.where(low, 0.0, q2)], axis=0).astype(BF16)

    def row_body(j, carry):
        r = i * rb + j
        rs = jnp.clip(r - NAT_KH // 2, 0, rows - NAT_KH)
        start = pl.multiple_of((rs - i * rb + rb) * GRID_W, GRID_W)
        cls = rs - r + NAT_KH - 1
        q0 = pl.multiple_of(j * GRID_W, GRID_W)
        for p in range(NAT_HEADS // 2):
            ls = slice(p * LANES, (p + 1) * LANES)
            bias = jnp.concatenate([bias_ref[cls, 2 * p], bias_ref[cls, 2 * p + 1]], axis=0)
            s1_scr[p * pair:(p + 1) * pair, :] = _dot_nt(stack_heads(qa_ref[0, pl.ds(q0, GRID_W), ls]),
                                                         kbuf[pl.ds(start, kwin), ls]) + bias
            s2_scr[p * pair:(p + 1) * pair, :] = _dot_nt(stack_heads(qb_ref[0, pl.ds(q0, GRID_W), ls]), kx_ref[0, :, ls])
        for c in range(NAT_HEADS * GRID_W // sm_rows):
            rsl = slice(c * sm_rows, (c + 1) * sm_rows)
            a = s1_scr[rsl, :]
            bb = s2_scr[rsl, :]
            m = jnp.maximum(jnp.max(a, axis=-1, keepdims=True), jnp.max(bb, axis=-1, keepdims=True))
            ea = jnp.exp(a - m)
            eb = jnp.exp(bb - m)
            inv = 1.0 / (jnp.sum(ea, axis=-1, keepdims=True) + jnp.sum(eb, axis=-1, keepdims=True))
            p1_scr[rsl, :] = (ea * inv).astype(BF16)
            p2_scr[rsl, :] = (eb * inv).astype(BF16)
        for p in range(NAT_HEADS // 2):
            ls = slice(p * LANES, (p + 1) * LANES)
            o = (_dot(p1_scr[p * pair:(p + 1) * pair, :], vbuf[pl.ds(start, kwin), ls])
                 + _dot(p2_scr[p * pair:(p + 1) * pair, :], vx_ref[0, :, ls]))
            o_ref[0, pl.ds(q0, GRID_W), ls] = jnp.where(low, o[:GRID_W], o[GRID_W:]).astype(BF16)
        return carry

    lax.fori_loop(0, rb, row_body, 0)


def _nat_attention(qa, qb, k, v, kx, vx, bias, rows):
    b, s, wd = qa.shape
    lx = kx.shape[1]
    rb = NAT_KH
    blk = rb * GRID_W
    nb = rows // rb
    cur = lambda bi, i: (bi, i, 0)
    prv = lambda bi, i: (bi, jnp.maximum(i - 1, 0), 0)
    nxt = lambda bi, i: (bi, jnp.minimum(i + 1, nb - 1), 0)
    ctx = lambda bi, i: (bi, 0, 0)
    tile = lambda im: pl.BlockSpec((1, blk, wd), im)
    return pl.pallas_call(
        functools.partial(_nat_kernel, rows=rows, rb=rb),
        grid=(b, nb),
        in_specs=[tile(cur), tile(cur), tile(prv), tile(cur), tile(nxt), tile(prv), tile(cur), tile(nxt),
                  pl.BlockSpec((1,) + kx.shape[1:], ctx), pl.BlockSpec((1,) + vx.shape[1:], ctx),
                  pl.BlockSpec(bias.shape, lambda bi, i: (0, 0, 0, 0))],
        out_specs=tile(cur),
        out_shape=jax.ShapeDtypeStruct((b, s, wd), BF16),
        scratch_shapes=[pltpu.VMEM((3 * blk, wd), BF16), pltpu.VMEM((3 * blk, wd), BF16),
                        pltpu.VMEM((NAT_HEADS * GRID_W, NAT_KH * GRID_W), F32), pltpu.VMEM((NAT_HEADS * GRID_W, lx), F32),
                        pltpu.VMEM((NAT_HEADS * GRID_W, NAT_KH * GRID_W), BF16), pltpu.VMEM((NAT_HEADS * GRID_W, lx), BF16)],
        compiler_params=_params(("arbitrary", "arbitrary"), VMEM_LIMIT),
    )(qa, qb, k, k, k, v, v, v, kx, vx, bias)


def _nat_bias_table(rpb):
    qc = np.arange(GRID_W)
    wc = np.clip(qc - NAT_KW // 2, 0, GRID_W - NAT_KW)
    kc = np.arange(GRID_W)
    valid = (kc[None, :] >= wc[:, None]) & (kc[None, :] < wc[:, None] + NAT_KW)
    cidx = np.clip(kc[None, :] - qc[:, None] + NAT_KW - 1, 0, 2 * NAT_KW - 2)
    ridx = np.arange(NAT_KH)[:, None] + np.arange(NAT_KH)[None, :]
    tab = rpb.astype(F32)[:, ridx][:, :, :, cidx]
    tab = jnp.where(valid[None, None, None], tab, MASK_VALUE)
    tab = tab.transpose(1, 0, 3, 2, 4)
    return tab.reshape(NAT_KH, NAT_HEADS, GRID_W, NAT_KH * GRID_W)


def _hgrn_kernel(*refs, rev, with_out, cb):
    if with_out:
        q_ref, kk_ref, lf_ref, v_ref, s0_ref, tri_ref, o_ref, sf_ref, st = refs
    else:
        kk_ref, lf_ref, v_ref, s0_ref, tri_ref, sf_ref, st = refs
    i = pl.program_id(1)
    c_sz = HGRN_CHUNK

    @pl.when(i == 0)
    def _():
        st[...] = s0_ref[0]

    tri = tri_ref[...]
    keep = tri > 0.5

    def chunk(ci, carry):
        c = (cb - 1 - ci) if rev else ci
        r0 = pl.multiple_of(c * c_sz, c_sz)
        bsum = jnp.dot(tri, lf_ref[0, pl.ds(r0, c_sz), :], precision=HIGHEST, preferred_element_type=F32)
        for h in range(HGRN_HEADS):
            ls = slice(h * HGRN_DIM, (h + 1) * HGRN_DIM)
            bh = bsum[:, ls]
            tot = bh[0:1] if rev else bh[c_sz - 1:c_sz]
            kh = kk_ref[0, pl.ds(r0, c_sz), ls].astype(F32)
            vh = v_ref[0, pl.ds(r0, c_sz), ls]
            state = st[h]
            if with_out:
                qh = q_ref[0, pl.ds(r0, c_sz), ls].astype(F32)
                blocks = []
                for sb in range(c_sz // HGRN_SUB):
                    lo = sb * HGRN_SUB
                    ref_row = lo + HGRN_SUB - 1 if rev else lo
                    cref = bh[ref_row:ref_row + 1]
                    f1 = jnp.exp(bh[lo:lo + HGRN_SUB] - cref)
                    f2 = jnp.exp(jnp.minimum(cref - bh, EXP_CLAMP))
                    blocks.append(_dot_nt((qh[lo:lo + HGRN_SUB] * f1).astype(BF16), (kh * f2).astype(BF16)))
                scores = jnp.where(keep, jnp.concatenate(blocks, axis=0), 0.0)
                o = _dot(scores.astype(BF16), vh) + _dot_nt((qh * jnp.exp(bh)).astype(BF16), state.astype(BF16))
                o_ref[0, pl.ds(r0, c_sz), ls] = o.astype(BF16)
            kd = (kh * jnp.exp(tot - bh)).astype(BF16)
            vt = vh.astype(F32).T.astype(BF16)
            st[h] = state * jnp.exp(tot) + _dot(vt, kd)
        return carry

    lax.fori_loop(0, cb, chunk, 0, unroll=2)

    @pl.when(i == pl.num_programs(1) - 1)
    def _():
        sf_ref[0] = st[...]


def _hgrn_scan(q, kk, lf, v, s0, tri, rev, with_out):
    b, t, wd = kk.shape
    nchunk = t // HGRN_CHUNK
    cb = min(8, nchunk)
    nblk = nchunk // cb
    tb = cb * HGRN_CHUNK
    tok = (lambda bi, i: (bi, nblk - 1 - i, 0)) if rev else (lambda bi, i: (bi, i, 0))
    st_map = lambda bi, i: (bi, 0, 0, 0)
    st_shape = (b, HGRN_HEADS, HGRN_DIM, HGRN_DIM)
    tile = pl.BlockSpec((1, tb, wd), tok)
    st_spec = pl.BlockSpec((1,) + st_shape[1:], st_map)
    ins = ([q] if with_out else []) + [kk, lf, v, s0, tri]
    in_specs = [tile] * (len(ins) - 2) + [st_spec, pl.BlockSpec(tri.shape, lambda bi, i: (0, 0))]
    out_specs = ([tile] if with_out else []) + [st_spec]
    out_shape = ([jax.ShapeDtypeStruct((b, t, wd), BF16)] if with_out else []) + [jax.ShapeDtypeStruct(st_shape, F32)]
    res = pl.pallas_call(
        functools.partial(_hgrn_kernel, rev=rev, with_out=with_out, cb=cb),
        grid=(b, nblk),
        in_specs=in_specs, out_specs=out_specs, out_shape=out_shape,
        scratch_shapes=[pltpu.VMEM(st_shape[1:], F32)],
        compiler_params=_params(("arbitrary", "arbitrary"), VMEM_LIMIT),
    )(*ins)
    return res if with_out else (None, res[0])


def _pack_bf16_pairs(x):
    half = x.shape[-1] // 2
    lo = pltpu.bitcast(x[:, :half].astype(BF16).astype(F32), U32) >> 16
    hi = pltpu.bitcast(x[:, half:].astype(BF16).astype(F32), U32) & jnp.uint32(0xFFFF0000)
    return hi | lo


def _unpack_bf16_pairs(w):
    lo = pltpu.bitcast(w << 16, F32)
    hi = pltpu.bitcast(w & jnp.uint32(0xFFFF0000), F32)
    return lo, hi


def _route_tail(h1, g2_ref, sh2_ref, sc2_ref, rwt_ref, rb_ref, su_ref,
                h1_ref, xm_ref, pos_ref, wgt_ref, cnt_ref, *, td):
    h1_ref[0] = h1
    xm2 = _rms(h1, g2_ref[...]) * (1.0 + sc2_ref[0]) + sh2_ref[0]
    xm_ref[...] = xm2.astype(BF16)
    logits = lax.dot_general(rwt_ref[...], xm2, NT, precision=HIGHEST, preferred_element_type=F32) + rb_ref[...]
    ne, tm = logits.shape
    eid = lax.broadcasted_iota(I32, (ne, tm), 0).astype(F32)
    vals, hots = [], []
    for k in range(TOP_K):
        m = jnp.max(logits, axis=0, keepdims=True)
        sel = jnp.min(jnp.where(logits == m, eid, float(ne)), axis=0, keepdims=True)
        hot = eid == sel
        logits = jnp.where(hot, -jnp.inf, logits)
        vals.append(m)
        hots.append(hot)
    es = [jnp.exp(vv - vals[0]) for vv in vals]
    den = es[0] + es[1] + es[2] + es[3]
    onehot = jnp.zeros((ne, tm), F32)
    for k in range(TOP_K):
        wgt_ref[k:k + 1, :] = es[k] / den
        onehot = onehot + hots[k].astype(F32)
    prefix = _dot(onehot.astype(BF16), su_ref[...])
    strict_lower = (lax.broadcasted_iota(I32, (ne, ne), 0) > lax.broadcasted_iota(I32, (ne, ne), 1)).astype(BF16)
    lane = lax.broadcasted_iota(I32, (ne, LANES), 1)
    counts = jnp.zeros((ne, LANES), F32)
    starts = []
    for j in range(tm // td):
        cj = jnp.sum(onehot[:, j * td:(j + 1) * td], axis=1, keepdims=True)
        padded = jnp.floor((cj + (ROW_ALIGN - 1)) * (1.0 / ROW_ALIGN)) * ROW_ALIGN
        group_start = _dot(strict_lower, jnp.broadcast_to(padded, (ne, LANES)).astype(BF16))
        starts.append(jnp.broadcast_to(group_start[:, 0:1], (ne, td)))
        counts = counts + jnp.where(lane == j, cj, 0.0)
    row = prefix + jnp.concatenate(starts, axis=1)
    for k in range(TOP_K):
        pos_ref[k:k + 1, :] = jnp.sum(jnp.where(hots[k], row, 0.0), axis=0, keepdims=True).astype(I32)
    cnt_ref[0] = counts


def _tail_specs(b, s, d, tm):
    nt = s // tm
    n = b * s
    vec = lambda bi, ti: (bi, 0, 0)
    const = lambda bi, ti: (0, 0)
    in_specs = [pl.BlockSpec((1, d), const), pl.BlockSpec((1, 1, d), vec), pl.BlockSpec((1, 1, d), vec),
                pl.BlockSpec((N_EXPERTS, d), const), pl.BlockSpec((N_EXPERTS, 1), const), pl.BlockSpec((tm, tm), const)]
    flat = lambda bi, ti: (0, bi * nt + ti)
    out_specs = [pl.BlockSpec((1, tm, d), lambda bi, ti: (bi, ti, 0)),
                 pl.BlockSpec((tm, d), lambda bi, ti: (bi * nt + ti, 0)),
                 pl.BlockSpec((TOP_K, tm), flat), pl.BlockSpec((TOP_K, tm), flat),
                 pl.BlockSpec((1, N_EXPERTS, LANES), lambda bi, ti: (bi * nt + ti, 0, 0))]
    out_shape = [jax.ShapeDtypeStruct((b, s, d), F32), jax.ShapeDtypeStruct((n, d), BF16),
                 jax.ShapeDtypeStruct((TOP_K, n), I32), jax.ShapeDtypeStruct((TOP_K, n), F32),
                 jax.ShapeDtypeStruct((b * nt, N_EXPERTS, LANES), F32)]
    return in_specs, out_specs, out_shape


def _outproj_kernel(nat_ref, of_ref, ob_ref, sg_ref, x_ref, gate_ref, on_ref, w_ref,
                    g2_ref, sh2_ref, sc2_ref, rwt_ref, rb_ref, su_ref,
                    h1_ref, xm_ref, pos_ref, wgt_ref, cnt_ref, *, td):
    o = of_ref[0].astype(F32) + ob_ref[0].astype(F32)
    gated = []
    for h in range(HGRN_HEADS):
        ls = slice(h * HGRN_DIM, (h + 1) * HGRN_DIM)
        gated.append(_rms(o[:, ls], on_ref[...]) * sg_ref[0, :, ls].astype(F32))
    gated = jnp.concatenate(gated, axis=-1).astype(BF16)
    y = _dot(nat_ref[0], w_ref[:NAT_WIDTH, :]) + _dot(gated, w_ref[NAT_WIDTH:, :])
    h1 = x_ref[0] + gate_ref[0] * y
    _route_tail(h1, g2_ref, sh2_ref, sc2_ref, rwt_ref, rb_ref, su_ref,
                h1_ref, xm_ref, pos_ref, wgt_ref, cnt_ref, td=td)


def _outproj_route(nat, of, ob, sg, x, gate, on_g, w_bf, tail_args, tm, td):
    b, s, d = x.shape
    tok = lambda bi, ti: (bi, ti, 0)
    vec = lambda bi, ti: (bi, 0, 0)
    const = lambda bi, ti: (0, 0)
    t_in, out_specs, out_shape = _tail_specs(b, s, d, tm)
    wide = pl.BlockSpec((1, tm, NAT_WIDTH), tok)
    return pl.pallas_call(
        functools.partial(_outproj_kernel, td=td),
        grid=(b, s // tm),
        in_specs=[wide, wide, wide, wide, pl.BlockSpec((1, tm, d), tok), pl.BlockSpec((1, 1, d), vec),
                  pl.BlockSpec((1, HGRN_DIM), const), pl.BlockSpec(w_bf.shape, const)] + t_in,
        out_specs=out_specs, out_shape=out_shape,
        compiler_params=_params(("arbitrary", "arbitrary"), VMEM_LIMIT),
    )(nat, of, ob, sg, x, gate, on_g, w_bf, *tail_args)


def _conf_in_kernel(x_ref, sh_ref, sc_ref, g_ref, w_ref, b_ref, u_ref):
    d = x_ref.shape[-1]
    xm = (_rms(x_ref[0], g_ref[...]) * (1.0 + sc_ref[0]) + sh_ref[0]).astype(BF16)
    a = _dot(xm, w_ref[:, :d]) + b_ref[:, :d]
    gate = _dot(xm, w_ref[:, d:]) + b_ref[:, d:]
    u_ref[0] = (a * jax.nn.sigmoid(gate)).astype(BF16)


def _conf_in(x, shift, scale, g, w_bf, b1, tm):
    b, s, d = x.shape
    tok = lambda bi, ti: (bi, ti, 0)
    vec = lambda bi, ti: (bi, 0, 0)
    const = lambda bi, ti: (0, 0)
    return pl.pallas_call(
        _conf_in_kernel,
        grid=(b, s // tm),
        in_specs=[pl.BlockSpec((1, tm, d), tok), pl.BlockSpec((1, 1, d), vec), pl.BlockSpec((1, 1, d), vec),
                  pl.BlockSpec((1, d), const), pl.BlockSpec(w_bf.shape, const), pl.BlockSpec((1, 2 * d), const)],
        out_specs=pl.BlockSpec((1, tm, d), tok),
        out_shape=jax.ShapeDtypeStruct((b, s, d), BF16),
        compiler_params=_params(("arbitrary", "arbitrary"), VMEM_LIMIT),
    )(x, shift, scale, g, w_bf, b1)


def _conf_out_kernel(up_ref, uc_ref, un_ref, dw_ref, dwb_ref, lg_ref, lb_ref, w_ref, b2_ref, x_ref, gate_ref,
                     g2_ref, sh2_ref, sc2_ref, rwt_ref, rb_ref, su_ref,
                     h1_ref, xm_ref, pos_ref, wgt_ref, cnt_ref, ubuf, cbuf, sbuf, *, rc, td):
    ti = pl.program_id(1)
    tm, d = uc_ref.shape[1], uc_ref.shape[2]
    hal = CONV_HALO
    prev = up_ref[0].astype(F32)
    nxt = un_ref[0].astype(F32)
    ubuf[0:hal, :] = jnp.where(ti > 0, prev, jnp.zeros_like(prev))
    ubuf[hal:hal + tm, :] = uc_ref[0].astype(F32)
    ubuf[hal + tm:, :] = jnp.where(ti < pl.num_programs(1) - 1, nxt, jnp.zeros_like(nxt))
    base = hal - CONV_WIDTH // 2
    lc = sbuf.shape[2]
    sub = sbuf.shape[0] + 1
    srows = sbuf.shape[1]
    bc = 64

    def shift_chunk(r0, nrows, l0):
        win = ubuf[pl.ds(r0, nrows + sub), l0:l0 + lc]
        for r in range(1, sub):
            sbuf[r - 1, pl.ds(r0, nrows), :] = win[r:r + nrows]

    for l0 in range(0, d, lc):
        def shift_rows(ci, carry, l0=l0):
            shift_chunk(pl.multiple_of(ci * bc, bc), bc, l0)
            return carry

        lax.fori_loop(0, srows // bc, shift_rows, 0)
        if srows % bc:
            shift_chunk(srows // bc * bc, srows % bc, l0)

        def conv_rows(ci, carry, l0=l0):
            r0 = pl.multiple_of(ci * rc, rc)
            acc = jnp.zeros((rc, lc), F32) + dwb_ref[:, l0:l0 + lc]
            for j in range(CONV_WIDTH):
                r = (base + j) % sub
                a = pl.multiple_of(r0 + (base + j - r), sub)
                src = ubuf[pl.ds(a, rc), l0:l0 + lc] if r == 0 else sbuf[r - 1, pl.ds(a, rc), :]
                acc = acc + src * dw_ref[j:j + 1, l0:l0 + lc]
            cbuf[pl.ds(r0, rc), l0:l0 + lc] = acc
            return carry

        lax.fori_loop(0, tm // rc, conv_rows, 0)
    c = cbuf[...]
    mu = jnp.mean(c, axis=-1, keepdims=True)
    cz = c - mu
    var = jnp.mean(cz * cz, axis=-1, keepdims=True)
    y = _silu(cz * lax.rsqrt(var + NORM_EPS) * lg_ref[...] + lb_ref[...]).astype(BF16)
    y = _dot(y, w_ref[...]) + b2_ref[...]
    h1 = x_ref[0] + gate_ref[0] * y
    _route_tail(h1, g2_ref, sh2_ref, sc2_ref, rwt_ref, rb_ref, su_ref,
                h1_ref, xm_ref, pos_ref, wgt_ref, cnt_ref, td=td)


def _conf_out_route(u, dw, dwb, ln_g, ln_b, w_bf, b2, x, gate, tail_args, tm, td):
    b, s, d = x.shape
    hal = CONV_HALO
    per = tm // hal
    nh = s // hal
    tok = lambda bi, ti: (bi, ti, 0)
    vec = lambda bi, ti: (bi, 0, 0)
    const = lambda bi, ti: (0, 0)
    t_in, out_specs, out_shape = _tail_specs(b, s, d, tm)
    return pl.pallas_call(
        functools.partial(_conf_out_kernel, rc=32, td=td),
        grid=(b, s // tm),
        in_specs=[pl.BlockSpec((1, hal, d), lambda bi, ti: (bi, jnp.maximum(ti * per - 1, 0), 0)),
                  pl.BlockSpec((1, tm, d), tok),
                  pl.BlockSpec((1, hal, d), lambda bi, ti: (bi, jnp.minimum((ti + 1) * per, nh - 1), 0)),
                  pl.BlockSpec(dw.shape, const), pl.BlockSpec((1, d), const), pl.BlockSpec((1, d), const),
                  pl.BlockSpec((1, d), const), pl.BlockSpec(w_bf.shape, const), pl.BlockSpec((1, d), const),
                  pl.BlockSpec((1, tm, d), tok), pl.BlockSpec((1, 1, d), vec)] + t_in,
        out_specs=out_specs, out_shape=out_shape,
        scratch_shapes=[pltpu.VMEM((tm + 2 * hal, d), F32), pltpu.VMEM((tm, d), F32),
                        pltpu.VMEM((SUBLANES - 1, tm + (CONV_WIDTH // SUBLANES) * SUBLANES, min(d, 2 * LANES)), F32)],
        compiler_params=_params(("arbitrary", "arbitrary"), VMEM_LIMIT),
    )(u, u, u, dw, dwb, ln_g, ln_b, w_bf, b2, x, gate, *tail_args)


def _copy_rows(src_ref, src0, dst_ref, dst0, length, sem, *, wait, src_fixed=False):
    def piece(off, size):
        s = src0 if src_fixed else pl.multiple_of(src0 + off, ROW_ALIGN)
        return pltpu.make_async_copy(src_ref.at[pl.ds(s, size), :],
                                     dst_ref.at[pl.ds(pl.multiple_of(dst0 + off, ROW_ALIGN), size), :], sem)

    def go(copy):
        if wait:
            copy.wait()
        else:
            copy.start()

    def big(c, carry):
        go(piece(c * RUN_CHUNK, RUN_CHUNK))
        return carry

    lax.fori_loop(0, length // RUN_CHUNK, big, 0)
    size = RUN_CHUNK // 2
    while size >= ROW_ALIGN:
        @pl.when((length & size) != 0)
        def _(size=size):
            go(piece(length & ~(2 * size - 1), size))
        size //= 2


def _moe_dispatch_kernel(gap_ref, prev_ref, meta_ref, x_ref, pos_ref, xs_ref, ybuf, zbuf, sems):
    i = pl.program_id(0)
    last = pl.num_programs(0) - 1
    slot = i % 2
    cap, td = ybuf.shape[1], x_ref.shape[0]
    pos = pos_ref[...]
    rid = lax.broadcasted_iota(I32, (cap, td), 0)
    hit = rid == pos[0:1]
    for k in range(1, TOP_K):
        hit = jnp.logical_or(hit, rid == pos[k:k + 1])
    ybuf[slot] = _pack_bf16_pairs(_dot(jnp.where(hit, 1.0, 0.0).astype(BF16), x_ref[...]))

    def runs(m_ref, buf_slot, wait):
        def run(e, carry):
            _copy_rows(ybuf.at[buf_slot], m_ref[0, 0, e], xs_ref, m_ref[0, 2, e], m_ref[0, 1, e], sems.at[buf_slot],
                       wait=wait)
            return carry
        lax.fori_loop(0, N_EXPERTS, run, 0)

    @pl.when(i > 0)
    def _():
        runs(prev_ref, 1 - slot, True)

    runs(meta_ref, slot, False)

    @pl.when(i == last)
    def _():
        runs(meta_ref, slot, True)
        zbuf[...] = jnp.zeros_like(zbuf)
        for wait in (False, True):
            def fill(e, carry, wait=wait):
                _copy_rows(zbuf, 0, xs_ref, gap_ref[e], gap_ref[N_EXPERTS + e], sems.at[slot], wait=wait, src_fixed=True)
                return carry
            lax.fori_loop(0, N_EXPERTS, fill, 0)


def _moe_dispatch(gap, meta, xm, pos, p, td):
    n, d = xm.shape
    cap = TOP_K * td + N_EXPERTS * ROW_ALIGN
    meta_spec = lambda im: pl.BlockSpec((1, 3, N_EXPERTS), im, memory_space=pltpu.SMEM)
    grid_spec = pltpu.PrefetchScalarGridSpec(
        num_scalar_prefetch=1,
        grid=(n // td,),
        in_specs=[meta_spec(lambda i, g: (jnp.maximum(i - 1, 0), 0, 0)), meta_spec(lambda i, g: (i, 0, 0)),
                  pl.BlockSpec((td, d), lambda i, g: (i, 0)),
                  pl.BlockSpec((TOP_K, td), lambda i, g: (0, i))],
        out_specs=pl.BlockSpec(memory_space=pl.ANY),
        scratch_shapes=[pltpu.VMEM((2, cap, d // 2), U32), pltpu.VMEM((RUN_CHUNK, d // 2), U32),
                        pltpu.SemaphoreType.DMA((2,))],
    )
    return pl.pallas_call(
        _moe_dispatch_kernel,
        grid_spec=grid_spec,
        out_shape=jax.ShapeDtypeStruct((p, d // 2), U32),
        compiler_params=_params(("arbitrary",), VMEM_LIMIT),
    )(gap, meta, meta, xm, pos)


def _moe_expert_kernel(te_ref, nv_ref, xs_ref, w1_ref, b1_ref, w2_ref, b2_ref, ys_ref, w1b, w2b):
    i = pl.program_id(0)
    f = w2_ref.shape[2]
    changed = jnp.logical_or(i == 0, te_ref[i] != te_ref[jnp.maximum(i - 1, 0)])

    @pl.when(jnp.logical_and(changed, i < nv_ref[0]))
    def _():
        w1b[...] = w1_ref[0, 0].astype(BF16)
        w2b[...] = w2_ref[0, 0].astype(BF16)

    @pl.when(i < nv_ref[0])
    def _():
        lo, hi = _unpack_bf16_pairs(xs_ref[...])
        x = jnp.concatenate([lo, hi], axis=-1).astype(BF16)
        y = jnp.zeros((x.shape[0], w2_ref.shape[3]), F32) + b2_ref[0, 0]
        fc = 512 if f % 512 == 0 else f
        for c in range(f // fc):
            glu = _dot(x, w1b[:, c * fc:(c + 1) * fc]) + b1_ref[0, 0, :, c * fc:(c + 1) * fc]
            lin = _dot(x, w1b[:, f + c * fc:f + (c + 1) * fc]) + b1_ref[0, 0, :, f + c * fc:f + (c + 1) * fc]
            glu = jnp.minimum(glu, SWIGLU_LIMIT)
            lin = jnp.clip(lin, -SWIGLU_LIMIT, SWIGLU_LIMIT)
            act = glu * jax.nn.sigmoid(SWIGLU_ALPHA * glu) * (lin + 1.0)
            y = y + _dot(act.astype(BF16), w2b[c * fc:(c + 1) * fc, :])
        ys_ref[...] = _pack_bf16_pairs(y)


def _moe_experts(tile_expert, n_valid, xs, layer, w1, b1, w2, b2, tme):
    p, d2 = xs.shape
    _, ne, d, f2 = w1.shape
    f = f2 // 2
    row = lambda i, te, nv: (jnp.minimum(i, nv[0] - 1), 0)
    wsel = lambda i, te, nv: (layer, te[jnp.minimum(i, nv[0] - 1)], 0, 0)
    grid_spec = pltpu.PrefetchScalarGridSpec(
        num_scalar_prefetch=2,
        grid=(p // tme,),
        in_specs=[pl.BlockSpec((tme, d2), row),
                  pl.BlockSpec((1, 1, d, f2), wsel), pl.BlockSpec((1, 1, 1, f2), wsel),
                  pl.BlockSpec((1, 1, f, d), wsel), pl.BlockSpec((1, 1, 1, d), wsel)],
        out_specs=pl.BlockSpec((tme, d2), row),
        scratch_shapes=[pltpu.VMEM((d, f2), BF16), pltpu.VMEM((f, d), BF16)],
    )
    depth = w1.shape[0]
    return pl.pallas_call(
        _moe_expert_kernel,
        grid_spec=grid_spec,
        out_shape=jax.ShapeDtypeStruct((p, d2), U32),
        compiler_params=_params(("arbitrary",), VMEM_LIMIT),
    )(tile_expert, n_valid, xs, w1, b1.reshape(depth, ne, 1, f2), w2, b2.reshape(depth, ne, 1, d))


def _moe_combine_kernel(meta_ref, next_ref, ys_ref, pos_ref, wt_ref, h_ref, gate_ref, o_ref, ybuf, sems):
    t = pl.program_id(0) * pl.num_programs(1) + pl.program_id(1)
    n_tiles = pl.num_programs(0) * pl.num_programs(1)
    slot = t % 2

    def runs(m_ref, buf_slot, wait):
        def run(e, carry):
            _copy_rows(ys_ref, m_ref[0, 2, e], ybuf.at[buf_slot], m_ref[0, 0, e], m_ref[0, 1, e], sems.at[buf_slot],
                       wait=wait)
            return carry
        lax.fori_loop(0, N_EXPERTS, run, 0)

    @pl.when(t == 0)
    def _():
        ybuf[...] = jnp.zeros_like(ybuf)
        runs(meta_ref, slot, False)

    @pl.when(t + 1 < n_tiles)
    def _():
        runs(next_ref, 1 - slot, False)

    runs(meta_ref, slot, True)

    cap, td = ybuf.shape[1], pos_ref.shape[0]
    pos = pos_ref[...]
    wt = wt_ref[...]
    cid = lax.broadcasted_iota(I32, (td, cap), 1)
    mix = jnp.zeros((td, cap), F32)
    for k in range(TOP_K):
        mix = mix + jnp.where(cid == pos[:, k:k + 1], wt[:, k:k + 1], 0.0)
    mix_hi = mix.astype(BF16)
    mix_lo = (mix - mix_hi.astype(F32)).astype(BF16)
    lo, hi = _unpack_bf16_pairs(ybuf[slot])
    y = jnp.concatenate([lo, hi], axis=-1).astype(BF16)
    o_ref[0] = h_ref[0] + gate_ref[0] * (_dot(mix_hi, y) + _dot(mix_lo, y))


def _moe_combine(meta, ys, pos_t, wt_t, h, gate, td):
    b, s, d = h.shape
    nt = s // td
    cap = TOP_K * td + N_EXPERTS * ROW_ALIGN
    flat = lambda bi, ti: (bi * nt + ti, 0)
    meta_spec = lambda im: pl.BlockSpec((1, 3, N_EXPERTS), im, memory_space=pltpu.SMEM)
    return pl.pallas_call(
        _moe_combine_kernel,
        grid=(b, nt),
        in_specs=[meta_spec(lambda bi, ti: (bi * nt + ti, 0, 0)),
                  meta_spec(lambda bi, ti: (jnp.minimum(bi * nt + ti + 1, b * nt - 1), 0, 0)),
                  pl.BlockSpec(memory_space=pl.ANY),
                  pl.BlockSpec((td, TOP_K), flat), pl.BlockSpec((td, TOP_K), flat),
                  pl.BlockSpec((1, td, d), lambda bi, ti: (bi, ti, 0)),
                  pl.BlockSpec((1, 1, d), lambda bi, ti: (bi, 0, 0))],
        out_specs=pl.BlockSpec((1, td, d), lambda bi, ti: (bi, ti, 0)),
        out_shape=jax.ShapeDtypeStruct((b, s, d), F32),
        scratch_shapes=[pltpu.VMEM((2, cap, d // 2), U32), pltpu.SemaphoreType.DMA((2,))],
        compiler_params=_params(("arbitrary", "arbitrary"), VMEM_LIMIT),
    )(meta, meta, ys, pos_t, wt_t, h, gate)


def _moe(h1, xm, pos, wgt, counts, gate, layer, w1, b1, w2, b2, tme, td):
    n, d = xm.shape
    per = (n // td) // counts.shape[0]
    cnt = jnp.round(counts[:, :, :per]).astype(I32).transpose(0, 2, 1).reshape(n // td, N_EXPERTS)
    run = (cnt + ROW_ALIGN - 1) // ROW_ALIGN * ROW_ALIGN
    local = jnp.cumsum(run, axis=1) - run
    tot = jnp.sum(run, axis=0)
    cap = (tot + tme - 1) // tme * tme
    ends = jnp.cumsum(cap)
    start = ends - cap
    glob = start[None, :] + jnp.cumsum(run, axis=0) - run
    meta = jnp.stack([local, run, glob], axis=1)
    gap = jnp.concatenate([start + tot, cap - tot]).astype(I32)
    p = (TOP_K * n + N_EXPERTS * ROW_ALIGN * (n // td) + N_EXPERTS * tme) // tme * tme
    tiles = jnp.arange(p // tme, dtype=I32)
    tile_expert = jnp.minimum(jnp.sum(tiles[:, None] >= (ends // tme)[None, :], axis=1), N_EXPERTS - 1).astype(I32)
    n_valid = (ends[-1:] // tme).astype(I32)
    xs = _moe_dispatch(gap, meta, xm, pos, p, td)
    ys = _moe_experts(tile_expert, n_valid, xs, layer, w1, b1, w2, b2, tme)
    return _moe_combine(meta, ys, pos.T, wgt.T, h1, gate, td)


def _rope_tables(s):
    pos = np.arange(s)
    lane = np.arange(LANES)
    dd = lane % NAT_HEAD_DIM
    n = NAT_HEAD_DIM // 4
    inv_freq = ROPE_BASE ** (-(dd % n).astype(np.float64) / n)
    p = np.where((dd // (NAT_HEAD_DIM // 2))[None, :] == 0, (pos // GRID_W)[:, None], (pos % GRID_W)[:, None])
    ang = (p.astype(np.float32) * inv_freq.astype(np.float32)[None, :]).astype(np.float32)
    sign = np.where((dd % (2 * n)) < n, -1.0, 1.0)[None, :]
    return jnp.asarray(np.cos(ang), F32), jnp.asarray(np.sin(ang) * sign, F32)


def _tile(n, want):
    t = min(want, n)
    while n % t:
        t //= 2
    return t


def kernel(x, c, ctx, c_ctx, ada_w, ada_b, norm1_g, norm2_g, ab_w_in, ab_w_out, nat_q_norm, nat_k_norm, nat_rpb, hgrn_lb, hgrn_o_norm, conv_w1, conv_b1, conv_dw, conv_dw_b, conv_ln_g, conv_ln_b, conv_w2, conv_b2, router_w, router_b, moe_w1, moe_b1, moe_w2, moe_b2):
    b, s, d = x.shape
    l = ctx.shape[1]
    rows = s // GRID_W
    assert ada_w.shape[0] == 2 and b < MOD_ROWS and rows >= NAT_KH and rows % NAT_KH == 0
    assert s % HGRN_CHUNK == 0 and l % HGRN_CHUNK == 0 and d % (2 * LANES) == 0
    tm = _tile(s, 512)
    td = _tile(s, 256)
    tme = 512 if TOP_K * b * s >= 512 * N_EXPERTS else 128

    cc = jnp.zeros((MOD_ROWS, d), F32).at[:b].set(c).at[b].set(c_ctx)
    mod = _modulation(cc, ada_w, ada_b)

    def mod_vec(layer, i):
        return mod[layer, :b, i * d:(i + 1) * d].reshape(b, 1, d)

    def ctx_vec(i):
        return jnp.broadcast_to(mod[0, b, i * d:(i + 1) * d].reshape(1, 1, d), (b, 1, d))

    tok = np.arange(tm)
    strict_upper = jnp.asarray((tok[:, None] < tok[None, :]) & (tok[:, None] // td == tok[None, :] // td), BF16)

    def tail_args(layer):
        return (norm2_g[layer].reshape(1, d), mod_vec(layer, 3), mod_vec(layer, 4),
                router_w[layer].T, router_b[layer].reshape(N_EXPERTS, 1), strict_upper)

    lb_all = jnp.cumsum(jax.nn.softmax(hgrn_lb.astype(F32), axis=1), axis=1)[:, 0]
    w_in = ab_w_in[0].astype(BF16)
    scale = NAT_HEAD_DIM ** -0.5
    qg = jnp.tile(nat_q_norm[0] * scale, NAT_HEADS).reshape(1, NAT_WIDTH)
    kg = jnp.tile(nat_k_norm[0], NAT_HEADS).reshape(1, NAT_WIDTH)
    head_of = np.arange(NAT_WIDTH) // NAT_HEAD_DIM
    bd = jnp.asarray(head_of[:, None] == head_of[None, :], BF16)
    lbf, lbb = lb_all[0].reshape(1, HGRN_WIDTH), lb_all[1].reshape(1, HGRN_WIDTH)
    cos, sin = _rope_tables(s)
    g1 = norm1_g[0].reshape(1, d)
    lat = _inproj(x, mod_vec(0, 0), mod_vec(0, 1), g1, w_in, cos, sin, qg, kg, bd, lbf, lbb, tm)
    qa, qb, k, v, hq, kf, lff, kb, lfb, hi, sg = lat
    tl = _tile(l, 256)
    cxt = _inproj(ctx, ctx_vec(0), ctx_vec(1), g1, w_in, jnp.ones((l, LANES), F32), jnp.zeros((l, LANES), F32),
                  qg, kg, bd, lbf, lbb, tl)
    _, _, kx, vx, _, kfx, lffx, kbx, lfbx, hix, _ = cxt

    nat = _nat_attention(qa, qb, k, v, kx, vx, _nat_bias_table(nat_rpb[0]), rows)

    tri_f = jnp.asarray(np.tril(np.ones((HGRN_CHUNK, HGRN_CHUNK), np.float32)))
    s0 = jnp.zeros((b, HGRN_HEADS, HGRN_DIM, HGRN_DIM), F32)
    _, sf = _hgrn_scan(None, kfx, lffx, hix, s0, tri_f, False, False)
    of, _ = _hgrn_scan(hq, kf, lff, hi, sf, tri_f, False, True)
    _, sb = _hgrn_scan(None, kbx, lfbx, hix, s0, tri_f.T, True, False)
    ob, _ = _hgrn_scan(hq, kb, lfb, hi, sb, tri_f.T, True, True)

    h1, xm, pos, wgt, counts = _outproj_route(
        nat, of, ob, sg, x, mod_vec(0, 2), hgrn_o_norm[0].reshape(1, HGRN_DIM), ab_w_out[0].astype(BF16),
        tail_args(0), tm, td)
    h = _moe(h1, xm, pos, wgt, counts, mod_vec(0, 5), 0, moe_w1, moe_b1, moe_w2, moe_b2, tme, td)

    u = _conf_in(h, mod_vec(1, 0), mod_vec(1, 1), norm1_g[1].reshape(1, d), conv_w1[0].astype(BF16),
                 conv_b1[0].reshape(1, 2 * d), tm)
    h1, xm, pos, wgt, counts = _conf_out_route(
        u, conv_dw[0], conv_dw_b[0].reshape(1, d), conv_ln_g[0].reshape(1, d), conv_ln_b[0].reshape(1, d),
        conv_w2[0].astype(BF16), conv_b2[0].reshape(1, d), h, mod_vec(1, 2), tail_args(1), tm, td)
    return _moe(h1, xm, pos, wgt, counts, mod_vec(1, 5), 1, moe_w1, moe_b1, moe_w2, moe_b2, tme, td)
```

```python
import functools

import numpy as np
import jax
import jax.numpy as jnp
from jax import lax
from jax.experimental import pallas as pl
from jax.experimental.pallas import tpu as pltpu

F32 = jnp.float32
BF16 = jnp.bfloat16
U32 = jnp.uint32
I32 = jnp.int32
HIGHEST = lax.Precision.HIGHEST

GRID_W = 64
NAT_HEADS = 8
NAT_HEAD_DIM = 64
NAT_WIDTH = NAT_HEADS * NAT_HEAD_DIM
NAT_KH = 8
NAT_KW = 16
HGRN_HEADS = 4
HGRN_DIM = 128
HGRN_WIDTH = HGRN_HEADS * HGRN_DIM
HGRN_CHUNK = 64
HGRN_SUB = 16
CONV_WIDTH = 31
CONV_HALO = 16
N_EXPERTS = 32
TOP_K = 4
SWIGLU_LIMIT = 7.0
SWIGLU_ALPHA = 1.702
ROPE_BASE = 10000.0
NORM_EPS = 1e-6
MASK_VALUE = -1e30
EXP_CLAMP = 80.0

SUBLANES = 8
ROW_ALIGN = SUBLANES
RUN_CHUNK = 64
LANES = 128
MOD_ROWS = 16
VMEM_LIMIT = 56 * 1024 * 1024
NT = (((1,), (1,)), ((), ()))


def _params(sem, vmem=None):
    return pltpu.CompilerParams(dimension_semantics=sem, vmem_limit_bytes=vmem)


def _dot(a, b):
    return jnp.dot(a, b, preferred_element_type=F32)


def _dot_nt(a, b):
    return lax.dot_general(a, b, NT, preferred_element_type=F32)


def _silu(x):
    return x * jax.nn.sigmoid(x)


def _rms(x, g):
    return x * lax.rsqrt(jnp.mean(x * x, axis=-1, keepdims=True) + NORM_EPS) * g


def _mod_kernel(cc_ref, w_ref, b_ref, o_ref):
    cc = cc_ref[...]
    o_ref[0] = jnp.dot(_silu(cc), w_ref[0], precision=HIGHEST, preferred_element_type=F32) + b_ref[0]


def _modulation(cc, ada_w, ada_b):
    depth, d, n = ada_w.shape
    tn = n // 4
    return pl.pallas_call(
        _mod_kernel,
        grid=(depth, n // tn),
        in_specs=[pl.BlockSpec((MOD_ROWS, d), lambda l, j: (0, 0)),
                  pl.BlockSpec((1, d, tn), lambda l, j: (l, 0, j)),
                  pl.BlockSpec((1, 1, tn), lambda l, j: (l, 0, j))],
        out_specs=pl.BlockSpec((1, MOD_ROWS, tn), lambda l, j: (l, 0, j)),
        out_shape=jax.ShapeDtypeStruct((depth, MOD_ROWS, n), F32),
        compiler_params=_params(("arbitrary", "arbitrary")),
    )(cc, ada_w, ada_b.reshape(depth, 1, n))


def _inproj_kernel(x_ref, sh_ref, sc_ref, g_ref, w_ref, cos_ref, sin_ref, qg_ref, kg_ref, bd_ref, lbf_ref, lbb_ref,
                   qa_ref, qb_ref, k_ref, v_ref, hq_ref, kf_ref, lff_ref, kb_ref, lfb_ref, hi_ref, sg_ref):
    x = x_ref[0]
    xm = (_rms(x, g_ref[...]) * (1.0 + sc_ref[0]) + sh_ref[0]).astype(BF16)
    wd = NAT_WIDTH

    def proj(i):
        return _dot(xm, w_ref[:, i * wd:(i + 1) * wd])

    def head_norm(y, g):
        sq = y * y
        hi = sq.astype(BF16)
        lo = (sq - hi.astype(F32)).astype(BF16)
        ss = _dot(hi, bd_ref[...]) + _dot(lo, bd_ref[...])
        return y * lax.rsqrt(ss * (1.0 / NAT_HEAD_DIM) + NORM_EPS) * g

    cos = cos_ref[...]
    sin = sin_ref[...]
    lane = lax.broadcasted_iota(I32, cos.shape, 1)
    first = (lane % 32) < 16

    def rope(y):
        outs = []
        for gi in range(wd // LANES):
            yg = y[:, gi * LANES:(gi + 1) * LANES]
            partner = jnp.where(first, pltpu.roll(yg, LANES - 16, 1), pltpu.roll(yg, 16, 1))
            outs.append(yg * cos + partner * sin)
        return jnp.concatenate(outs, axis=-1)

    qf = head_norm(proj(0), qg_ref[...])
    qa_ref[0] = rope(qf).astype(BF16)
    qb_ref[0] = qf.astype(BF16)
    k_ref[0] = rope(head_norm(proj(1), kg_ref[...])).astype(BF16)
    v_ref[0] = proj(2).astype(BF16)
    hq_ref[0] = (_silu(proj(3)) * (HGRN_DIM ** -0.5)).astype(BF16)
    for i, lb_ref, kk_ref, lf_ref in ((4, lbf_ref, kf_ref, lff_ref), (5, lbb_ref, kb_ref, lfb_ref)):
        lb = lb_ref[...]
        f = lb + (1.0 - lb) * jax.nn.sigmoid(proj(i))
        kk_ref[0] = (1.0 - f).astype(BF16)
        lf_ref[0] = jnp.log(f)
    hi_ref[0] = proj(6).astype(BF16)
    sg_ref[0] = _silu(proj(7)).astype(BF16)


def _inproj(x, shift, scale, g, w_bf, cos, sin, qg, kg, bd, lbf, lbb, tm):
    b, t, d = x.shape
    wd = NAT_WIDTH
    tok = lambda bi, ti: (bi, ti, 0)
    vec = lambda bi, ti: (bi, 0, 0)
    const = lambda bi, ti: (0, 0)
    out_dtypes = (BF16, BF16, BF16, BF16, BF16, BF16, F32, BF16, F32, BF16, BF16)
    return pl.pallas_call(
        _inproj_kernel,
        grid=(b, t // tm),
        in_specs=[pl.BlockSpec((1, tm, d), tok), pl.BlockSpec((1, 1, d), vec), pl.BlockSpec((1, 1, d), vec),
                  pl.BlockSpec((1, d), const), pl.BlockSpec(w_bf.shape, const),
                  pl.BlockSpec((tm, LANES), lambda bi, ti: (ti, 0)), pl.BlockSpec((tm, LANES), lambda bi, ti: (ti, 0)),
                  pl.BlockSpec((1, wd), const), pl.BlockSpec((1, wd), const), pl.BlockSpec((wd, wd), const),
                  pl.BlockSpec((1, wd), const), pl.BlockSpec((1, wd), const)],
        out_specs=[pl.BlockSpec((1, tm, wd), tok)] * len(out_dtypes),
        out_shape=[jax.ShapeDtypeStruct((b, t, wd), dt) for dt in out_dtypes],
        compiler_params=_params(("arbitrary", "arbitrary"), VMEM_LIMIT),
    )(x, shift, scale, g, w_bf, cos, sin, qg, kg, bd, lbf, lbb)


def _nat_kernel(qa_ref, qb_ref, kp_ref, kc_ref, kn_ref, vp_ref, vc_ref, vn_ref, kx_ref, vx_ref, bias_ref,
                o_ref, kbuf, vbuf, s1_scr, s2_scr, p1_scr, p2_scr, *, rows, rb):
    i = pl.program_id(1)
    blk = rb * GRID_W
    for s, (kr, vr) in enumerate(((kp_ref, vp_ref), (kc_ref, vc_ref), (kn_ref, vn_ref))):
        kbuf[s * blk:(s + 1) * blk, :] = kr[0]
        vbuf[s * blk:(s + 1) * blk, :] = vr[0]
    lane = lax.broadcasted_iota(I32, (GRID_W, LANES), 1)
    low = lane < NAT_HEAD_DIM
    kwin = NAT_KH * GRID_W
    pair = 2 * GRID_W
    sm_rows = 16

    def stack_heads(q2):
        q2 = q2.astype(F32)
        return jnp.concatenate([jnp.where(low, q2, 0.0), jnp.where(low, 0.0, q2)], axis=0).astype(BF16)

    def row_body(j, carry):
        r = i * rb + j
        rs = jnp.clip(r - NAT_KH // 2, 0, rows - NAT_KH)
        start = pl.multiple_of((rs - i * rb + rb) * GRID_W, GRID_W)
        cls = rs - r + NAT_KH - 1
        q0 = pl.multiple_of(j * GRID_W, GRID_W)
        for p in range(NAT_HEADS // 2):
            ls = slice(p * LANES, (p + 1) * LANES)
            bias = jnp.concatenate([bias_ref[cls, 2 * p], bias_ref[cls, 2 * p + 1]], axis=0)
            s1_scr[p * pair:(p + 1) * pair, :] = _dot_nt(stack_heads(qa_ref[0, pl.ds(q0, GRID_W), ls]),
                                                         kbuf[pl.ds(start, kwin), ls]) + bias
            s2_scr[p * pair:(p + 1) * pair, :] = _dot_nt(stack_heads(qb_ref[0, pl.ds(q0, GRID_W), ls]), kx_ref[0, :, ls])
        for c in range(NAT_HEADS * GRID_W // sm_rows):
            rsl = slice(c * sm_rows, (c + 1) * sm_rows)
            a = s1_scr[rsl, :]
            bb = s2_scr[rsl, :]
            m = jnp.maximum(jnp.max(a, axis=-1, keepdims=True), jnp.max(bb, axis=-1, keepdims=True))
            ea = jnp.exp(a - m)
            eb = jnp.exp(bb - m)
            inv = 1.0 / (jnp.sum(ea, axis=-1, keepdims=True) + jnp.sum(eb, axis=-1, keepdims=True))
            p1_scr[rsl, :] = (ea * inv).astype(BF16)
            p2_scr[rsl, :] = (eb * inv).astype(BF16)
        for p in range(NAT_HEADS // 2):
            ls = slice(p * LANES, (p + 1) * LANES)
            o = (_dot(p1_scr[p * pair:(p + 1) * pair, :], vbuf[pl.ds(start, kwin), ls])
                 + _dot(p2_scr[p * pair:(p + 1) * pair, :], vx_ref[0, :, ls]))
            o_ref[0, pl.ds(q0, GRID_W), ls] = jnp.where(low, o[:GRID_W], o[GRID_W:]).astype(BF16)
        return carry

    lax.fori_loop(0, rb, row_body, 0)


def _nat_attention(qa, qb, k, v, kx, vx, bias, rows):
    b, s, wd = qa.shape
    lx = kx.shape[1]
    rb = NAT_KH
    blk = rb * GRID_W
    nb = rows // rb
    cur = lambda bi, i: (bi, i, 0)
    prv = lambda bi, i: (bi, jnp.maximum(i - 1, 0), 0)
    nxt = lambda bi, i: (bi, jnp.minimum(i + 1, nb - 1), 0)
    ctx = lambda bi, i: (bi, 0, 0)
    tile = lambda im: pl.BlockSpec((1, blk, wd), im)
    return pl.pallas_call(
        functools.partial(_nat_kernel, rows=rows, rb=rb),
        grid=(b, nb),
        in_specs=[tile(cur), tile(cur), tile(prv), tile(cur), tile(nxt), tile(prv), tile(cur), tile(nxt),
                  pl.BlockSpec((1,) + kx.shape[1:], ctx), pl.BlockSpec((1,) + vx.shape[1:], ctx),
                  pl.BlockSpec(bias.shape, lambda bi, i: (0, 0, 0, 0))],
        out_specs=tile(cur),
        out_shape=jax.ShapeDtypeStruct((b, s, wd), BF16),
        scratch_shapes=[pltpu.VMEM((3 * blk, wd), BF16), pltpu.VMEM((3 * blk, wd), BF16),
                        pltpu.VMEM((NAT_HEADS * GRID_W, NAT_KH * GRID_W), F32), pltpu.VMEM((NAT_HEADS * GRID_W, lx), F32),
                        pltpu.VMEM((NAT_HEADS * GRID_W, NAT_KH * GRID_W), BF16), pltpu.VMEM((NAT_HEADS * GRID_W, lx), BF16)],
        compiler_params=_params(("arbitrary", "arbitrary"), VMEM_LIMIT),
    )(qa, qb, k, k, k, v, v, v, kx, vx, bias)


def _nat_bias_table(rpb):
    qc = np.arange(GRID_W)
    wc = np.clip(qc - NAT_KW // 2, 0, GRID_W - NAT_KW)
    kc = np.arange(GRID_W)
    valid = (kc[None, :] >= wc[:, None]) & (kc[None, :] < wc[:, None] + NAT_KW)
    cidx = np.clip(kc[None, :] - qc[:, None] + NAT_KW - 1, 0, 2 * NAT_KW - 2)
    ridx = np.arange(NAT_KH)[:, None] + np.arange(NAT_KH)[None, :]
    tab = rpb.astype(F32)[:, ridx][:, :, :, cidx]
    tab = jnp.where(valid[None, None, None], tab, MASK_VALUE)
    tab = tab.transpose(1, 0, 3, 2, 4)
    return tab.reshape(NAT_KH, NAT_HEADS, GRID_W, NAT_KH * GRID_W)


def _hgrn_kernel(*refs, rev, with_out, cb):
    if with_out:
        q_ref, kk_ref, lf_ref, v_ref, s0_ref, tri_ref, o_ref, sf_ref, st = refs
    else:
        kk_ref, lf_ref, v_ref, s0_ref, tri_ref, sf_ref, st = refs
    i = pl.program_id(1)
    c_sz = HGRN_CHUNK

    @pl.when(i == 0)
    def _():
        st[...] = s0_ref[0]

    tri = tri_ref[...]
    keep = tri > 0.5

    def chunk(ci, carry):
        c = (cb - 1 - ci) if rev else ci
        r0 = pl.multiple_of(c * c_sz, c_sz)
        bsum = jnp.dot(tri, lf_ref[0, pl.ds(r0, c_sz), :], precision=HIGHEST, preferred_element_type=F32)
        for h in range(HGRN_HEADS):
            ls = slice(h * HGRN_DIM, (h + 1) * HGRN_DIM)
            bh = bsum[:, ls]
            tot = bh[0:1] if rev else bh[c_sz - 1:c_sz]
            kh = kk_ref[0, pl.ds(r0, c_sz), ls].astype(F32)
            vh = v_ref[0, pl.ds(r0, c_sz), ls]
            state = st[h]
            if with_out:
                qh = q_ref[0, pl.ds(r0, c_sz), ls].astype(F32)
                blocks = []
                for sb in range(c_sz // HGRN_SUB):
                    lo = sb * HGRN_SUB
                    ref_row = lo + HGRN_SUB - 1 if rev else lo
                    cref = bh[ref_row:ref_row + 1]
                    f1 = jnp.exp(bh[lo:lo + HGRN_SUB] - cref)
                    f2 = jnp.exp(jnp.minimum(cref - bh, EXP_CLAMP))
                    blocks.append(_dot_nt((qh[lo:lo + HGRN_SUB] * f1).astype(BF16), (kh * f2).astype(BF16)))
                scores = jnp.where(keep, jnp.concatenate(blocks, axis=0), 0.0)
                o = _dot(scores.astype(BF16), vh) + _dot_nt((qh * jnp.exp(bh)).astype(BF16), state.astype(BF16))
                o_ref[0, pl.ds(r0, c_sz), ls] = o.astype(BF16)
            kd = (kh * jnp.exp(tot - bh)).astype(BF16)
            vt = vh.astype(F32).T.astype(BF16)
            st[h] = state * jnp.exp(tot) + _dot(vt, kd)
        return carry

    lax.fori_loop(0, cb, chunk, 0, unroll=2)

    @pl.when(i == pl.num_programs(1) - 1)
    def _():
        sf_ref[0] = st[...]


def _hgrn_scan(q, kk, lf, v, s0, tri, rev, with_out):
    b, t, wd = kk.shape
    nchunk = t // HGRN_CHUNK
    cb = min(8, nchunk)
    nblk = nchunk // cb
    tb = cb * HGRN_CHUNK
    tok = (lambda bi, i: (bi, nblk - 1 - i, 0)) if rev else (lambda bi, i: (bi, i, 0))
    st_map = lambda bi, i: (bi, 0, 0, 0)
    st_shape = (b, HGRN_HEADS, HGRN_DIM, HGRN_DIM)
    tile = pl.BlockSpec((1, tb, wd), tok)
    st_spec = pl.BlockSpec((1,) + st_shape[1:], st_map)
    ins = ([q] if with_out else []) + [kk, lf, v, s0, tri]
    in_specs = [tile] * (len(ins) - 2) + [st_spec, pl.BlockSpec(tri.shape, lambda bi, i: (0, 0))]
    out_specs = ([tile] if with_out else []) + [st_spec]
    out_shape = ([jax.ShapeDtypeStruct((b, t, wd), BF16)] if with_out else []) + [jax.ShapeDtypeStruct(st_shape, F32)]
    res = pl.pallas_call(
        functools.partial(_hgrn_kernel, rev=rev, with_out=with_out, cb=cb),
        grid=(b, nblk),
        in_specs=in_specs, out_specs=out_specs, out_shape=out_shape,
        scratch_shapes=[pltpu.VMEM(st_shape[1:], F32)],
        compiler_params=_params(("arbitrary", "arbitrary"), VMEM_LIMIT),
    )(*ins)
    return res if with_out else (None, res[0])


def _pack_bf16_pairs(x):
    half = x.shape[-1] // 2
    lo = pltpu.bitcast(x[:, :half].astype(BF16).astype(F32), U32) >> 16
    hi = pltpu.bitcast(x[:, half:].astype(BF16).astype(F32), U32) & jnp.uint32(0xFFFF0000)
    return hi | lo


def _unpack_bf16_pairs(w):
    lo = pltpu.bitcast(w << 16, F32)
    hi = pltpu.bitcast(w & jnp.uint32(0xFFFF0000), F32)
    return lo, hi


def _route_tail(h1, g2_ref, sh2_ref, sc2_ref, rwt_ref, rb_ref, su_ref,
                h1_ref, xm_ref, pos_ref, wgt_ref, cnt_ref, *, td):
    h1_ref[0] = h1
    xm2 = _rms(h1, g2_ref[...]) * (1.0 + sc2_ref[0]) + sh2_ref[0]
    xm_ref[...] = xm2.astype(BF16)
    logits = lax.dot_general(rwt_ref[...], xm2, NT, precision=HIGHEST, preferred_element_type=F32) + rb_ref[...]
    ne, tm = logits.shape
    eid = lax.broadcasted_iota(I32, (ne, tm), 0).astype(F32)
    vals, hots = [], []
    for k in range(TOP_K):
        m = jnp.max(logits, axis=0, keepdims=True)
        sel = jnp.min(jnp.where(logits == m, eid, float(ne)), axis=0, keepdims=True)
        hot = eid == sel
        logits = jnp.where(hot, -jnp.inf, logits)
        vals.append(m)
        hots.append(hot)
    es = [jnp.exp(vv - vals[0]) for vv in vals]
    den = es[0] + es[1] + es[2] + es[3]
    onehot = jnp.zeros((ne, tm), F32)
    for k in range(TOP_K):
        wgt_ref[k:k + 1, :] = es[k] / den
        onehot = onehot + hots[k].astype(F32)
    prefix = _dot(onehot.astype(BF16), su_ref[...])
    strict_lower = (lax.broadcasted_iota(I32, (ne, ne), 0) > lax.broadcasted_iota(I32, (ne, ne), 1)).astype(BF16)
    lane = lax.broadcasted_iota(I32, (ne, LANES), 1)
    counts = jnp.zeros((ne, LANES), F32)
    starts = []
    for j in range(tm // td):
        cj = jnp.sum(onehot[:, j * td:(j + 1) * td], axis=1, keepdims=True)
        padded = jnp.floor((cj + (ROW_ALIGN - 1)) * (1.0 / ROW_ALIGN)) * ROW_ALIGN
        group_start = _dot(strict_lower, jnp.broadcast_to(padded, (ne, LANES)).astype(BF16))
        starts.append(jnp.broadcast_to(group_start[:, 0:1], (ne, td)))
        counts = counts + jnp.where(lane == j, cj, 0.0)
    row = prefix + jnp.concatenate(starts, axis=1)
    for k in range(TOP_K):
        pos_ref[k:k + 1, :] = jnp.sum(jnp.where(hots[k], row, 0.0), axis=0, keepdims=True).astype(I32)
    cnt_ref[0] = counts


def _tail_specs(b, s, d, tm):
    nt = s // tm
    n = b * s
    vec = lambda bi, ti: (bi, 0, 0)
    const = lambda bi, ti: (0, 0)
    in_specs = [pl.BlockSpec((1, d), const), pl.BlockSpec((1, 1, d), vec), pl.BlockSpec((1, 1, d), vec),
                pl.BlockSpec((N_EXPERTS, d), const), pl.BlockSpec((N_EXPERTS, 1), const), pl.BlockSpec((tm, tm), const)]
    flat = lambda bi, ti: (0, bi * nt + ti)
    out_specs = [pl.BlockSpec((1, tm, d), lambda bi, ti: (bi, ti, 0)),
                 pl.BlockSpec((tm, d), lambda bi, ti: (bi * nt + ti, 0)),
                 pl.BlockSpec((TOP_K, tm), flat), pl.BlockSpec((TOP_K, tm), flat),
                 pl.BlockSpec((1, N_EXPERTS, LANES), lambda bi, ti: (bi * nt + ti, 0, 0))]
    out_shape = [jax.ShapeDtypeStruct((b, s, d), F32), jax.ShapeDtypeStruct((n, d), BF16),
                 jax.ShapeDtypeStruct((TOP_K, n), I32), jax.ShapeDtypeStruct((TOP_K, n), F32),
                 jax.ShapeDtypeStruct((b * nt, N_EXPERTS, LANES), F32)]
    return in_specs, out_specs, out_shape


def _outproj_kernel(nat_ref, of_ref, ob_ref, sg_ref, x_ref, gate_ref, on_ref, w_ref,
                    g2_ref, sh2_ref, sc2_ref, rwt_ref, rb_ref, su_ref,
                    h1_ref, xm_ref, pos_ref, wgt_ref, cnt_ref, *, td):
    o = of_ref[0].astype(F32) + ob_ref[0].astype(F32)
    gated = []
    for h in range(HGRN_HEADS):
        ls = slice(h * HGRN_DIM, (h + 1) * HGRN_DIM)
        gated.append(_rms(o[:, ls], on_ref[...]) * sg_ref[0, :, ls].astype(F32))
    gated = jnp.concatenate(gated, axis=-1).astype(BF16)
    y = _dot(nat_ref[0], w_ref[:NAT_WIDTH, :]) + _dot(gated, w_ref[NAT_WIDTH:, :])
    h1 = x_ref[0] + gate_ref[0] * y
    _route_tail(h1, g2_ref, sh2_ref, sc2_ref, rwt_ref, rb_ref, su_ref,
                h1_ref, xm_ref, pos_ref, wgt_ref, cnt_ref, td=td)


def _outproj_route(nat, of, ob, sg, x, gate, on_g, w_bf, tail_args, tm, td):
    b, s, d = x.shape
    tok = lambda bi, ti: (bi, ti, 0)
    vec = lambda bi, ti: (bi, 0, 0)
    const = lambda bi, ti: (0, 0)
    t_in, out_specs, out_shape = _tail_specs(b, s, d, tm)
    wide = pl.BlockSpec((1, tm, NAT_WIDTH), tok)
    return pl.pallas_call(
        functools.partial(_outproj_kernel, td=td),
        grid=(b, s // tm),
        in_specs=[wide, wide, wide, wide, pl.BlockSpec((1, tm, d), tok), pl.BlockSpec((1, 1, d), vec),
                  pl.BlockSpec((1, HGRN_DIM), const), pl.BlockSpec(w_bf.shape, const)] + t_in,
        out_specs=out_specs, out_shape=out_shape,
        compiler_params=_params(("arbitrary", "arbitrary"), VMEM_LIMIT),
    )(nat, of, ob, sg, x, gate, on_g, w_bf, *tail_args)


def _conf_in_kernel(x_ref, sh_ref, sc_ref, g_ref, w_ref, b_ref, u_ref):
    d = x_ref.shape[-1]
    xm = (_rms(x_ref[0], g_ref[...]) * (1.0 + sc_ref[0]) + sh_ref[0]).astype(BF16)
    a = _dot(xm, w_ref[:, :d]) + b_ref[:, :d]
    gate = _dot(xm, w_ref[:, d:]) + b_ref[:, d:]
    u_ref[0] = (a * jax.nn.sigmoid(gate)).astype(BF16)


def _conf_in(x, shift, scale, g, w_bf, b1, tm):
    b, s, d = x.shape
    tok = lambda bi, ti: (bi, ti, 0)
    vec = lambda bi, ti: (bi, 0, 0)
    const = lambda bi, ti: (0, 0)
    return pl.pallas_call(
        _conf_in_kernel,
        grid=(b, s // tm),
        in_specs=[pl.BlockSpec((1, tm, d), tok), pl.BlockSpec((1, 1, d), vec), pl.BlockSpec((1, 1, d), vec),
                  pl.BlockSpec((1, d), const), pl.BlockSpec(w_bf.shape, const), pl.BlockSpec((1, 2 * d), const)],
        out_specs=pl.BlockSpec((1, tm, d), tok),
        out_shape=jax.ShapeDtypeStruct((b, s, d), BF16),
        compiler_params=_params(("arbitrary", "arbitrary"), VMEM_LIMIT),
    )(x, shift, scale, g, w_bf, b1)


def _conf_out_kernel(up_ref, uc_ref, un_ref, dw_ref, dwb_ref, lg_ref, lb_ref, w_ref, b2_ref, x_ref, gate_ref,
                     g2_ref, sh2_ref, sc2_ref, rwt_ref, rb_ref, su_ref,
                     h1_ref, xm_ref, pos_ref, wgt_ref, cnt_ref, ubuf, cbuf, sbuf, *, rc, td):
    ti = pl.program_id(1)
    tm, d = uc_ref.shape[1], uc_ref.shape[2]
    hal = CONV_HALO
    prev = up_ref[0].astype(F32)
    nxt = un_ref[0].astype(F32)
    ubuf[0:hal, :] = jnp.where(ti > 0, prev, jnp.zeros_like(prev))
    ubuf[hal:hal + tm, :] = uc_ref[0].astype(F32)
    ubuf[hal + tm:, :] = jnp.where(ti < pl.num_programs(1) - 1, nxt, jnp.zeros_like(nxt))
    base = hal - CONV_WIDTH // 2
    lc = sbuf.shape[2]
    sub = sbuf.shape[0] + 1
    srows = sbuf.shape[1]
    bc = 64

    def shift_chunk(r0, nrows, l0):
        win = ubuf[pl.ds(r0, nrows + sub), l0:l0 + lc]
        for r in range(1, sub):
            sbuf[r - 1, pl.ds(r0, nrows), :] = win[r:r + nrows]

    for l0 in range(0, d, lc):
        def shift_rows(ci, carry, l0=l0):
            shift_chunk(pl.multiple_of(ci * bc, bc), bc, l0)
            return carry

        lax.fori_loop(0, srows // bc, shift_rows, 0)
        if srows % bc:
            shift_chunk(srows // bc * bc, srows % bc, l0)

        def conv_rows(ci, carry, l0=l0):
            r0 = pl.multiple_of(ci * rc, rc)
            acc = jnp.zeros((rc, lc), F32) + dwb_ref[:, l0:l0 + lc]
            for j in range(CONV_WIDTH):
                r = (base + j) % sub
                a = pl.multiple_of(r0 + (base + j - r), sub)
                src = ubuf[pl.ds(a, rc), l0:l0 + lc] if r == 0 else sbuf[r - 1, pl.ds(a, rc), :]
                acc = acc + src * dw_ref[j:j + 1, l0:l0 + lc]
            cbuf[pl.ds(r0, rc), l0:l0 + lc] = acc
            return carry

        lax.fori_loop(0, tm // rc, conv_rows, 0)
    c = cbuf[...]
    mu = jnp.mean(c, axis=-1, keepdims=True)
    cz = c - mu
    var = jnp.mean(cz * cz, axis=-1, keepdims=True)
    y = _silu(cz * lax.rsqrt(var + NORM_EPS) * lg_ref[...] + lb_ref[...]).astype(BF16)
    y = _dot(y, w_ref[...]) + b2_ref[...]
    h1 = x_ref[0] + gate_ref[0] * y
    _route_tail(h1, g2_ref, sh2_ref, sc2_ref, rwt_ref, rb_ref, su_ref,
                h1_ref, xm_ref, pos_ref, wgt_ref, cnt_ref, td=td)


def _conf_out_route(u, dw, dwb, ln_g, ln_b, w_bf, b2, x, gate, tail_args, tm, td):
    b, s, d = x.shape
    hal = CONV_HALO
    per = tm // hal
    nh = s // hal
    tok = lambda bi, ti: (bi, ti, 0)
    vec = lambda bi, ti: (bi, 0, 0)
    const = lambda bi, ti: (0, 0)
    t_in, out_specs, out_shape = _tail_specs(b, s, d, tm)
    return pl.pallas_call(
        functools.partial(_conf_out_kernel, rc=64, td=td),
        grid=(b, s // tm),
        in_specs=[pl.BlockSpec((1, hal, d), lambda bi, ti: (bi, jnp.maximum(ti * per - 1, 0), 0)),
                  pl.BlockSpec((1, tm, d), tok),
                  pl.BlockSpec((1, hal, d), lambda bi, ti: (bi, jnp.minimum((ti + 1) * per, nh - 1), 0)),
                  pl.BlockSpec(dw.shape, const), pl.BlockSpec((1, d), const), pl.BlockSpec((1, d), const),
                  pl.BlockSpec((1, d), const), pl.BlockSpec(w_bf.shape, const), pl.BlockSpec((1, d), const),
                  pl.BlockSpec((1, tm, d), tok), pl.BlockSpec((1, 1, d), vec)] + t_in,
        out_specs=out_specs, out_shape=out_shape,
        scratch_shapes=[pltpu.VMEM((tm + 2 * hal, d), F32), pltpu.VMEM((tm, d), F32),
                        pltpu.VMEM((SUBLANES - 1, tm + (CONV_WIDTH // SUBLANES) * SUBLANES, min(d, 2 * LANES)), F32)],
        compiler_params=_params(("arbitrary", "arbitrary"), VMEM_LIMIT),
    )(u, u, u, dw, dwb, ln_g, ln_b, w_bf, b2, x, gate, *tail_args)


def _copy_rows(src_ref, src0, dst_ref, dst0, length, sem, *, wait, src_fixed=False):
    def piece(off, size):
        s = src0 if src_fixed else pl.multiple_of(src0 + off, ROW_ALIGN)
        return pltpu.make_async_copy(src_ref.at[pl.ds(s, size), :],
                                     dst_ref.at[pl.ds(pl.multiple_of(dst0 + off, ROW_ALIGN), size), :], sem)

    def go(copy):
        if wait:
            copy.wait()
        else:
            copy.start()

    def big(c, carry):
        go(piece(c * RUN_CHUNK, RUN_CHUNK))
        return carry

    lax.fori_loop(0, length // RUN_CHUNK, big, 0)
    size = RUN_CHUNK // 2
    while size >= ROW_ALIGN:
        @pl.when((length & size) != 0)
        def _(size=size):
            go(piece(length & ~(2 * size - 1), size))
        size //= 2


def _wait_rows(src_ref, dst_ref, total, max_rows, sem):
    size = pl.next_power_of_2(max_rows)
    while size >= ROW_ALIGN:
        if size <= max_rows:
            @pl.when((total & size) != 0)
            def _(size=size):
                pltpu.make_async_copy(src_ref.at[pl.ds(0, size), :], dst_ref.at[pl.ds(0, size), :], sem).wait()
        size //= 2


def _moe_dispatch_kernel(gap_ref, prev_ref, meta_ref, x_ref, pos_ref, xs_ref, ybuf, zbuf, sems):
    i = pl.program_id(0)
    last = pl.num_programs(0) - 1
    slot = i % 2
    cap, td = ybuf.shape[1], x_ref.shape[0]
    pos = pos_ref[...]
    rid = lax.broadcasted_iota(I32, (cap, td), 0)
    hit = rid == pos[0:1]
    for k in range(1, TOP_K):
        hit = jnp.logical_or(hit, rid == pos[k:k + 1])
    ybuf[slot] = _pack_bf16_pairs(_dot(jnp.where(hit, 1.0, 0.0).astype(BF16), x_ref[...]))

    def wait_runs(m_ref, buf_slot):
        _wait_rows(ybuf.at[buf_slot], xs_ref, m_ref[0, 3, 0], cap, sems.at[buf_slot])

    @pl.when(i > 0)
    def _():
        wait_runs(prev_ref, 1 - slot)

    def run(e, carry):
        _copy_rows(ybuf.at[slot], meta_ref[0, 0, e], xs_ref, meta_ref[0, 2, e], meta_ref[0, 1, e], sems.at[slot],
                   wait=False)
        return carry

    lax.fori_loop(0, N_EXPERTS, run, 0)

    @pl.when(i == last)
    def _():
        wait_runs(meta_ref, slot)
        zbuf[...] = jnp.zeros_like(zbuf)
        for wait in (False, True):
            def fill(e, carry, wait=wait):
                _copy_rows(zbuf, 0, xs_ref, gap_ref[e], gap_ref[N_EXPERTS + e], sems.at[slot], wait=wait, src_fixed=True)
                return carry
            lax.fori_loop(0, N_EXPERTS, fill, 0)


def _moe_dispatch(gap, meta, xm, pos, p, td):
    n, d = xm.shape
    cap = TOP_K * td + N_EXPERTS * ROW_ALIGN
    meta_spec = lambda im: pl.BlockSpec((1, 4, N_EXPERTS), im, memory_space=pltpu.SMEM)
    grid_spec = pltpu.PrefetchScalarGridSpec(
        num_scalar_prefetch=1,
        grid=(n // td,),
        in_specs=[meta_spec(lambda i, g: (jnp.maximum(i - 1, 0), 0, 0)), meta_spec(lambda i, g: (i, 0, 0)),
                  pl.BlockSpec((td, d), lambda i, g: (i, 0)),
                  pl.BlockSpec((TOP_K, td), lambda i, g: (0, i))],
        out_specs=pl.BlockSpec(memory_space=pl.ANY),
        scratch_shapes=[pltpu.VMEM((2, cap, d // 2), U32), pltpu.VMEM((RUN_CHUNK, d // 2), U32),
                        pltpu.SemaphoreType.DMA((2,))],
    )
    return pl.pallas_call(
        _moe_dispatch_kernel,
        grid_spec=grid_spec,
        out_shape=jax.ShapeDtypeStruct((p, d // 2), U32),
        compiler_params=_params(("arbitrary",), VMEM_LIMIT),
    )(gap, meta, meta, xm, pos)


def _moe_expert_kernel(te_ref, nv_ref, xs_ref, w1_ref, b1_ref, w2_ref, b2_ref, ys_ref, w1b, w2b):
    i = pl.program_id(0)
    f = w2_ref.shape[2]
    changed = jnp.logical_or(i == 0, te_ref[i] != te_ref[jnp.maximum(i - 1, 0)])

    @pl.when(jnp.logical_and(changed, i < nv_ref[0]))
    def _():
        w1b[...] = w1_ref[0, 0].astype(BF16)
        w2b[...] = w2_ref[0, 0].astype(BF16)

    @pl.when(i < nv_ref[0])
    def _():
        lo, hi = _unpack_bf16_pairs(xs_ref[...])
        x = jnp.concatenate([lo, hi], axis=-1).astype(BF16)
        y = jnp.zeros((x.shape[0], w2_ref.shape[3]), F32) + b2_ref[0, 0]
        fc = 512 if f % 512 == 0 else f
        for c in range(f // fc):
            glu = _dot(x, w1b[:, c * fc:(c + 1) * fc]) + b1_ref[0, 0, :, c * fc:(c + 1) * fc]
            lin = _dot(x, w1b[:, f + c * fc:f + (c + 1) * fc]) + b1_ref[0, 0, :, f + c * fc:f + (c + 1) * fc]
            glu = jnp.minimum(glu, SWIGLU_LIMIT)
            lin = jnp.clip(lin, -SWIGLU_LIMIT, SWIGLU_LIMIT)
            act = glu * jax.nn.sigmoid(SWIGLU_ALPHA * glu) * (lin + 1.0)
            y = y + _dot(act.astype(BF16), w2b[c * fc:(c + 1) * fc, :])
        ys_ref[...] = _pack_bf16_pairs(y)


def _moe_experts(tile_expert, n_valid, xs, layer, w1, b1, w2, b2, tme):
    p, d2 = xs.shape
    _, ne, d, f2 = w1.shape
    f = f2 // 2
    row = lambda i, te, nv: (jnp.minimum(i, nv[0] - 1), 0)
    wsel = lambda i, te, nv: (layer, te[jnp.minimum(i, nv[0] - 1)], 0, 0)
    grid_spec = pltpu.PrefetchScalarGridSpec(
        num_scalar_prefetch=2,
        grid=(p // tme,),
        in_specs=[pl.BlockSpec((tme, d2), row),
                  pl.BlockSpec((1, 1, d, f2), wsel), pl.BlockSpec((1, 1, 1, f2), wsel),
                  pl.BlockSpec((1, 1, f, d), wsel), pl.BlockSpec((1, 1, 1, d), wsel)],
        out_specs=pl.BlockSpec((tme, d2), row),
        scratch_shapes=[pltpu.VMEM((d, f2), BF16), pltpu.VMEM((f, d), BF16)],
    )
    depth = w1.shape[0]
    return pl.pallas_call(
        _moe_expert_kernel,
        grid_spec=grid_spec,
        out_shape=jax.ShapeDtypeStruct((p, d2), U32),
        compiler_params=_params(("arbitrary",), VMEM_LIMIT),
    )(tile_expert, n_valid, xs, w1, b1.reshape(depth, ne, 1, f2), w2, b2.reshape(depth, ne, 1, d))


def _moe_combine_kernel(meta_ref, next_ref, ys_ref, pos_ref, wt_ref, h_ref, gate_ref, o_ref, ybuf, sems):
    t = pl.program_id(0) * pl.num_programs(1) + pl.program_id(1)
    n_tiles = pl.num_programs(0) * pl.num_programs(1)
    slot = t % 2

    cap, td = ybuf.shape[1], pos_ref.shape[0]

    def fetch_runs(m_ref, buf_slot):
        def run(e, carry):
            _copy_rows(ys_ref, m_ref[0, 2, e], ybuf.at[buf_slot], m_ref[0, 0, e], m_ref[0, 1, e], sems.at[buf_slot],
                       wait=False)
            return carry
        lax.fori_loop(0, N_EXPERTS, run, 0)

    @pl.when(t == 0)
    def _():
        ybuf[...] = jnp.zeros_like(ybuf)
        fetch_runs(meta_ref, slot)

    @pl.when(t + 1 < n_tiles)
    def _():
        fetch_runs(next_ref, 1 - slot)

    _wait_rows(ys_ref, ybuf.at[slot], meta_ref[0, 3, 0], cap, sems.at[slot])

    pos = pos_ref[...]
    wt = wt_ref[...]
    cid = lax.broadcasted_iota(I32, (td, cap), 1)
    mix = jnp.zeros((td, cap), F32)
    for k in range(TOP_K):
        mix = mix + jnp.where(cid == pos[:, k:k + 1], wt[:, k:k + 1], 0.0)
    lo, hi = _unpack_bf16_pairs(ybuf[slot])
    y = jnp.concatenate([lo, hi], axis=-1).astype(BF16)
    o_ref[0] = h_ref[0] + gate_ref[0] * _dot(mix.astype(BF16), y)


def _moe_combine(meta, ys, pos_t, wt_t, h, gate, td):
    b, s, d = h.shape
    nt = s // td
    cap = TOP_K * td + N_EXPERTS * ROW_ALIGN
    flat = lambda bi, ti: (bi * nt + ti, 0)
    meta_spec = lambda im: pl.BlockSpec((1, 4, N_EXPERTS), im, memory_space=pltpu.SMEM)
    return pl.pallas_call(
        _moe_combine_kernel,
        grid=(b, nt),
        in_specs=[meta_spec(lambda bi, ti: (bi * nt + ti, 0, 0)),
                  meta_spec(lambda bi, ti: (jnp.minimum(bi * nt + ti + 1, b * nt - 1), 0, 0)),
                  pl.BlockSpec(memory_space=pl.ANY),
                  pl.BlockSpec((td, TOP_K), flat), pl.BlockSpec((td, TOP_K), flat),
                  pl.BlockSpec((1, td, d), lambda bi, ti: (bi, ti, 0)),
                  pl.BlockSpec((1, 1, d), lambda bi, ti: (bi, 0, 0))],
        out_specs=pl.BlockSpec((1, td, d), lambda bi, ti: (bi, ti, 0)),
        out_shape=jax.ShapeDtypeStruct((b, s, d), F32),
        scratch_shapes=[pltpu.VMEM((2, cap, d // 2), U32), pltpu.SemaphoreType.DMA((2,))],
        compiler_params=_params(("arbitrary", "arbitrary"), VMEM_LIMIT),
    )(meta, meta, ys, pos_t, wt_t, h, gate)


def _moe(h1, xm, pos, wgt, counts, gate, layer, w1, b1, w2, b2, tme, td):
    n, d = xm.shape
    per = (n // td) // counts.shape[0]
    cnt = jnp.round(counts[:, :, :per]).astype(I32).transpose(0, 2, 1).reshape(n // td, N_EXPERTS)
    run = (cnt + ROW_ALIGN - 1) // ROW_ALIGN * ROW_ALIGN
    local = jnp.cumsum(run, axis=1) - run
    tot = jnp.sum(run, axis=0)
    cap = (tot + tme - 1) // tme * tme
    ends = jnp.cumsum(cap)
    start = ends - cap
    glob = start[None, :] + jnp.cumsum(run, axis=0) - run
    rows_used = jnp.broadcast_to(jnp.sum(run, axis=1, keepdims=True), run.shape)
    meta = jnp.stack([local, run, glob, rows_used], axis=1)
    gap = jnp.concatenate([start + tot, cap - tot]).astype(I32)
    p = (TOP_K * n + N_EXPERTS * ROW_ALIGN * (n // td) + N_EXPERTS * tme) // tme * tme
    tiles = jnp.arange(p // tme, dtype=I32)
    tile_expert = jnp.minimum(jnp.sum(tiles[:, None] >= (ends // tme)[None, :], axis=1), N_EXPERTS - 1).astype(I32)
    n_valid = (ends[-1:] // tme).astype(I32)
    xs = _moe_dispatch(gap, meta, xm, pos, p, td)
    ys = _moe_experts(tile_expert, n_valid, xs, layer, w1, b1, w2, b2, tme)
    return _moe_combine(meta, ys, pos.T, wgt.T, h1, gate, td)


def _rope_tables(s):
    pos = np.arange(s)
    lane = np.arange(LANES)
    dd = lane % NAT_HEAD_DIM
    n = NAT_HEAD_DIM // 4
    inv_freq = ROPE_BASE ** (-(dd % n).astype(np.float64) / n)
    p = np.where((dd // (NAT_HEAD_DIM // 2))[None, :] == 0, (pos // GRID_W)[:, None], (pos % GRID_W)[:, None])
    ang = (p.astype(np.float32) * inv_freq.astype(np.float32)[None, :]).astype(np.float32)
    sign = np.where((dd % (2 * n)) < n, -1.0, 1.0)[None, :]
    return jnp.asarray(np.cos(ang), F32), jnp.asarray(np.sin(ang) * sign, F32)


def _tile(n, want):
    t = min(want, n)
    while n % t:
        t //= 2
    return t


def kernel(x, c, ctx, c_ctx, ada_w, ada_b, norm1_g, norm2_g, ab_w_in, ab_w_out, nat_q_norm, nat_k_norm, nat_rpb, hgrn_lb, hgrn_o_norm, conv_w1, conv_b1, conv_dw, conv_dw_b, conv_ln_g, conv_ln_b, conv_w2, conv_b2, router_w, router_b, moe_w1, moe_b1, moe_w2, moe_b2):
    b, s, d = x.shape
    l = ctx.shape[1]
    rows = s // GRID_W
    assert ada_w.shape[0] == 2 and b < MOD_ROWS and rows >= NAT_KH and rows % NAT_KH == 0
    assert s % HGRN_CHUNK == 0 and l % HGRN_CHUNK == 0 and d % (2 * LANES) == 0
    tm = _tile(s, 512)
    td = tm
    tme = 512 if TOP_K * b * s >= 512 * N_EXPERTS else 128

    cc = jnp.zeros((MOD_ROWS, d), F32).at[:b].set(c).at[b].set(c_ctx)
    mod = _modulation(cc, ada_w, ada_b)

    def mod_vec(layer, i):
        return mod[layer, :b, i * d:(i + 1) * d].reshape(b, 1, d)

    def ctx_vec(i):
        return jnp.broadcast_to(mod[0, b, i * d:(i + 1) * d].reshape(1, 1, d), (b, 1, d))

    tok = np.arange(tm)
    strict_upper = jnp.asarray((tok[:, None] < tok[None, :]) & (tok[:, None] // td == tok[None, :] // td), BF16)

    def tail_args(layer):
        return (norm2_g[layer].reshape(1, d), mod_vec(layer, 3), mod_vec(layer, 4),
                router_w[layer].T, router_b[layer].reshape(N_EXPERTS, 1), strict_upper)

    lb_all = jnp.cumsum(jax.nn.softmax(hgrn_lb.astype(F32), axis=1), axis=1)[:, 0]
    w_in = ab_w_in[0].astype(BF16)
    scale = NAT_HEAD_DIM ** -0.5
    qg = jnp.tile(nat_q_norm[0] * scale, NAT_HEADS).reshape(1, NAT_WIDTH)
    kg = jnp.tile(nat_k_norm[0], NAT_HEADS).reshape(1, NAT_WIDTH)
    head_of = np.arange(NAT_WIDTH) // NAT_HEAD_DIM
    bd = jnp.asarray(head_of[:, None] == head_of[None, :], BF16)
    lbf, lbb = lb_all[0].reshape(1, HGRN_WIDTH), lb_all[1].reshape(1, HGRN_WIDTH)
    cos, sin = _rope_tables(s)
    g1 = norm1_g[0].reshape(1, d)
    lat = _inproj(x, mod_vec(0, 0), mod_vec(0, 1), g1, w_in, cos, sin, qg, kg, bd, lbf, lbb, tm)
    qa, qb, k, v, hq, kf, lff, kb, lfb, hi, sg = lat
    tl = _tile(l, 256)
    cxt = _inproj(ctx, ctx_vec(0), ctx_vec(1), g1, w_in, jnp.ones((l, LANES), F32), jnp.zeros((l, LANES), F32),
                  qg, kg, bd, lbf, lbb, tl)
    _, _, kx, vx, _, kfx, lffx, kbx, lfbx, hix, _ = cxt

    nat = _nat_attention(qa, qb, k, v, kx, vx, _nat_bias_table(nat_rpb[0]), rows)

    tri_f = jnp.asarray(np.tril(np.ones((HGRN_CHUNK, HGRN_CHUNK), np.float32)))
    s0 = jnp.zeros((b, HGRN_HEADS, HGRN_DIM, HGRN_DIM), F32)
    _, sf = _hgrn_scan(None, kfx, lffx, hix, s0, tri_f, False, False)
    of, _ = _hgrn_scan(hq, kf, lff, hi, sf, tri_f, False, True)
    _, sb = _hgrn_scan(None, kbx, lfbx, hix, s0, tri_f.T, True, False)
    ob, _ = _hgrn_scan(hq, kb, lfb, hi, sb, tri_f.T, True, True)

    h1, xm, pos, wgt, counts = _outproj_route(
        nat, of, ob, sg, x, mod_vec(0, 2), hgrn_o_norm[0].reshape(1, HGRN_DIM), ab_w_out[0].astype(BF16),
        tail_args(0), tm, td)
    h = _moe(h1, xm, pos, wgt, counts, mod_vec(0, 5), 0, moe_w1, moe_b1, moe_w2, moe_b2, tme, td)

    u = _conf_in(h, mod_vec(1, 0), mod_vec(1, 1), norm1_g[1].reshape(1, d), conv_w1[0].astype(BF16),
                 conv_b1[0].reshape(1, 2 * d), tm)
    h1, xm, pos, wgt, counts = _conf_out_route(
        u, conv_dw[0], conv_dw_b[0].reshape(1, d), conv_ln_g[0].reshape(1, d), conv_ln_b[0].reshape(1, d),
        conv_w2[0].astype(BF16), conv_b2[0].reshape(1, d), h, mod_vec(1, 2), tail_args(1), tm, td)
    return _moe(h1, xm, pos, wgt, counts, mod_vec(1, 5), 1, moe_w1, moe_b1, moe_w2, moe_b2, tme, td)
```

```python
import functools

import numpy as np
import jax
import jax.numpy as jnp
from jax import lax
from jax.experimental import pallas as pl
from jax.experimental.pallas import tpu as pltpu

F32 = jnp.float32
BF16 = jnp.bfloat16
U32 = jnp.uint32
I32 = jnp.int32
HIGHEST = lax.Precision.HIGHEST

GRID_W = 64
NAT_HEADS = 8
NAT_HEAD_DIM = 64
NAT_WIDTH = NAT_HEADS * NAT_HEAD_DIM
NAT_KH = 8
NAT_KW = 16
HGRN_HEADS = 4
HGRN_DIM = 128
HGRN_WIDTH = HGRN_HEADS * HGRN_DIM
HGRN_CHUNK = 64
HGRN_SUB = 16
CONV_WIDTH = 31
CONV_HALO = 16
N_EXPERTS = 32
TOP_K = 4
SWIGLU_LIMIT = 7.0
SWIGLU_ALPHA = 1.702
ROPE_BASE = 10000.0
NORM_EPS = 1e-6
MASK_VALUE = -1e30
EXP_CLAMP = 80.0

SUBLANES = 8
ROW_ALIGN = SUBLANES
RUN_CHUNK = 64
LANES = 128
MOD_ROWS = 16
VMEM_LIMIT = 56 * 1024 * 1024
NT = (((1,), (1,)), ((), ()))


def _params(sem, vmem=None):
    return pltpu.CompilerParams(dimension_semantics=sem, vmem_limit_bytes=vmem)


def _dot(a, b):
    return jnp.dot(a, b, preferred_element_type=F32)


def _dot_nt(a, b):
    return lax.dot_general(a, b, NT, preferred_element_type=F32)


def _silu(x):
    return x * jax.nn.sigmoid(x)


def _rms(x, g):
    return x * lax.rsqrt(jnp.mean(x * x, axis=-1, keepdims=True) + NORM_EPS) * g


def _mod_kernel(cc_ref, w_ref, b_ref, o_ref):
    cc = cc_ref[...]
    o_ref[0] = jnp.dot(_silu(cc), w_ref[0], precision=HIGHEST, preferred_element_type=F32) + b_ref[0]


def _modulation(cc, ada_w, ada_b):
    depth, d, n = ada_w.shape
    tn = n // 4
    return pl.pallas_call(
        _mod_kernel,
        grid=(depth, n // tn),
        in_specs=[pl.BlockSpec((MOD_ROWS, d), lambda l, j: (0, 0)),
                  pl.BlockSpec((1, d, tn), lambda l, j: (l, 0, j)),
                  pl.BlockSpec((1, 1, tn), lambda l, j: (l, 0, j))],
        out_specs=pl.BlockSpec((1, MOD_ROWS, tn), lambda l, j: (l, 0, j)),
        out_shape=jax.ShapeDtypeStruct((depth, MOD_ROWS, n), F32),
        compiler_params=_params(("arbitrary", "arbitrary")),
    )(cc, ada_w, ada_b.reshape(depth, 1, n))


def _inproj_kernel(x_ref, sh_ref, sc_ref, g_ref, w_ref, cos_ref, sin_ref, qg_ref, kg_ref, bd_ref, lbf_ref, lbb_ref,
                   qa_ref, qb_ref, k_ref, v_ref, hq_ref, kf_ref, lff_ref, kb_ref, lfb_ref, hi_ref, sg_ref):
    x = x_ref[0]
    xm = (_rms(x, g_ref[...]) * (1.0 + sc_ref[0]) + sh_ref[0]).astype(BF16)
    wd = NAT_WIDTH

    def proj(i):
        return _dot(xm, w_ref[:, i * wd:(i + 1) * wd])

    def head_norm(y, g):
        sq = y * y
        hi = sq.astype(BF16)
        lo = (sq - hi.astype(F32)).astype(BF16)
        ss = _dot(hi, bd_ref[...]) + _dot(lo, bd_ref[...])
        return y * lax.rsqrt(ss * (1.0 / NAT_HEAD_DIM) + NORM_EPS) * g

    cos = cos_ref[...]
    sin = sin_ref[...]
    lane = lax.broadcasted_iota(I32, cos.shape, 1)
    first = (lane % 32) < 16

    def rope(y):
        outs = []
        for gi in range(wd // LANES):
            yg = y[:, gi * LANES:(gi + 1) * LANES]
            partner = jnp.where(first, pltpu.roll(yg, LANES - 16, 1), pltpu.roll(yg, 16, 1))
            outs.append(yg * cos + partner * sin)
        return jnp.concatenate(outs, axis=-1)

    qf = head_norm(proj(0), qg_ref[...])
    qa_ref[0] = rope(qf).astype(BF16)
    qb_ref[0] = qf.astype(BF16)
    k_ref[0] = rope(head_norm(proj(1), kg_ref[...])).astype(BF16)
    v_ref[0] = proj(2).astype(BF16)
    hq_ref[0] = (_silu(proj(3)) * (HGRN_DIM ** -0.5)).astype(BF16)
    for i, lb_ref, kk_ref, lf_ref in ((4, lbf_ref, kf_ref, lff_ref), (5, lbb_ref, kb_ref, lfb_ref)):
        lb = lb_ref[...]
        f = lb + (1.0 - lb) * jax.nn.sigmoid(proj(i))
        kk_ref[0] = (1.0 - f).astype(BF16)
        lf_ref[0] = jnp.log(f)
    hi_ref[0] = proj(6).astype(BF16)
    sg_ref[0] = _silu(proj(7)).astype(BF16)


def _inproj(x, shift, scale, g, w_bf, cos, sin, qg, kg, bd, lbf, lbb, tm):
    b, t, d = x.shape
    wd = NAT_WIDTH
    tok = lambda bi, ti: (bi, ti, 0)
    vec = lambda bi, ti: (bi, 0, 0)
    const = lambda bi, ti: (0, 0)
    out_dtypes = (BF16, BF16, BF16, BF16, BF16, BF16, F32, BF16, F32, BF16, BF16)
    return pl.pallas_call(
        _inproj_kernel,
        grid=(b, t // tm),
        in_specs=[pl.BlockSpec((1, tm, d), tok), pl.BlockSpec((1, 1, d), vec), pl.BlockSpec((1, 1, d), vec),
                  pl.BlockSpec((1, d), const), pl.BlockSpec(w_bf.shape, const),
                  pl.BlockSpec((tm, LANES), lambda bi, ti: (ti, 0)), pl.BlockSpec((tm, LANES), lambda bi, ti: (ti, 0)),
                  pl.BlockSpec((1, wd), const), pl.BlockSpec((1, wd), const), pl.BlockSpec((wd, wd), const),
                  pl.BlockSpec((1, wd), const), pl.BlockSpec((1, wd), const)],
        out_specs=[pl.BlockSpec((1, tm, wd), tok)] * len(out_dtypes),
        out_shape=[jax.ShapeDtypeStruct((b, t, wd), dt) for dt in out_dtypes],
        compiler_params=_params(("arbitrary", "arbitrary"), VMEM_LIMIT),
    )(x, shift, scale, g, w_bf, cos, sin, qg, kg, bd, lbf, lbb)


def _nat_kernel(qa_ref, qb_ref, kp_ref, kc_ref, kn_ref, vp_ref, vc_ref, vn_ref, kx_ref, vx_ref, bias_ref,
                o_ref, kbuf, vbuf, s1_scr, s2_scr, p1_scr, p2_scr, *, rows, rb):
    i = pl.program_id(1)
    blk = rb * GRID_W
    for s, (kr, vr) in enumerate(((kp_ref, vp_ref), (kc_ref, vc_ref), (kn_ref, vn_ref))):
        kbuf[s * blk:(s + 1) * blk, :] = kr[0]
        vbuf[s * blk:(s + 1) * blk, :] = vr[0]
    lane = lax.broadcasted_iota(I32, (GRID_W, LANES), 1)
    low = lane < NAT_HEAD_DIM
    kwin = NAT_KH * GRID_W
    pair = 2 * GRID_W
    sm_rows = 16

    def stack_heads(q2):
        q2 = q2.astype(F32)
        return jnp.concatenate([jnp.where(low, q2, 0.0), jnp.where(low, 0.0, q2)], axis=0).astype(BF16)

    def window(j):
        r = i * rb + j
        rs = jnp.clip(r - NAT_KH // 2, 0, rows - NAT_KH)
        start = pl.multiple_of((rs - i * rb + rb) * GRID_W, GRID_W)
        return start, rs - r + NAT_KH - 1, pl.multiple_of(j * GRID_W, GRID_W)

    def scores(j, u):
        start, cls, q0 = window(j)
        for p in range(NAT_HEADS // 2):
            ls = slice(p * LANES, (p + 1) * LANES)
            bias = jnp.concatenate([bias_ref[cls, 2 * p], bias_ref[cls, 2 * p + 1]], axis=0)
            s1_scr[u, p * pair:(p + 1) * pair, :] = _dot_nt(stack_heads(qa_ref[0, pl.ds(q0, GRID_W), ls]),
                                                            kbuf[pl.ds(start, kwin), ls]) + bias
            s2_scr[u, p * pair:(p + 1) * pair, :] = _dot_nt(stack_heads(qb_ref[0, pl.ds(q0, GRID_W), ls]),
                                                            kx_ref[0, :, ls])

    def softmax(u):
        for c in range(NAT_HEADS * GRID_W // sm_rows):
            rsl = slice(c * sm_rows, (c + 1) * sm_rows)
            a = s1_scr[u, rsl, :]
            bb = s2_scr[u, rsl, :]
            m = jnp.maximum(jnp.max(a, axis=-1, keepdims=True), jnp.max(bb, axis=-1, keepdims=True))
            ea = jnp.exp(a - m)
            eb = jnp.exp(bb - m)
            inv = 1.0 / (jnp.sum(ea, axis=-1, keepdims=True) + jnp.sum(eb, axis=-1, keepdims=True))
            p1_scr[u, rsl, :] = (ea * inv).astype(BF16)
            p2_scr[u, rsl, :] = (eb * inv).astype(BF16)

    def values(j, u):
        start, _, q0 = window(j)
        for p in range(NAT_HEADS // 2):
            ls = slice(p * LANES, (p + 1) * LANES)
            o = (_dot(p1_scr[u, p * pair:(p + 1) * pair, :], vbuf[pl.ds(start, kwin), ls])
                 + _dot(p2_scr[u, p * pair:(p + 1) * pair, :], vx_ref[0, :, ls]))
            o_ref[0, pl.ds(q0, GRID_W), ls] = jnp.where(low, o[:GRID_W], o[GRID_W:]).astype(BF16)

    def row_pair(jp, carry):
        for u in range(2):
            scores(2 * jp + u, u)
        for u in range(2):
            softmax(u)
        for u in range(2):
            values(2 * jp + u, u)
        return carry

    lax.fori_loop(0, rb // 2, row_pair, 0)


def _nat_attention(qa, qb, k, v, kx, vx, bias, rows):
    b, s, wd = qa.shape
    lx = kx.shape[1]
    rb = NAT_KH
    blk = rb * GRID_W
    nb = rows // rb
    cur = lambda bi, i: (bi, i, 0)
    prv = lambda bi, i: (bi, jnp.maximum(i - 1, 0), 0)
    nxt = lambda bi, i: (bi, jnp.minimum(i + 1, nb - 1), 0)
    ctx = lambda bi, i: (bi, 0, 0)
    tile = lambda im: pl.BlockSpec((1, blk, wd), im)
    return pl.pallas_call(
        functools.partial(_nat_kernel, rows=rows, rb=rb),
        grid=(b, nb),
        in_specs=[tile(cur), tile(cur), tile(prv), tile(cur), tile(nxt), tile(prv), tile(cur), tile(nxt),
                  pl.BlockSpec((1,) + kx.shape[1:], ctx), pl.BlockSpec((1,) + vx.shape[1:], ctx),
                  pl.BlockSpec(bias.shape, lambda bi, i: (0, 0, 0, 0))],
        out_specs=tile(cur),
        out_shape=jax.ShapeDtypeStruct((b, s, wd), BF16),
        scratch_shapes=[pltpu.VMEM((3 * blk, wd), BF16), pltpu.VMEM((3 * blk, wd), BF16),
                        pltpu.VMEM((2, NAT_HEADS * GRID_W, NAT_KH * GRID_W), F32),
                        pltpu.VMEM((2, NAT_HEADS * GRID_W, lx), F32),
                        pltpu.VMEM((2, NAT_HEADS * GRID_W, NAT_KH * GRID_W), BF16),
                        pltpu.VMEM((2, NAT_HEADS * GRID_W, lx), BF16)],
        compiler_params=_params(("arbitrary", "arbitrary"), VMEM_LIMIT),
    )(qa, qb, k, k, k, v, v, v, kx, vx, bias)


def _nat_bias_table(rpb):
    qc = np.arange(GRID_W)
    wc = np.clip(qc - NAT_KW // 2, 0, GRID_W - NAT_KW)
    kc = np.arange(GRID_W)
    valid = (kc[None, :] >= wc[:, None]) & (kc[None, :] < wc[:, None] + NAT_KW)
    cidx = np.clip(kc[None, :] - qc[:, None] + NAT_KW - 1, 0, 2 * NAT_KW - 2)
    ridx = np.arange(NAT_KH)[:, None] + np.arange(NAT_KH)[None, :]
    tab = rpb.astype(F32)[:, ridx][:, :, :, cidx]
    tab = jnp.where(valid[None, None, None], tab, MASK_VALUE)
    tab = tab.transpose(1, 0, 3, 2, 4)
    return tab.reshape(NAT_KH, NAT_HEADS, GRID_W, NAT_KH * GRID_W)


def _hgrn_kernel(*refs, rev, with_out, cb):
    if with_out:
        q_ref, kk_ref, lf_ref, v_ref, s0_ref, tri_ref, o_ref, sf_ref, st = refs
    else:
        kk_ref, lf_ref, v_ref, s0_ref, tri_ref, sf_ref, st = refs
    i = pl.program_id(1)
    c_sz = HGRN_CHUNK

    @pl.when(i == 0)
    def _():
        st[...] = s0_ref[0]

    tri = tri_ref[...]
    keep = tri > 0.5

    def chunk(ci, carry):
        c = (cb - 1 - ci) if rev else ci
        r0 = pl.multiple_of(c * c_sz, c_sz)
        bsum = jnp.dot(tri, lf_ref[0, pl.ds(r0, c_sz), :], precision=HIGHEST, preferred_element_type=F32)
        for h in range(HGRN_HEADS):
            ls = slice(h * HGRN_DIM, (h + 1) * HGRN_DIM)
            bh = bsum[:, ls]
            tot = bh[0:1] if rev else bh[c_sz - 1:c_sz]
            kh = kk_ref[0, pl.ds(r0, c_sz), ls].astype(F32)
            vh = v_ref[0, pl.ds(r0, c_sz), ls]
            state = st[h]
            if with_out:
                qh = q_ref[0, pl.ds(r0, c_sz), ls].astype(F32)
                blocks = []
                for sb in range(c_sz // HGRN_SUB):
                    lo = sb * HGRN_SUB
                    ref_row = lo + HGRN_SUB - 1 if rev else lo
                    cref = bh[ref_row:ref_row + 1]
                    f1 = jnp.exp(bh[lo:lo + HGRN_SUB] - cref)
                    f2 = jnp.exp(jnp.minimum(cref - bh, EXP_CLAMP))
                    blocks.append(_dot_nt((qh[lo:lo + HGRN_SUB] * f1).astype(BF16), (kh * f2).astype(BF16)))
                scores = jnp.where(keep, jnp.concatenate(blocks, axis=0), 0.0)
                o = _dot(scores.astype(BF16), vh) + _dot_nt((qh * jnp.exp(bh)).astype(BF16), state.astype(BF16))
                o_ref[0, pl.ds(r0, c_sz), ls] = o.astype(BF16)
            kd = (kh * jnp.exp(tot - bh)).astype(BF16)
            vt = vh.astype(F32).T.astype(BF16)
            st[h] = state * jnp.exp(tot) + _dot(vt, kd)
        return carry

    lax.fori_loop(0, cb, chunk, 0, unroll=4)

    @pl.when(i == pl.num_programs(1) - 1)
    def _():
        sf_ref[0] = st[...]


def _hgrn_scan(q, kk, lf, v, s0, tri, rev, with_out):
    b, t, wd = kk.shape
    nchunk = t // HGRN_CHUNK
    cb = min(8, nchunk)
    nblk = nchunk // cb
    tb = cb * HGRN_CHUNK
    tok = (lambda bi, i: (bi, nblk - 1 - i, 0)) if rev else (lambda bi, i: (bi, i, 0))
    st_map = lambda bi, i: (bi, 0, 0, 0)
    st_shape = (b, HGRN_HEADS, HGRN_DIM, HGRN_DIM)
    tile = pl.BlockSpec((1, tb, wd), tok)
    st_spec = pl.BlockSpec((1,) + st_shape[1:], st_map)
    ins = ([q] if with_out else []) + [kk, lf, v, s0, tri]
    in_specs = [tile] * (len(ins) - 2) + [st_spec, pl.BlockSpec(tri.shape, lambda bi, i: (0, 0))]
    out_specs = ([tile] if with_out else []) + [st_spec]
    out_shape = ([jax.ShapeDtypeStruct((b, t, wd), BF16)] if with_out else []) + [jax.ShapeDtypeStruct(st_shape, F32)]
    res = pl.pallas_call(
        functools.partial(_hgrn_kernel, rev=rev, with_out=with_out, cb=cb),
        grid=(b, nblk),
        in_specs=in_specs, out_specs=out_specs, out_shape=out_shape,
        scratch_shapes=[pltpu.VMEM(st_shape[1:], F32)],
        compiler_params=_params(("arbitrary", "arbitrary"), VMEM_LIMIT),
    )(*ins)
    return res if with_out else (None, res[0])


def _pack_bf16_pairs(x):
    half = x.shape[-1] // 2
    lo = pltpu.bitcast(x[:, :half].astype(BF16).astype(F32), U32) >> 16
    hi = pltpu.bitcast(x[:, half:].astype(BF16).astype(F32), U32) & jnp.uint32(0xFFFF0000)
    return hi | lo


def _unpack_bf16_pairs(w):
    lo = pltpu.bitcast(w << 16, F32)
    hi = pltpu.bitcast(w & jnp.uint32(0xFFFF0000), F32)
    return lo, hi


def _route_tail(h1, g2_ref, sh2_ref, sc2_ref, rwt_ref, rb_ref, su_ref,
                h1_ref, xm_ref, pos_ref, wgt_ref, cnt_ref, *, td):
    h1_ref[0] = h1
    xm2 = _rms(h1, g2_ref[...]) * (1.0 + sc2_ref[0]) + sh2_ref[0]
    xm_ref[...] = xm2.astype(BF16)
    logits = lax.dot_general(rwt_ref[...], xm2, NT, precision=HIGHEST, preferred_element_type=F32) + rb_ref[...]
    ne, tm = logits.shape
    eid = lax.broadcasted_iota(I32, (ne, tm), 0).astype(F32)
    vals, hots = [], []
    for k in range(TOP_K):
        m = jnp.max(logits, axis=0, keepdims=True)
        sel = jnp.min(jnp.where(logits == m, eid, float(ne)), axis=0, keepdims=True)
        hot = eid == sel
        logits = jnp.where(hot, -jnp.inf, logits)
        vals.append(m)
        hots.append(hot)
    es = [jnp.exp(vv - vals[0]) for vv in vals]
    den = es[0] + es[1] + es[2] + es[3]
    onehot = jnp.zeros((ne, tm), F32)
    for k in range(TOP_K):
        wgt_ref[k:k + 1, :] = es[k] / den
        onehot = onehot + hots[k].astype(F32)
    prefix = _dot(onehot.astype(BF16), su_ref[...])
    strict_lower = (lax.broadcasted_iota(I32, (ne, ne), 0) > lax.broadcasted_iota(I32, (ne, ne), 1)).astype(BF16)
    lane = lax.broadcasted_iota(I32, (ne, LANES), 1)
    counts = jnp.zeros((ne, LANES), F32)
    starts = []
    for j in range(tm // td):
        cj = jnp.sum(onehot[:, j * td:(j + 1) * td], axis=1, keepdims=True)
        padded = jnp.floor((cj + (ROW_ALIGN - 1)) * (1.0 / ROW_ALIGN)) * ROW_ALIGN
        group_start = _dot(strict_lower, jnp.broadcast_to(padded, (ne, LANES)).astype(BF16))
        starts.append(jnp.broadcast_to(group_start[:, 0:1], (ne, td)))
        counts = counts + jnp.where(lane == j, cj, 0.0)
    row = prefix + jnp.concatenate(starts, axis=1)
    for k in range(TOP_K):
        pos_ref[k:k + 1, :] = jnp.sum(jnp.where(hots[k], row, 0.0), axis=0, keepdims=True).astype(I32)
    cnt_ref[0] = counts


def _tail_specs(b, s, d, tm):
    nt = s // tm
    n = b * s
    vec = lambda bi, ti: (bi, 0, 0)
    const = lambda bi, ti: (0, 0)
    in_specs = [pl.BlockSpec((1, d), const), pl.BlockSpec((1, 1, d), vec), pl.BlockSpec((1, 1, d), vec),
                pl.BlockSpec((N_EXPERTS, d), const), pl.BlockSpec((N_EXPERTS, 1), const), pl.BlockSpec((tm, tm), const)]
    flat = lambda bi, ti: (0, bi * nt + ti)
    out_specs = [pl.BlockSpec((1, tm, d), lambda bi, ti: (bi, ti, 0)),
                 pl.BlockSpec((tm, d), lambda bi, ti: (bi * nt + ti, 0)),
                 pl.BlockSpec((TOP_K, tm), flat), pl.BlockSpec((TOP_K, tm), flat),
                 pl.BlockSpec((1, N_EXPERTS, LANES), lambda bi, ti: (bi * nt + ti, 0, 0))]
    out_shape = [jax.ShapeDtypeStruct((b, s, d), F32), jax.ShapeDtypeStruct((n, d), BF16),
                 jax.ShapeDtypeStruct((TOP_K, n), I32), jax.ShapeDtypeStruct((TOP_K, n), F32),
                 jax.ShapeDtypeStruct((b * nt, N_EXPERTS, LANES), F32)]
    return in_specs, out_specs, out_shape


def _outproj_kernel(nat_ref, of_ref, ob_ref, sg_ref, x_ref, gate_ref, on_ref, w_ref,
                    g2_ref, sh2_ref, sc2_ref, rwt_ref, rb_ref, su_ref,
                    h1_ref, xm_ref, pos_ref, wgt_ref, cnt_ref, *, td):
    o = of_ref[0].astype(F32) + ob_ref[0].astype(F32)
    gated = []
    for h in range(HGRN_HEADS):
        ls = slice(h * HGRN_DIM, (h + 1) * HGRN_DIM)
        gated.append(_rms(o[:, ls], on_ref[...]) * sg_ref[0, :, ls].astype(F32))
    gated = jnp.concatenate(gated, axis=-1).astype(BF16)
    y = _dot(nat_ref[0], w_ref[:NAT_WIDTH, :]) + _dot(gated, w_ref[NAT_WIDTH:, :])
    h1 = x_ref[0] + gate_ref[0] * y
    _route_tail(h1, g2_ref, sh2_ref, sc2_ref, rwt_ref, rb_ref, su_ref,
                h1_ref, xm_ref, pos_ref, wgt_ref, cnt_ref, td=td)


def _outproj_route(nat, of, ob, sg, x, gate, on_g, w_bf, tail_args, tm, td):
    b, s, d = x.shape
    tok = lambda bi, ti: (bi, ti, 0)
    vec = lambda bi, ti: (bi, 0, 0)
    const = lambda bi, ti: (0, 0)
    t_in, out_specs, out_shape = _tail_specs(b, s, d, tm)
    wide = pl.BlockSpec((1, tm, NAT_WIDTH), tok)
    return pl.pallas_call(
        functools.partial(_outproj_kernel, td=td),
        grid=(b, s // tm),
        in_specs=[wide, wide, wide, wide, pl.BlockSpec((1, tm, d), tok), pl.BlockSpec((1, 1, d), vec),
                  pl.BlockSpec((1, HGRN_DIM), const), pl.BlockSpec(w_bf.shape, const)] + t_in,
        out_specs=out_specs, out_shape=out_shape,
        compiler_params=_params(("arbitrary", "arbitrary"), VMEM_LIMIT),
    )(nat, of, ob, sg, x, gate, on_g, w_bf, *tail_args)


def _conf_in_kernel(x_ref, sh_ref, sc_ref, g_ref, w_ref, b_ref, u_ref):
    d = x_ref.shape[-1]
    xm = (_rms(x_ref[0], g_ref[...]) * (1.0 + sc_ref[0]) + sh_ref[0]).astype(BF16)
    a = _dot(xm, w_ref[:, :d]) + b_ref[:, :d]
    gate = _dot(xm, w_ref[:, d:]) + b_ref[:, d:]
    u_ref[0] = (a * jax.nn.sigmoid(gate)).astype(BF16)


def _conf_in(x, shift, scale, g, w_bf, b1, tm):
    b, s, d = x.shape
    tok = lambda bi, ti: (bi, ti, 0)
    vec = lambda bi, ti: (bi, 0, 0)
    const = lambda bi, ti: (0, 0)
    return pl.pallas_call(
        _conf_in_kernel,
        grid=(b, s // tm),
        in_specs=[pl.BlockSpec((1, tm, d), tok), pl.BlockSpec((1, 1, d), vec), pl.BlockSpec((1, 1, d), vec),
                  pl.BlockSpec((1, d), const), pl.BlockSpec(w_bf.shape, const), pl.BlockSpec((1, 2 * d), const)],
        out_specs=pl.BlockSpec((1, tm, d), tok),
        out_shape=jax.ShapeDtypeStruct((b, s, d), BF16),
        compiler_params=_params(("arbitrary", "arbitrary"), VMEM_LIMIT),
    )(x, shift, scale, g, w_bf, b1)


def _conf_out_kernel(up_ref, uc_ref, un_ref, dw_ref, dwb_ref, lg_ref, lb_ref, w_ref, b2_ref, x_ref, gate_ref,
                     g2_ref, sh2_ref, sc2_ref, rwt_ref, rb_ref, su_ref,
                     h1_ref, xm_ref, pos_ref, wgt_ref, cnt_ref, ubuf, cbuf, sbuf, *, rc, td):
    ti = pl.program_id(1)
    tm, d = uc_ref.shape[1], uc_ref.shape[2]
    hal = CONV_HALO
    prev = up_ref[0].astype(F32)
    nxt = un_ref[0].astype(F32)
    ubuf[0:hal, :] = jnp.where(ti > 0, prev, jnp.zeros_like(prev))
    ubuf[hal:hal + tm, :] = uc_ref[0].astype(F32)
    ubuf[hal + tm:, :] = jnp.where(ti < pl.num_programs(1) - 1, nxt, jnp.zeros_like(nxt))
    base = hal - CONV_WIDTH // 2
    lc = sbuf.shape[2]
    sub = sbuf.shape[0] + 1
    srows = sbuf.shape[1]
    bc = 64

    def shift_chunk(r0, nrows, l0):
        win = ubuf[pl.ds(r0, nrows + sub), l0:l0 + lc]
        for r in range(1, sub):
            sbuf[r - 1, pl.ds(r0, nrows), :] = win[r:r + nrows]

    for l0 in range(0, d, lc):
        def shift_rows(ci, carry, l0=l0):
            shift_chunk(pl.multiple_of(ci * bc, bc), bc, l0)
            return carry

        lax.fori_loop(0, srows // bc, shift_rows, 0)
        if srows % bc:
            shift_chunk(srows // bc * bc, srows % bc, l0)

        def conv_rows(ci, carry, l0=l0):
            r0 = pl.multiple_of(ci * rc, rc)
            acc = jnp.zeros((rc, lc), F32) + dwb_ref[:, l0:l0 + lc]
            for j in range(CONV_WIDTH):
                r = (base + j) % sub
                a = pl.multiple_of(r0 + (base + j - r), sub)
                src = ubuf[pl.ds(a, rc), l0:l0 + lc] if r == 0 else sbuf[r - 1, pl.ds(a, rc), :]
                acc = acc + src * dw_ref[j:j + 1, l0:l0 + lc]
            cbuf[pl.ds(r0, rc), l0:l0 + lc] = acc
            return carry

        lax.fori_loop(0, tm // rc, conv_rows, 0)
    c = cbuf[...]
    mu = jnp.mean(c, axis=-1, keepdims=True)
    cz = c - mu
    var = jnp.mean(cz * cz, axis=-1, keepdims=True)
    y = _silu(cz * lax.rsqrt(var + NORM_EPS) * lg_ref[...] + lb_ref[...]).astype(BF16)
    y = _dot(y, w_ref[...]) + b2_ref[...]
    h1 = x_ref[0] + gate_ref[0] * y
    _route_tail(h1, g2_ref, sh2_ref, sc2_ref, rwt_ref, rb_ref, su_ref,
                h1_ref, xm_ref, pos_ref, wgt_ref, cnt_ref, td=td)


def _conf_out_route(u, dw, dwb, ln_g, ln_b, w_bf, b2, x, gate, tail_args, tm, td):
    b, s, d = x.shape
    hal = CONV_HALO
    per = tm // hal
    nh = s // hal
    tok = lambda bi, ti: (bi, ti, 0)
    vec = lambda bi, ti: (bi, 0, 0)
    const = lambda bi, ti: (0, 0)
    t_in, out_specs, out_shape = _tail_specs(b, s, d, tm)
    return pl.pallas_call(
        functools.partial(_conf_out_kernel, rc=64, td=td),
        grid=(b, s // tm),
        in_specs=[pl.BlockSpec((1, hal, d), lambda bi, ti: (bi, jnp.maximum(ti * per - 1, 0), 0)),
                  pl.BlockSpec((1, tm, d), tok),
                  pl.BlockSpec((1, hal, d), lambda bi, ti: (bi, jnp.minimum((ti + 1) * per, nh - 1), 0)),
                  pl.BlockSpec(dw.shape, const), pl.BlockSpec((1, d), const), pl.BlockSpec((1, d), const),
                  pl.BlockSpec((1, d), const), pl.BlockSpec(w_bf.shape, const), pl.BlockSpec((1, d), const),
                  pl.BlockSpec((1, tm, d), tok), pl.BlockSpec((1, 1, d), vec)] + t_in,
        out_specs=out_specs, out_shape=out_shape,
        scratch_shapes=[pltpu.VMEM((tm + 2 * hal, d), F32), pltpu.VMEM((tm, d), F32),
                        pltpu.VMEM((SUBLANES - 1, tm + (CONV_WIDTH // SUBLANES) * SUBLANES, min(d, 2 * LANES)), F32)],
        compiler_params=_params(("arbitrary", "arbitrary"), VMEM_LIMIT),
    )(u, u, u, dw, dwb, ln_g, ln_b, w_bf, b2, x, gate, *tail_args)


def _copy_rows(src_ref, src0, dst_ref, dst0, length, sem, *, wait, src_fixed=False):
    def piece(off, size):
        s = src0 if src_fixed else pl.multiple_of(src0 + off, ROW_ALIGN)
        return pltpu.make_async_copy(src_ref.at[pl.ds(s, size), :],
                                     dst_ref.at[pl.ds(pl.multiple_of(dst0 + off, ROW_ALIGN), size), :], sem)

    def go(copy):
        if wait:
            copy.wait()
        else:
            copy.start()

    def big(c, carry):
        go(piece(c * RUN_CHUNK, RUN_CHUNK))
        return carry

    lax.fori_loop(0, lax.shift_right_logical(length, RUN_CHUNK.bit_length() - 1), big, 0)
    size = RUN_CHUNK // 2
    while size >= ROW_ALIGN:
        @pl.when((length & size) != 0)
        def _(size=size):
            go(piece(length & ~(2 * size - 1), size))
        size //= 2


def _wait_rows(src_ref, dst_ref, total, max_rows, sem):
    size = pl.next_power_of_2(max_rows)
    while size >= ROW_ALIGN:
        if size <= max_rows:
            @pl.when((total & size) != 0)
            def _(size=size):
                pltpu.make_async_copy(src_ref.at[pl.ds(0, size), :], dst_ref.at[pl.ds(0, size), :], sem).wait()
        size //= 2


def _moe_dispatch_kernel(gap_ref, prev_ref, meta_ref, x_ref, pos_ref, xs_ref, ybuf, zbuf, sems):
    i = pl.program_id(0)
    last = pl.num_programs(0) - 1
    slot = i % 2
    cap, td = ybuf.shape[1], x_ref.shape[0]
    pos = pos_ref[...]
    rid = lax.broadcasted_iota(I32, (cap, td), 0)
    hit = rid == pos[0:1]
    for k in range(1, TOP_K):
        hit = jnp.logical_or(hit, rid == pos[k:k + 1])
    ybuf[slot] = _pack_bf16_pairs(_dot(jnp.where(hit, 1.0, 0.0).astype(BF16), x_ref[...]))

    def wait_runs(m_ref, buf_slot):
        _wait_rows(ybuf.at[buf_slot], xs_ref, m_ref[0, 3, 0], cap, sems.at[buf_slot])

    @pl.when(i > 0)
    def _():
        wait_runs(prev_ref, 1 - slot)

    def run(e, carry):
        _copy_rows(ybuf.at[slot], meta_ref[0, 0, e], xs_ref, meta_ref[0, 2, e], meta_ref[0, 1, e], sems.at[slot],
                   wait=False)
        return carry

    lax.fori_loop(0, N_EXPERTS, run, 0, unroll=4)

    @pl.when(i == last)
    def _():
        wait_runs(meta_ref, slot)
        zbuf[...] = jnp.zeros_like(zbuf)
        for wait in (False, True):
            def fill(e, carry, wait=wait):
                _copy_rows(zbuf, 0, xs_ref, gap_ref[e], gap_ref[N_EXPERTS + e], sems.at[slot], wait=wait, src_fixed=True)
                return carry
            lax.fori_loop(0, N_EXPERTS, fill, 0)


def _moe_dispatch(gap, meta, xm, pos, p, td):
    n, d = xm.shape
    cap = TOP_K * td + N_EXPERTS * ROW_ALIGN
    meta_spec = lambda im: pl.BlockSpec((1, 4, N_EXPERTS), im, memory_space=pltpu.SMEM)
    grid_spec = pltpu.PrefetchScalarGridSpec(
        num_scalar_prefetch=1,
        grid=(n // td,),
        in_specs=[meta_spec(lambda i, g: (jnp.maximum(i - 1, 0), 0, 0)), meta_spec(lambda i, g: (i, 0, 0)),
                  pl.BlockSpec((td, d), lambda i, g: (i, 0)),
                  pl.BlockSpec((TOP_K, td), lambda i, g: (0, i))],
        out_specs=pl.BlockSpec(memory_space=pl.ANY),
        scratch_shapes=[pltpu.VMEM((2, cap, d // 2), U32), pltpu.VMEM((RUN_CHUNK, d // 2), U32),
                        pltpu.SemaphoreType.DMA((2,))],
    )
    return pl.pallas_call(
        _moe_dispatch_kernel,
        grid_spec=grid_spec,
        out_shape=jax.ShapeDtypeStruct((p, d // 2), U32),
        compiler_params=_params(("arbitrary",), VMEM_LIMIT),
    )(gap, meta, meta, xm, pos)


def _moe_expert_kernel(te_ref, nv_ref, xs_ref, w1_ref, b1_ref, w2_ref, b2_ref, ys_ref, w1b, w2b):
    i = pl.program_id(0)
    f = w2_ref.shape[2]
    changed = jnp.logical_or(i == 0, te_ref[i] != te_ref[jnp.maximum(i - 1, 0)])

    @pl.when(jnp.logical_and(changed, i < nv_ref[0]))
    def _():
        w1b[...] = w1_ref[0, 0].astype(BF16)
        w2b[...] = w2_ref[0, 0].astype(BF16)

    @pl.when(i < nv_ref[0])
    def _():
        lo, hi = _unpack_bf16_pairs(xs_ref[...])
        x = jnp.concatenate([lo, hi], axis=-1).astype(BF16)
        y = jnp.zeros((x.shape[0], w2_ref.shape[3]), F32) + b2_ref[0, 0]
        fc = f
        for c in range(f // fc):
            glu = _dot(x, w1b[:, c * fc:(c + 1) * fc]) + b1_ref[0, 0, :, c * fc:(c + 1) * fc]
            lin = _dot(x, w1b[:, f + c * fc:f + (c + 1) * fc]) + b1_ref[0, 0, :, f + c * fc:f + (c + 1) * fc]
            glu = jnp.minimum(glu, SWIGLU_LIMIT)
            lin = jnp.clip(lin, -SWIGLU_LIMIT, SWIGLU_LIMIT)
            act = glu * jax.nn.sigmoid(SWIGLU_ALPHA * glu) * (lin + 1.0)
            y = y + _dot(act.astype(BF16), w2b[c * fc:(c + 1) * fc, :])
        ys_ref[...] = _pack_bf16_pairs(y)


def _moe_experts(tile_expert, n_valid, xs, layer, w1, b1, w2, b2, tme):
    p, d2 = xs.shape
    _, ne, d, f2 = w1.shape
    f = f2 // 2
    row = lambda i, te, nv: (jnp.minimum(i, nv[0] - 1), 0)
    wsel = lambda i, te, nv: (layer, te[jnp.minimum(i, nv[0] - 1)], 0, 0)
    grid_spec = pltpu.PrefetchScalarGridSpec(
        num_scalar_prefetch=2,
        grid=(p // tme,),
        in_specs=[pl.BlockSpec((tme, d2), row),
                  pl.BlockSpec((1, 1, d, f2), wsel), pl.BlockSpec((1, 1, 1, f2), wsel),
                  pl.BlockSpec((1, 1, f, d), wsel), pl.BlockSpec((1, 1, 1, d), wsel)],
        out_specs=pl.BlockSpec((tme, d2), row),
        scratch_shapes=[pltpu.VMEM((d, f2), BF16), pltpu.VMEM((f, d), BF16)],
    )
    depth = w1.shape[0]
    return pl.pallas_call(
        _moe_expert_kernel,
        grid_spec=grid_spec,
        out_shape=jax.ShapeDtypeStruct((p, d2), U32),
        compiler_params=_params(("arbitrary",), VMEM_LIMIT),
    )(tile_expert, n_valid, xs, w1, b1.reshape(depth, ne, 1, f2), w2, b2.reshape(depth, ne, 1, d))


def _moe_combine_kernel(meta_ref, next_ref, ys_ref, pos_ref, wt_ref, h_ref, gate_ref, o_ref, ybuf, sems):
    t = pl.program_id(0) * pl.num_programs(1) + pl.program_id(1)
    n_tiles = pl.num_programs(0) * pl.num_programs(1)
    slot = t % 2

    cap, td = ybuf.shape[1], pos_ref.shape[0]

    def fetch_runs(m_ref, buf_slot):
        def run(e, carry):
            _copy_rows(ys_ref, m_ref[0, 2, e], ybuf.at[buf_slot], m_ref[0, 0, e], m_ref[0, 1, e], sems.at[buf_slot],
                       wait=False)
            return carry
        lax.fori_loop(0, N_EXPERTS, run, 0, unroll=4)

    @pl.when(t == 0)
    def _():
        ybuf[...] = jnp.zeros_like(ybuf)
        fetch_runs(meta_ref, slot)

    @pl.when(t + 1 < n_tiles)
    def _():
        fetch_runs(next_ref, 1 - slot)

    _wait_rows(ys_ref, ybuf.at[slot], meta_ref[0, 3, 0], cap, sems.at[slot])

    pos = pos_ref[...]
    wt = wt_ref[...]
    cid = lax.broadcasted_iota(I32, (td, cap), 1)
    mix = jnp.zeros((td, cap), F32)
    for k in range(TOP_K):
        mix = mix + jnp.where(cid == pos[:, k:k + 1], wt[:, k:k + 1], 0.0)
    lo, hi = _unpack_bf16_pairs(ybuf[slot])
    y = jnp.concatenate([lo, hi], axis=-1).astype(BF16)
    o_ref[0] = h_ref[0] + gate_ref[0] * _dot(mix.astype(BF16), y)


def _moe_combine(meta, ys, pos_t, wt_t, h, gate, td):
    b, s, d = h.shape
    nt = s // td
    cap = TOP_K * td + N_EXPERTS * ROW_ALIGN
    flat = lambda bi, ti: (bi * nt + ti, 0)
    meta_spec = lambda im: pl.BlockSpec((1, 4, N_EXPERTS), im, memory_space=pltpu.SMEM)
    return pl.pallas_call(
        _moe_combine_kernel,
        grid=(b, nt),
        in_specs=[meta_spec(lambda bi, ti: (bi * nt + ti, 0, 0)),
                  meta_spec(lambda bi, ti: (jnp.minimum(bi * nt + ti + 1, b * nt - 1), 0, 0)),
                  pl.BlockSpec(memory_space=pl.ANY),
                  pl.BlockSpec((td, TOP_K), flat), pl.BlockSpec((td, TOP_K), flat),
                  pl.BlockSpec((1, td, d), lambda bi, ti: (bi, ti, 0)),
                  pl.BlockSpec((1, 1, d), lambda bi, ti: (bi, 0, 0))],
        out_specs=pl.BlockSpec((1, td, d), lambda bi, ti: (bi, ti, 0)),
        out_shape=jax.ShapeDtypeStruct((b, s, d), F32),
        scratch_shapes=[pltpu.VMEM((2, cap, d // 2), U32), pltpu.SemaphoreType.DMA((2,))],
        compiler_params=_params(("arbitrary", "arbitrary"), VMEM_LIMIT),
    )(meta, meta, ys, pos_t, wt_t, h, gate)


def _moe(h1, xm, pos, wgt, counts, gate, layer, w1, b1, w2, b2, tme, td):
    n, d = xm.shape
    per = (n // td) // counts.shape[0]
    cnt = jnp.round(counts[:, :, :per]).astype(I32).transpose(0, 2, 1).reshape(n // td, N_EXPERTS)
    run = (cnt + ROW_ALIGN - 1) // ROW_ALIGN * ROW_ALIGN
    local = jnp.cumsum(run, axis=1) - run
    tot = jnp.sum(run, axis=0)
    cap = (tot + tme - 1) // tme * tme
    ends = jnp.cumsum(cap)
    start = ends - cap
    glob = start[None, :] + jnp.cumsum(run, axis=0) - run
    rows_used = jnp.broadcast_to(jnp.sum(run, axis=1, keepdims=True), run.shape)
    meta = jnp.stack([local, run, glob, rows_used], axis=1)
    gap = jnp.concatenate([start + tot, cap - tot]).astype(I32)
    p = (TOP_K * n + N_EXPERTS * ROW_ALIGN * (n // td) + N_EXPERTS * tme) // tme * tme
    tiles = jnp.arange(p // tme, dtype=I32)
    tile_expert = jnp.minimum(jnp.sum(tiles[:, None] >= (ends // tme)[None, :], axis=1), N_EXPERTS - 1).astype(I32)
    n_valid = (ends[-1:] // tme).astype(I32)
    xs = _moe_dispatch(gap, meta, xm, pos, p, td)
    ys = _moe_experts(tile_expert, n_valid, xs, layer, w1, b1, w2, b2, tme)
    return _moe_combine(meta, ys, pos.T, wgt.T, h1, gate, td)


def _rope_tables(s):
    pos = np.arange(s)
    lane = np.arange(LANES)
    dd = lane % NAT_HEAD_DIM
    n = NAT_HEAD_DIM // 4
    inv_freq = ROPE_BASE ** (-(dd % n).astype(np.float64) / n)
    p = np.where((dd // (NAT_HEAD_DIM // 2))[None, :] == 0, (pos // GRID_W)[:, None], (pos % GRID_W)[:, None])
    ang = (p.astype(np.float32) * inv_freq.astype(np.float32)[None, :]).astype(np.float32)
    sign = np.where((dd % (2 * n)) < n, -1.0, 1.0)[None, :]
    return jnp.asarray(np.cos(ang), F32), jnp.asarray(np.sin(ang) * sign, F32)


def _tile(n, want):
    t = min(want, n)
    while n % t:
        t //= 2
    return t


def kernel(x, c, ctx, c_ctx, ada_w, ada_b, norm1_g, norm2_g, ab_w_in, ab_w_out, nat_q_norm, nat_k_norm, nat_rpb, hgrn_lb, hgrn_o_norm, conv_w1, conv_b1, conv_dw, conv_dw_b, conv_ln_g, conv_ln_b, conv_w2, conv_b2, router_w, router_b, moe_w1, moe_b1, moe_w2, moe_b2):
    b, s, d = x.shape
    l = ctx.shape[1]
    rows = s // GRID_W
    assert ada_w.shape[0] == 2 and b < MOD_ROWS and rows >= NAT_KH and rows % NAT_KH == 0
    assert s % HGRN_CHUNK == 0 and l % HGRN_CHUNK == 0 and d % (2 * LANES) == 0
    tm = _tile(s, 512)
    td = tm
    tme = 512 if TOP_K * b * s >= 512 * N_EXPERTS else 128

    cc = jnp.zeros((MOD_ROWS, d), F32).at[:b].set(c).at[b].set(c_ctx)
    mod = _modulation(cc, ada_w, ada_b)

    def mod_vec(layer, i):
        return mod[layer, :b, i * d:(i + 1) * d].reshape(b, 1, d)

    def ctx_vec(i):
        return jnp.broadcast_to(mod[0, b, i * d:(i + 1) * d].reshape(1, 1, d), (b, 1, d))

    tok = np.arange(tm)
    strict_upper = jnp.asarray((tok[:, None] < tok[None, :]) & (tok[:, None] // td == tok[None, :] // td), BF16)

    def tail_args(layer):
        return (norm2_g[layer].reshape(1, d), mod_vec(layer, 3), mod_vec(layer, 4),
                router_w[layer].T, router_b[layer].reshape(N_EXPERTS, 1), strict_upper)

    lb_all = jnp.cumsum(jax.nn.softmax(hgrn_lb.astype(F32), axis=1), axis=1)[:, 0]
    w_in = ab_w_in[0].astype(BF16)
    scale = NAT_HEAD_DIM ** -0.5
    qg = jnp.tile(nat_q_norm[0] * scale, NAT_HEADS).reshape(1, NAT_WIDTH)
    kg = jnp.tile(nat_k_norm[0], NAT_HEADS).reshape(1, NAT_WIDTH)
    head_of = np.arange(NAT_WIDTH) // NAT_HEAD_DIM
    bd = jnp.asarray(head_of[:, None] == head_of[None, :], BF16)
    lbf, lbb = lb_all[0].reshape(1, HGRN_WIDTH), lb_all[1].reshape(1, HGRN_WIDTH)
    cos, sin = _rope_tables(s)
    g1 = norm1_g[0].reshape(1, d)
    lat = _inproj(x, mod_vec(0, 0), mod_vec(0, 1), g1, w_in, cos, sin, qg, kg, bd, lbf, lbb, tm)
    qa, qb, k, v, hq, kf, lff, kb, lfb, hi, sg = lat
    tl = _tile(l, 256)
    cxt = _inproj(ctx, ctx_vec(0), ctx_vec(1), g1, w_in, jnp.ones((l, LANES), F32), jnp.zeros((l, LANES), F32),
                  qg, kg, bd, lbf, lbb, tl)
    _, _, kx, vx, _, kfx, lffx, kbx, lfbx, hix, _ = cxt

    nat = _nat_attention(qa, qb, k, v, kx, vx, _nat_bias_table(nat_rpb[0]), rows)

    tri_f = jnp.asarray(np.tril(np.ones((HGRN_CHUNK, HGRN_CHUNK), np.float32)))
    s0 = jnp.zeros((b, HGRN_HEADS, HGRN_DIM, HGRN_DIM), F32)
    _, sf = _hgrn_scan(None, kfx, lffx, hix, s0, tri_f, False, False)
    of, _ = _hgrn_scan(hq, kf, lff, hi, sf, tri_f, False, True)
    _, sb = _hgrn_scan(None, kbx, lfbx, hix, s0, tri_f.T, True, False)
    ob, _ = _hgrn_scan(hq, kb, lfb, hi, sb, tri_f.T, True, True)

    h1, xm, pos, wgt, counts = _outproj_route(
        nat, of, ob, sg, x, mod_vec(0, 2), hgrn_o_norm[0].reshape(1, HGRN_DIM), ab_w_out[0].astype(BF16),
        tail_args(0), tm, td)
    h = _moe(h1, xm, pos, wgt, counts, mod_vec(0, 5), 0, moe_w1, moe_b1, moe_w2, moe_b2, tme, td)

    u = _conf_in(h, mod_vec(1, 0), mod_vec(1, 1), norm1_g[1].reshape(1, d), conv_w1[0].astype(BF16),
                 conv_b1[0].reshape(1, 2 * d), tm)
    h1, xm, pos, wgt, counts = _conf_out_route(
        u, conv_dw[0], conv_dw_b[0].reshape(1, d), conv_ln_g[0].reshape(1, d), conv_ln_b[0].reshape(1, d),
        conv_w2[0].astype(BF16), conv_b2[0].reshape(1, d), h, mod_vec(1, 2), tail_args(1), tm, td)
    return _moe(h1, xm, pos, wgt, counts, mod_vec(1, 5), 1, moe_w1, moe_b1, moe_w2, moe_b2, tme, td)
```

```python
import functools

import numpy as np
import jax
import jax.numpy as jnp
from jax import lax
from jax.experimental import pallas as pl
from jax.experimental.pallas import tpu as pltpu

F32 = jnp.float32
BF16 = jnp.bfloat16
U32 = jnp.uint32
I32 = jnp.int32
HIGHEST = lax.Precision.HIGHEST

GRID_W = 64
NAT_HEADS = 8
NAT_HEAD_DIM = 64
NAT_WIDTH = NAT_HEADS * NAT_HEAD_DIM
NAT_KH = 8
NAT_KW = 16
HGRN_HEADS = 4
HGRN_DIM = 128
HGRN_WIDTH = HGRN_HEADS * HGRN_DIM
HGRN_CHUNK = 64
HGRN_SUB = 16
CONV_WIDTH = 31
CONV_HALO = 16
N_EXPERTS = 32
TOP_K = 4
SWIGLU_LIMIT = 7.0
SWIGLU_ALPHA = 1.702
ROPE_BASE = 10000.0
NORM_EPS = 1e-6
MASK_VALUE = -1e30
EXP_CLAMP = 80.0

SUBLANES = 8
ROW_ALIGN = SUBLANES
RUN_CHUNK = 64
LANES = 128
MOD_ROWS = 16
VMEM_LIMIT = 56 * 1024 * 1024
NT = (((1,), (1,)), ((), ()))


def _params(sem, vmem=None):
    return pltpu.CompilerParams(dimension_semantics=sem, vmem_limit_bytes=vmem)


def _dot(a, b):
    return jnp.dot(a, b, preferred_element_type=F32)


def _dot_nt(a, b):
    return lax.dot_general(a, b, NT, preferred_element_type=F32)


def _silu(x):
    return x * jax.nn.sigmoid(x)


def _rms(x, g):
    return x * lax.rsqrt(jnp.mean(x * x, axis=-1, keepdims=True) + NORM_EPS) * g


def _mod_kernel(cc_ref, w_ref, b_ref, o_ref):
    cc = cc_ref[...]
    o_ref[0] = jnp.dot(_silu(cc), w_ref[0], precision=HIGHEST, preferred_element_type=F32) + b_ref[0]


def _modulation(cc, ada_w, ada_b):
    depth, d, n = ada_w.shape
    tn = n // 4
    return pl.pallas_call(
        _mod_kernel,
        grid=(depth, n // tn),
        in_specs=[pl.BlockSpec((MOD_ROWS, d), lambda l, j: (0, 0)),
                  pl.BlockSpec((1, d, tn), lambda l, j: (l, 0, j)),
                  pl.BlockSpec((1, 1, tn), lambda l, j: (l, 0, j))],
        out_specs=pl.BlockSpec((1, MOD_ROWS, tn), lambda l, j: (l, 0, j)),
        out_shape=jax.ShapeDtypeStruct((depth, MOD_ROWS, n), F32),
        compiler_params=_params(("arbitrary", "arbitrary")),
    )(cc, ada_w, ada_b.reshape(depth, 1, n))


def _inproj_kernel(x_ref, sh_ref, sc_ref, g_ref, w_ref, cos_ref, sin_ref, qg_ref, kg_ref, bd_ref, lbf_ref, lbb_ref,
                   qa_ref, qb_ref, k_ref, v_ref, hq_ref, kf_ref, lff_ref, kb_ref, lfb_ref, hi_ref, sg_ref):
    x = x_ref[0]
    xm = (_rms(x, g_ref[...]) * (1.0 + sc_ref[0]) + sh_ref[0]).astype(BF16)
    wd = NAT_WIDTH

    def proj(i):
        return _dot(xm, w_ref[:, i * wd:(i + 1) * wd])

    def head_norm(y, g):
        sq = y * y
        hi = sq.astype(BF16)
        lo = (sq - hi.astype(F32)).astype(BF16)
        ss = _dot(hi, bd_ref[...]) + _dot(lo, bd_ref[...])
        return y * lax.rsqrt(ss * (1.0 / NAT_HEAD_DIM) + NORM_EPS) * g

    cos = cos_ref[...]
    sin = sin_ref[...]
    lane = lax.broadcasted_iota(I32, cos.shape, 1)
    first = (lane % 32) < 16

    def rope(y):
        outs = []
        for gi in range(wd // LANES):
            yg = y[:, gi * LANES:(gi + 1) * LANES]
            partner = jnp.where(first, pltpu.roll(yg, LANES - 16, 1), pltpu.roll(yg, 16, 1))
            outs.append(yg * cos + partner * sin)
        return jnp.concatenate(outs, axis=-1)

    qf = head_norm(proj(0), qg_ref[...])
    qa_ref[0] = rope(qf).astype(BF16)
    qb_ref[0] = qf.astype(BF16)
    k_ref[0] = rope(head_norm(proj(1), kg_ref[...])).astype(BF16)
    v_ref[0] = proj(2).astype(BF16)
    hq_ref[0] = (_silu(proj(3)) * (HGRN_DIM ** -0.5)).astype(BF16)
    for i, lb_ref, kk_ref, lf_ref in ((4, lbf_ref, kf_ref, lff_ref), (5, lbb_ref, kb_ref, lfb_ref)):
        lb = lb_ref[...]
        f = lb + (1.0 - lb) * jax.nn.sigmoid(proj(i))
        kk_ref[0] = (1.0 - f).astype(BF16)
        lf_ref[0] = jnp.log(f)
    hi_ref[0] = proj(6).astype(BF16)
    sg_ref[0] = _silu(proj(7)).astype(BF16)


def _inproj(x, shift, scale, g, w_bf, cos, sin, qg, kg, bd, lbf, lbb, tm):
    b, t, d = x.shape
    wd = NAT_WIDTH
    tok = lambda bi, ti: (bi, ti, 0)
    vec = lambda bi, ti: (bi, 0, 0)
    const = lambda bi, ti: (0, 0)
    out_dtypes = (BF16, BF16, BF16, BF16, BF16, BF16, F32, BF16, F32, BF16, BF16)
    return pl.pallas_call(
        _inproj_kernel,
        grid=(b, t // tm),
        in_specs=[pl.BlockSpec((1, tm, d), tok), pl.BlockSpec((1, 1, d), vec), pl.BlockSpec((1, 1, d), vec),
                  pl.BlockSpec((1, d), const), pl.BlockSpec(w_bf.shape, const),
                  pl.BlockSpec((tm, LANES), lambda bi, ti: (ti, 0)), pl.BlockSpec((tm, LANES), lambda bi, ti: (ti, 0)),
                  pl.BlockSpec((1, wd), const), pl.BlockSpec((1, wd), const), pl.BlockSpec((wd, wd), const),
                  pl.BlockSpec((1, wd), const), pl.BlockSpec((1, wd), const)],
        out_specs=[pl.BlockSpec((1, tm, wd), tok)] * len(out_dtypes),
        out_shape=[jax.ShapeDtypeStruct((b, t, wd), dt) for dt in out_dtypes],
        compiler_params=_params(("arbitrary", "arbitrary"), VMEM_LIMIT),
    )(x, shift, scale, g, w_bf, cos, sin, qg, kg, bd, lbf, lbb)


def _nat_kernel(qa_ref, qb_ref, kp_ref, kc_ref, kn_ref, vp_ref, vc_ref, vn_ref, kx_ref, vx_ref, bias_ref,
                o_ref, kbuf, vbuf, s1_scr, s2_scr, p1_scr, p2_scr, *, rows, rb):
    i = pl.program_id(1)
    blk = rb * GRID_W
    for s, (kr, vr) in enumerate(((kp_ref, vp_ref), (kc_ref, vc_ref), (kn_ref, vn_ref))):
        kbuf[s * blk:(s + 1) * blk, :] = kr[0]
        vbuf[s * blk:(s + 1) * blk, :] = vr[0]
    lane = lax.broadcasted_iota(I32, (GRID_W, LANES), 1)
    low = lane < NAT_HEAD_DIM
    kwin = NAT_KH * GRID_W
    pair = 2 * GRID_W
    sm_rows = 16

    def stack_heads(q2):
        q2 = q2.astype(F32)
        return jnp.concatenate([jnp.where(low, q2, 0.0), jnp.where(low, 0.0, q2)], axis=0).astype(BF16)

    def window(j):
        r = i * rb + j
        rs = jnp.clip(r - NAT_KH // 2, 0, rows - NAT_KH)
        start = pl.multiple_of((rs - i * rb + rb) * GRID_W, GRID_W)
        return start, rs - r + NAT_KH - 1, pl.multiple_of(j * GRID_W, GRID_W)

    def scores(j, u):
        start, cls, q0 = window(j)
        for p in range(NAT_HEADS // 2):
            ls = slice(p * LANES, (p + 1) * LANES)
            bias = jnp.concatenate([bias_ref[cls, 2 * p], bias_ref[cls, 2 * p + 1]], axis=0)
            s1_scr[u, p * pair:(p + 1) * pair, :] = _dot_nt(stack_heads(qa_ref[0, pl.ds(q0, GRID_W), ls]),
                                                            kbuf[pl.ds(start, kwin), ls]) + bias
            s2_scr[u, p * pair:(p + 1) * pair, :] = _dot_nt(stack_heads(qb_ref[0, pl.ds(q0, GRID_W), ls]),
                                                            kx_ref[0, :, ls])

    def softmax(u):
        for c in range(NAT_HEADS * GRID_W // sm_rows):
            rsl = slice(c * sm_rows, (c + 1) * sm_rows)
            a = s1_scr[u, rsl, :]
            bb = s2_scr[u, rsl, :]
            m = jnp.maximum(jnp.max(a, axis=-1, keepdims=True), jnp.max(bb, axis=-1, keepdims=True))
            ea = jnp.exp(a - m)
            eb = jnp.exp(bb - m)
            inv = 1.0 / (jnp.sum(ea, axis=-1, keepdims=True) + jnp.sum(eb, axis=-1, keepdims=True))
            p1_scr[u, rsl, :] = (ea * inv).astype(BF16)
            p2_scr[u, rsl, :] = (eb * inv).astype(BF16)

    def values(j, u):
        start, _, q0 = window(j)
        for p in range(NAT_HEADS // 2):
            ls = slice(p * LANES, (p + 1) * LANES)
            o = (_dot(p1_scr[u, p * pair:(p + 1) * pair, :], vbuf[pl.ds(start, kwin), ls])
                 + _dot(p2_scr[u, p * pair:(p + 1) * pair, :], vx_ref[0, :, ls]))
            o_ref[0, pl.ds(q0, GRID_W), ls] = jnp.where(low, o[:GRID_W], o[GRID_W:]).astype(BF16)

    def row_pair(jp, carry):
        for u in range(2):
            scores(2 * jp + u, u)
        for u in range(2):
            softmax(u)
        for u in range(2):
            values(2 * jp + u, u)
        return carry

    lax.fori_loop(0, rb // 2, row_pair, 0)


def _nat_attention(qa, qb, k, v, kx, vx, bias, rows):
    b, s, wd = qa.shape
    lx = kx.shape[1]
    rb = NAT_KH
    blk = rb * GRID_W
    nb = rows // rb
    cur = lambda bi, i: (bi, i, 0)
    prv = lambda bi, i: (bi, jnp.maximum(i - 1, 0), 0)
    nxt = lambda bi, i: (bi, jnp.minimum(i + 1, nb - 1), 0)
    ctx = lambda bi, i: (bi, 0, 0)
    tile = lambda im: pl.BlockSpec((1, blk, wd), im)
    return pl.pallas_call(
        functools.partial(_nat_kernel, rows=rows, rb=rb),
        grid=(b, nb),
        in_specs=[tile(cur), tile(cur), tile(prv), tile(cur), tile(nxt), tile(prv), tile(cur), tile(nxt),
                  pl.BlockSpec((1,) + kx.shape[1:], ctx), pl.BlockSpec((1,) + vx.shape[1:], ctx),
                  pl.BlockSpec(bias.shape, lambda bi, i: (0, 0, 0, 0))],
        out_specs=tile(cur),
        out_shape=jax.ShapeDtypeStruct((b, s, wd), BF16),
        scratch_shapes=[pltpu.VMEM((3 * blk, wd), BF16), pltpu.VMEM((3 * blk, wd), BF16),
                        pltpu.VMEM((2, NAT_HEADS * GRID_W, NAT_KH * GRID_W), F32),
                        pltpu.VMEM((2, NAT_HEADS * GRID_W, lx), F32),
                        pltpu.VMEM((2, NAT_HEADS * GRID_W, NAT_KH * GRID_W), BF16),
                        pltpu.VMEM((2, NAT_HEADS * GRID_W, lx), BF16)],
        compiler_params=_params(("arbitrary", "arbitrary"), VMEM_LIMIT),
    )(qa, qb, k, k, k, v, v, v, kx, vx, bias)


def _nat_bias_table(rpb):
    qc = np.arange(GRID_W)
    wc = np.clip(qc - NAT_KW // 2, 0, GRID_W - NAT_KW)
    kc = np.arange(GRID_W)
    valid = (kc[None, :] >= wc[:, None]) & (kc[None, :] < wc[:, None] + NAT_KW)
    cidx = np.clip(kc[None, :] - qc[:, None] + NAT_KW - 1, 0, 2 * NAT_KW - 2)
    ridx = np.arange(NAT_KH)[:, None] + np.arange(NAT_KH)[None, :]
    tab = rpb.astype(F32)[:, ridx][:, :, :, cidx]
    tab = jnp.where(valid[None, None, None], tab, MASK_VALUE)
    tab = tab.transpose(1, 0, 3, 2, 4)
    return tab.reshape(NAT_KH, NAT_HEADS, GRID_W, NAT_KH * GRID_W)


def _hgrn_kernel(*refs, rev, with_out, cb):
    if with_out:
        q_ref, kk_ref, lf_ref, v_ref, s0_ref, tri_ref, o_ref, sf_ref, st = refs
    else:
        kk_ref, lf_ref, v_ref, s0_ref, tri_ref, sf_ref, st = refs
    i = pl.program_id(1)
    c_sz = HGRN_CHUNK

    @pl.when(i == 0)
    def _():
        st[...] = s0_ref[0]

    tri = tri_ref[...]
    tri_b = tri.astype(BF16)
    keep = tri > 0.5

    def chunk(ci, carry):
        c = (cb - 1 - ci) if rev else ci
        r0 = pl.multiple_of(c * c_sz, c_sz)
        lf = lf_ref[0, pl.ds(r0, c_sz), :]
        lf_hi = lf.astype(BF16)
        rest = lf - lf_hi.astype(F32)
        lf_mid = rest.astype(BF16)
        lf_lo = (rest - lf_mid.astype(F32)).astype(BF16)
        bsum = _dot(tri_b, lf_hi) + (_dot(tri_b, lf_mid) + _dot(tri_b, lf_lo))
        for h in range(HGRN_HEADS):
            ls = slice(h * HGRN_DIM, (h + 1) * HGRN_DIM)
            bh = bsum[:, ls]
            tot = bh[0:1] if rev else bh[c_sz - 1:c_sz]
            kh = kk_ref[0, pl.ds(r0, c_sz), ls].astype(F32)
            vh = v_ref[0, pl.ds(r0, c_sz), ls]
            state = st[h]
            if with_out:
                qh = q_ref[0, pl.ds(r0, c_sz), ls].astype(F32)
                blocks = []
                for sb in range(c_sz // HGRN_SUB):
                    lo = sb * HGRN_SUB
                    ref_row = lo + HGRN_SUB - 1 if rev else lo
                    cref = bh[ref_row:ref_row + 1]
                    f1 = jnp.exp(bh[lo:lo + HGRN_SUB] - cref)
                    f2 = jnp.exp(jnp.minimum(cref - bh, EXP_CLAMP))
                    blocks.append(_dot_nt((qh[lo:lo + HGRN_SUB] * f1).astype(BF16), (kh * f2).astype(BF16)))
                scores = jnp.where(keep, jnp.concatenate(blocks, axis=0), 0.0)
                o = _dot(scores.astype(BF16), vh) + _dot_nt((qh * jnp.exp(bh)).astype(BF16), state.astype(BF16))
                o_ref[0, pl.ds(r0, c_sz), ls] = o.astype(BF16)
            kd = (kh * jnp.exp(tot - bh)).astype(BF16)
            vt = vh.astype(F32).T.astype(BF16)
            st[h] = state * jnp.exp(tot) + _dot(vt, kd)
        return carry

    lax.fori_loop(0, cb, chunk, 0, unroll=4)

    @pl.when(i == pl.num_programs(1) - 1)
    def _():
        sf_ref[0] = st[...]


def _hgrn_scan(q, kk, lf, v, s0, tri, rev, with_out):
    b, t, wd = kk.shape
    nchunk = t // HGRN_CHUNK
    cb = min(8, nchunk)
    nblk = nchunk // cb
    tb = cb * HGRN_CHUNK
    tok = (lambda bi, i: (bi, nblk - 1 - i, 0)) if rev else (lambda bi, i: (bi, i, 0))
    st_map = lambda bi, i: (bi, 0, 0, 0)
    st_shape = (b, HGRN_HEADS, HGRN_DIM, HGRN_DIM)
    tile = pl.BlockSpec((1, tb, wd), tok)
    st_spec = pl.BlockSpec((1,) + st_shape[1:], st_map)
    ins = ([q] if with_out else []) + [kk, lf, v, s0, tri]
    in_specs = [tile] * (len(ins) - 2) + [st_spec, pl.BlockSpec(tri.shape, lambda bi, i: (0, 0))]
    out_specs = ([tile] if with_out else []) + [st_spec]
    out_shape = ([jax.ShapeDtypeStruct((b, t, wd), BF16)] if with_out else []) + [jax.ShapeDtypeStruct(st_shape, F32)]
    res = pl.pallas_call(
        functools.partial(_hgrn_kernel, rev=rev, with_out=with_out, cb=cb),
        grid=(b, nblk),
        in_specs=in_specs, out_specs=out_specs, out_shape=out_shape,
        scratch_shapes=[pltpu.VMEM(st_shape[1:], F32)],
        compiler_params=_params(("arbitrary", "arbitrary"), VMEM_LIMIT),
    )(*ins)
    return res if with_out else (None, res[0])


def _pack_bf16_pairs(x):
    half = x.shape[-1] // 2
    lo = pltpu.bitcast(x[:, :half].astype(BF16).astype(F32), U32) >> 16
    hi = pltpu.bitcast(x[:, half:].astype(BF16).astype(F32), U32) & jnp.uint32(0xFFFF0000)
    return hi | lo


def _unpack_bf16_pairs(w):
    lo = pltpu.bitcast(w << 16, F32)
    hi = pltpu.bitcast(w & jnp.uint32(0xFFFF0000), F32)
    return lo, hi


def _route_tail(h1, g2_ref, sh2_ref, sc2_ref, rwt_ref, rb_ref, su_ref,
                h1_ref, xm_ref, pos_ref, wgt_ref, cnt_ref, *, td):
    h1_ref[0] = h1
    xm2 = _rms(h1, g2_ref[...]) * (1.0 + sc2_ref[0]) + sh2_ref[0]
    xm_hi = xm2.astype(BF16)
    xm_ref[...] = xm_hi
    xm_lo = (xm2 - xm_hi.astype(F32)).astype(BF16)
    ne = rb_ref.shape[0]
    part = _dot_nt(rwt_ref[...], xm_hi)
    logits = part[:ne] + part[ne:] + _dot_nt(rwt_ref[:ne, :], xm_lo) + rb_ref[...]
    tm = logits.shape[1]
    eid = lax.broadcasted_iota(I32, (ne, tm), 0).astype(F32)
    vals, hots = [], []
    for k in range(TOP_K):
        m = jnp.max(logits, axis=0, keepdims=True)
        sel = jnp.min(jnp.where(logits == m, eid, float(ne)), axis=0, keepdims=True)
        hot = eid == sel
        logits = jnp.where(hot, -jnp.inf, logits)
        vals.append(m)
        hots.append(hot)
    es = [jnp.exp(vv - vals[0]) for vv in vals]
    den = es[0] + es[1] + es[2] + es[3]
    onehot = jnp.zeros((ne, tm), F32)
    for k in range(TOP_K):
        wgt_ref[k:k + 1, :] = es[k] / den
        onehot = onehot + hots[k].astype(F32)
    prefix = _dot(onehot.astype(BF16), su_ref[...])
    strict_lower = (lax.broadcasted_iota(I32, (ne, ne), 0) > lax.broadcasted_iota(I32, (ne, ne), 1)).astype(BF16)
    lane = lax.broadcasted_iota(I32, (ne, LANES), 1)
    counts = jnp.zeros((ne, LANES), F32)
    starts = []
    for j in range(tm // td):
        cj = jnp.sum(onehot[:, j * td:(j + 1) * td], axis=1, keepdims=True)
        padded = jnp.floor((cj + (ROW_ALIGN - 1)) * (1.0 / ROW_ALIGN)) * ROW_ALIGN
        group_start = _dot(strict_lower, jnp.broadcast_to(padded, (ne, LANES)).astype(BF16))
        starts.append(jnp.broadcast_to(group_start[:, 0:1], (ne, td)))
        counts = counts + jnp.where(lane == j, cj, 0.0)
    row = prefix + jnp.concatenate(starts, axis=1)
    for k in range(TOP_K):
        pos_ref[k:k + 1, :] = jnp.sum(jnp.where(hots[k], row, 0.0), axis=0, keepdims=True).astype(I32)
    cnt_ref[0] = counts


def _tail_specs(b, s, d, tm):
    nt = s // tm
    n = b * s
    vec = lambda bi, ti: (bi, 0, 0)
    const = lambda bi, ti: (0, 0)
    in_specs = [pl.BlockSpec((1, d), const), pl.BlockSpec((1, 1, d), vec), pl.BlockSpec((1, 1, d), vec),
                pl.BlockSpec((2 * N_EXPERTS, d), const), pl.BlockSpec((N_EXPERTS, 1), const),
                pl.BlockSpec((tm, tm), const)]
    flat = lambda bi, ti: (0, bi * nt + ti)
    out_specs = [pl.BlockSpec((1, tm, d), lambda bi, ti: (bi, ti, 0)),
                 pl.BlockSpec((tm, d), lambda bi, ti: (bi * nt + ti, 0)),
                 pl.BlockSpec((TOP_K, tm), flat), pl.BlockSpec((TOP_K, tm), flat),
                 pl.BlockSpec((1, N_EXPERTS, LANES), lambda bi, ti: (bi * nt + ti, 0, 0))]
    out_shape = [jax.ShapeDtypeStruct((b, s, d), F32), jax.ShapeDtypeStruct((n, d), BF16),
                 jax.ShapeDtypeStruct((TOP_K, n), I32), jax.ShapeDtypeStruct((TOP_K, n), F32),
                 jax.ShapeDtypeStruct((b * nt, N_EXPERTS, LANES), F32)]
    return in_specs, out_specs, out_shape


def _outproj_kernel(nat_ref, of_ref, ob_ref, sg_ref, x_ref, gate_ref, on_ref, w_ref,
                    g2_ref, sh2_ref, sc2_ref, rwt_ref, rb_ref, su_ref,
                    h1_ref, xm_ref, pos_ref, wgt_ref, cnt_ref, *, td):
    o = of_ref[0].astype(F32) + ob_ref[0].astype(F32)
    gated = []
    for h in range(HGRN_HEADS):
        ls = slice(h * HGRN_DIM, (h + 1) * HGRN_DIM)
        gated.append(_rms(o[:, ls], on_ref[...]) * sg_ref[0, :, ls].astype(F32))
    gated = jnp.concatenate(gated, axis=-1).astype(BF16)
    y = _dot(nat_ref[0], w_ref[:NAT_WIDTH, :]) + _dot(gated, w_ref[NAT_WIDTH:, :])
    h1 = x_ref[0] + gate_ref[0] * y
    _route_tail(h1, g2_ref, sh2_ref, sc2_ref, rwt_ref, rb_ref, su_ref,
                h1_ref, xm_ref, pos_ref, wgt_ref, cnt_ref, td=td)


def _outproj_route(nat, of, ob, sg, x, gate, on_g, w_bf, tail_args, tm, td):
    b, s, d = x.shape
    tok = lambda bi, ti: (bi, ti, 0)
    vec = lambda bi, ti: (bi, 0, 0)
    const = lambda bi, ti: (0, 0)
    t_in, out_specs, out_shape = _tail_specs(b, s, d, tm)
    wide = pl.BlockSpec((1, tm, NAT_WIDTH), tok)
    return pl.pallas_call(
        functools.partial(_outproj_kernel, td=td),
        grid=(b, s // tm),
        in_specs=[wide, wide, wide, wide, pl.BlockSpec((1, tm, d), tok), pl.BlockSpec((1, 1, d), vec),
                  pl.BlockSpec((1, HGRN_DIM), const), pl.BlockSpec(w_bf.shape, const)] + t_in,
        out_specs=out_specs, out_shape=out_shape,
        compiler_params=_params(("arbitrary", "arbitrary"), VMEM_LIMIT),
    )(nat, of, ob, sg, x, gate, on_g, w_bf, *tail_args)


def _conf_in_kernel(x_ref, sh_ref, sc_ref, g_ref, w_ref, b_ref, u_ref):
    d = x_ref.shape[-1]
    xm = (_rms(x_ref[0], g_ref[...]) * (1.0 + sc_ref[0]) + sh_ref[0]).astype(BF16)
    a = _dot(xm, w_ref[:, :d]) + b_ref[:, :d]
    gate = _dot(xm, w_ref[:, d:]) + b_ref[:, d:]
    u_ref[0] = (a * jax.nn.sigmoid(gate)).astype(BF16)


def _conf_in(x, shift, scale, g, w_bf, b1, tm):
    b, s, d = x.shape
    tok = lambda bi, ti: (bi, ti, 0)
    vec = lambda bi, ti: (bi, 0, 0)
    const = lambda bi, ti: (0, 0)
    return pl.pallas_call(
        _conf_in_kernel,
        grid=(b, s // tm),
        in_specs=[pl.BlockSpec((1, tm, d), tok), pl.BlockSpec((1, 1, d), vec), pl.BlockSpec((1, 1, d), vec),
                  pl.BlockSpec((1, d), const), pl.BlockSpec(w_bf.shape, const), pl.BlockSpec((1, 2 * d), const)],
        out_specs=pl.BlockSpec((1, tm, d), tok),
        out_shape=jax.ShapeDtypeStruct((b, s, d), BF16),
        compiler_params=_params(("arbitrary", "arbitrary"), VMEM_LIMIT),
    )(x, shift, scale, g, w_bf, b1)


def _conf_out_kernel(up_ref, uc_ref, un_ref, dw_ref, dwb_ref, lg_ref, lb_ref, w_ref, b2_ref, x_ref, gate_ref,
                     g2_ref, sh2_ref, sc2_ref, rwt_ref, rb_ref, su_ref,
                     h1_ref, xm_ref, pos_ref, wgt_ref, cnt_ref, ubuf, cbuf, sbuf, *, rc, td):
    ti = pl.program_id(1)
    tm, d = uc_ref.shape[1], uc_ref.shape[2]
    hal = CONV_HALO
    prev = up_ref[0].astype(F32)
    nxt = un_ref[0].astype(F32)
    ubuf[0:hal, :] = jnp.where(ti > 0, prev, jnp.zeros_like(prev))
    ubuf[hal:hal + tm, :] = uc_ref[0].astype(F32)
    ubuf[hal + tm:, :] = jnp.where(ti < pl.num_programs(1) - 1, nxt, jnp.zeros_like(nxt))
    base = hal - CONV_WIDTH // 2
    lc = sbuf.shape[2]
    sub = sbuf.shape[0] + 1
    srows = sbuf.shape[1]
    bc = 64

    def shift_chunk(r0, nrows, l0):
        win = ubuf[pl.ds(r0, nrows + sub), l0:l0 + lc]
        for r in range(1, sub):
            sbuf[r - 1, pl.ds(r0, nrows), :] = win[r:r + nrows]

    for l0 in range(0, d, lc):
        def shift_rows(ci, carry, l0=l0):
            shift_chunk(pl.multiple_of(ci * bc, bc), bc, l0)
            return carry

        lax.fori_loop(0, srows // bc, shift_rows, 0)
        if srows % bc:
            shift_chunk(srows // bc * bc, srows % bc, l0)

        def conv_rows(ci, carry, l0=l0):
            r0 = pl.multiple_of(ci * rc, rc)
            acc = jnp.zeros((rc, lc), F32) + dwb_ref[:, l0:l0 + lc]
            for j in range(CONV_WIDTH):
                r = (base + j) % sub
                a = pl.multiple_of(r0 + (base + j - r), sub)
                src = ubuf[pl.ds(a, rc), l0:l0 + lc] if r == 0 else sbuf[r - 1, pl.ds(a, rc), :]
                acc = acc + src * dw_ref[j:j + 1, l0:l0 + lc]
            cbuf[pl.ds(r0, rc), l0:l0 + lc] = acc
            return carry

        lax.fori_loop(0, tm // rc, conv_rows, 0)
    c = cbuf[...]
    mu = jnp.mean(c, axis=-1, keepdims=True)
    cz = c - mu
    var = jnp.mean(cz * cz, axis=-1, keepdims=True)
    y = _silu(cz * lax.rsqrt(var + NORM_EPS) * lg_ref[...] + lb_ref[...]).astype(BF16)
    y = _dot(y, w_ref[...]) + b2_ref[...]
    h1 = x_ref[0] + gate_ref[0] * y
    _route_tail(h1, g2_ref, sh2_ref, sc2_ref, rwt_ref, rb_ref, su_ref,
                h1_ref, xm_ref, pos_ref, wgt_ref, cnt_ref, td=td)


def _conf_out_route(u, dw, dwb, ln_g, ln_b, w_bf, b2, x, gate, tail_args, tm, td):
    b, s, d = x.shape
    hal = CONV_HALO
    per = tm // hal
    nh = s // hal
    tok = lambda bi, ti: (bi, ti, 0)
    vec = lambda bi, ti: (bi, 0, 0)
    const = lambda bi, ti: (0, 0)
    t_in, out_specs, out_shape = _tail_specs(b, s, d, tm)
    return pl.pallas_call(
        functools.partial(_conf_out_kernel, rc=64, td=td),
        grid=(b, s // tm),
        in_specs=[pl.BlockSpec((1, hal, d), lambda bi, ti: (bi, jnp.maximum(ti * per - 1, 0), 0)),
                  pl.BlockSpec((1, tm, d), tok),
                  pl.BlockSpec((1, hal, d), lambda bi, ti: (bi, jnp.minimum((ti + 1) * per, nh - 1), 0)),
                  pl.BlockSpec(dw.shape, const), pl.BlockSpec((1, d), const), pl.BlockSpec((1, d), const),
                  pl.BlockSpec((1, d), const), pl.BlockSpec(w_bf.shape, const), pl.BlockSpec((1, d), const),
                  pl.BlockSpec((1, tm, d), tok), pl.BlockSpec((1, 1, d), vec)] + t_in,
        out_specs=out_specs, out_shape=out_shape,
        scratch_shapes=[pltpu.VMEM((tm + 2 * hal, d), F32), pltpu.VMEM((tm, d), F32),
                        pltpu.VMEM((SUBLANES - 1, tm + (CONV_WIDTH // SUBLANES) * SUBLANES, min(d, 2 * LANES)), F32)],
        compiler_params=_params(("arbitrary", "arbitrary"), VMEM_LIMIT),
    )(u, u, u, dw, dwb, ln_g, ln_b, w_bf, b2, x, gate, *tail_args)


def _copy_rows(src_ref, src0, dst_ref, dst0, length, sem, *, wait, src_fixed=False):
    def piece(off, size):
        s = src0 if src_fixed else pl.multiple_of(src0 + off, ROW_ALIGN)
        return pltpu.make_async_copy(src_ref.at[pl.ds(s, size), :],
                                     dst_ref.at[pl.ds(pl.multiple_of(dst0 + off, ROW_ALIGN), size), :], sem)

    def go(copy):
        if wait:
            copy.wait()
        else:
            copy.start()

    def big(c, carry):
        go(piece(c * RUN_CHUNK, RUN_CHUNK))
        return carry

    lax.fori_loop(0, lax.shift_right_logical(length, RUN_CHUNK.bit_length() - 1), big, 0)
    size = RUN_CHUNK // 2
    while size >= ROW_ALIGN:
        @pl.when((length & size) != 0)
        def _(size=size):
            go(piece(length & ~(2 * size - 1), size))
        size //= 2


def _wait_rows(src_ref, dst_ref, total, max_rows, sem):
    size = pl.next_power_of_2(max_rows)
    while size >= ROW_ALIGN:
        if size <= max_rows:
            @pl.when((total & size) != 0)
            def _(size=size):
                pltpu.make_async_copy(src_ref.at[pl.ds(0, size), :], dst_ref.at[pl.ds(0, size), :], sem).wait()
        size //= 2


def _moe_dispatch_kernel(gap_ref, prev_ref, meta_ref, x_ref, pos_ref, xs_ref, ybuf, zbuf, sems):
    i = pl.program_id(0)
    last = pl.num_programs(0) - 1
    slot = i % 2
    cap, td = ybuf.shape[1], x_ref.shape[0]
    pos = pos_ref[...]
    rid = lax.broadcasted_iota(I32, (cap, td), 0)
    hit = rid == pos[0:1]
    for k in range(1, TOP_K):
        hit = jnp.logical_or(hit, rid == pos[k:k + 1])
    ybuf[slot] = _pack_bf16_pairs(_dot(jnp.where(hit, 1.0, 0.0).astype(BF16), x_ref[...]))

    def wait_runs(m_ref, buf_slot):
        _wait_rows(ybuf.at[buf_slot], xs_ref, m_ref[0, 3, 0], cap, sems.at[buf_slot])

    @pl.when(i > 0)
    def _():
        wait_runs(prev_ref, 1 - slot)

    def run(e, carry):
        _copy_rows(ybuf.at[slot], meta_ref[0, 0, e], xs_ref, meta_ref[0, 2, e], meta_ref[0, 1, e], sems.at[slot],
                   wait=False)
        return carry

    lax.fori_loop(0, N_EXPERTS, run, 0, unroll=4)

    @pl.when(i == last)
    def _():
        wait_runs(meta_ref, slot)
        zbuf[...] = jnp.zeros_like(zbuf)
        for wait in (False, True):
            def fill(e, carry, wait=wait):
                _copy_rows(zbuf, 0, xs_ref, gap_ref[e], gap_ref[N_EXPERTS + e], sems.at[slot], wait=wait, src_fixed=True)
                return carry
            lax.fori_loop(0, N_EXPERTS, fill, 0)


def _moe_dispatch(gap, meta, xm, pos, p, td):
    n, d = xm.shape
    cap = TOP_K * td + N_EXPERTS * ROW_ALIGN
    meta_spec = lambda im: pl.BlockSpec((1, 4, N_EXPERTS), im, memory_space=pltpu.SMEM)
    grid_spec = pltpu.PrefetchScalarGridSpec(
        num_scalar_prefetch=1,
        grid=(n // td,),
        in_specs=[meta_spec(lambda i, g: (jnp.maximum(i - 1, 0), 0, 0)), meta_spec(lambda i, g: (i, 0, 0)),
                  pl.BlockSpec((td, d), lambda i, g: (i, 0)),
                  pl.BlockSpec((TOP_K, td), lambda i, g: (0, i))],
        out_specs=pl.BlockSpec(memory_space=pl.ANY),
        scratch_shapes=[pltpu.VMEM((2, cap, d // 2), U32), pltpu.VMEM((RUN_CHUNK, d // 2), U32),
                        pltpu.SemaphoreType.DMA((2,))],
    )
    return pl.pallas_call(
        _moe_dispatch_kernel,
        grid_spec=grid_spec,
        out_shape=jax.ShapeDtypeStruct((p, d // 2), U32),
        compiler_params=_params(("arbitrary",), VMEM_LIMIT),
    )(gap, meta, meta, xm, pos)


def _moe_expert_kernel(te_ref, nv_ref, xs_ref, w1_ref, b1_ref, w2_ref, b2_ref, ys_ref, w1b, w2b):
    i = pl.program_id(0)
    f = w2_ref.shape[2]
    changed = jnp.logical_or(i == 0, te_ref[i] != te_ref[jnp.maximum(i - 1, 0)])

    @pl.when(jnp.logical_and(changed, i < nv_ref[0]))
    def _():
        w1b[...] = w1_ref[0, 0].astype(BF16)
        w2b[...] = w2_ref[0, 0].astype(BF16)

    @pl.when(i < nv_ref[0])
    def _():
        lo, hi = _unpack_bf16_pairs(xs_ref[...])
        x = jnp.concatenate([lo, hi], axis=-1).astype(BF16)
        y = jnp.zeros((x.shape[0], w2_ref.shape[3]), F32) + b2_ref[0, 0]
        fc = f
        for c in range(f // fc):
            glu = _dot(x, w1b[:, c * fc:(c + 1) * fc]) + b1_ref[0, 0, :, c * fc:(c + 1) * fc]
            lin = _dot(x, w1b[:, f + c * fc:f + (c + 1) * fc]) + b1_ref[0, 0, :, f + c * fc:f + (c + 1) * fc]
            glu = jnp.minimum(glu, SWIGLU_LIMIT)
            lin = jnp.clip(lin, -SWIGLU_LIMIT, SWIGLU_LIMIT)
            act = glu * jax.nn.sigmoid(SWIGLU_ALPHA * glu) * (lin + 1.0)
            y = y + _dot(act.astype(BF16), w2b[c * fc:(c + 1) * fc, :])
        ys_ref[...] = _pack_bf16_pairs(y)


def _moe_experts(tile_expert, n_valid, xs, layer, w1, b1, w2, b2, tme):
    p, d2 = xs.shape
    _, ne, d, f2 = w1.shape
    f = f2 // 2
    row = lambda i, te, nv: (jnp.minimum(i, nv[0] - 1), 0)
    wsel = lambda i, te, nv: (layer, te[jnp.minimum(i, nv[0] - 1)], 0, 0)
    grid_spec = pltpu.PrefetchScalarGridSpec(
        num_scalar_prefetch=2,
        grid=(p // tme,),
        in_specs=[pl.BlockSpec((tme, d2), row),
                  pl.BlockSpec((1, 1, d, f2), wsel), pl.BlockSpec((1, 1, 1, f2), wsel),
                  pl.BlockSpec((1, 1, f, d), wsel), pl.BlockSpec((1, 1, 1, d), wsel)],
        out_specs=pl.BlockSpec((tme, d2), row),
        scratch_shapes=[pltpu.VMEM((d, f2), BF16), pltpu.VMEM((f, d), BF16)],
    )
    depth = w1.shape[0]
    return pl.pallas_call(
        _moe_expert_kernel,
        grid_spec=grid_spec,
        out_shape=jax.ShapeDtypeStruct((p, d2), U32),
        compiler_params=_params(("arbitrary",), VMEM_LIMIT),
    )(tile_expert, n_valid, xs, w1, b1.reshape(depth, ne, 1, f2), w2, b2.reshape(depth, ne, 1, d))


def _moe_combine_kernel(meta_ref, next_ref, ys_ref, pos_ref, wt_ref, h_ref, gate_ref, o_ref, ybuf, sems):
    t = pl.program_id(0) * pl.num_programs(1) + pl.program_id(1)
    n_tiles = pl.num_programs(0) * pl.num_programs(1)
    slot = t % 2

    cap, td = ybuf.shape[1], pos_ref.shape[0]

    def fetch_runs(m_ref, buf_slot):
        def run(e, carry):
            _copy_rows(ys_ref, m_ref[0, 2, e], ybuf.at[buf_slot], m_ref[0, 0, e], m_ref[0, 1, e], sems.at[buf_slot],
                       wait=False)
            return carry
        lax.fori_loop(0, N_EXPERTS, run, 0, unroll=4)

    @pl.when(t == 0)
    def _():
        ybuf[...] = jnp.zeros_like(ybuf)
        fetch_runs(meta_ref, slot)

    @pl.when(t + 1 < n_tiles)
    def _():
        fetch_runs(next_ref, 1 - slot)

    _wait_rows(ys_ref, ybuf.at[slot], meta_ref[0, 3, 0], cap, sems.at[slot])

    pos = pos_ref[...]
    wt = wt_ref[...]
    cid = lax.broadcasted_iota(I32, (td, cap), 1)
    mix = jnp.zeros((td, cap), F32)
    for k in range(TOP_K):
        mix = mix + jnp.where(cid == pos[:, k:k + 1], wt[:, k:k + 1], 0.0)
    lo, hi = _unpack_bf16_pairs(ybuf[slot])
    y = jnp.concatenate([lo, hi], axis=-1).astype(BF16)
    o_ref[0] = h_ref[0] + gate_ref[0] * _dot(mix.astype(BF16), y)


def _moe_combine(meta, ys, pos_t, wt_t, h, gate, td):
    b, s, d = h.shape
    nt = s // td
    cap = TOP_K * td + N_EXPERTS * ROW_ALIGN
    flat = lambda bi, ti: (bi * nt + ti, 0)
    meta_spec = lambda im: pl.BlockSpec((1, 4, N_EXPERTS), im, memory_space=pltpu.SMEM)
    return pl.pallas_call(
        _moe_combine_kernel,
        grid=(b, nt),
        in_specs=[meta_spec(lambda bi, ti: (bi * nt + ti, 0, 0)),
                  meta_spec(lambda bi, ti: (jnp.minimum(bi * nt + ti + 1, b * nt - 1), 0, 0)),
                  pl.BlockSpec(memory_space=pl.ANY),
                  pl.BlockSpec((td, TOP_K), flat), pl.BlockSpec((td, TOP_K), flat),
                  pl.BlockSpec((1, td, d), lambda bi, ti: (bi, ti, 0)),
                  pl.BlockSpec((1, 1, d), lambda bi, ti: (bi, 0, 0))],
        out_specs=pl.BlockSpec((1, td, d), lambda bi, ti: (bi, ti, 0)),
        out_shape=jax.ShapeDtypeStruct((b, s, d), F32),
        scratch_shapes=[pltpu.VMEM((2, cap, d // 2), U32), pltpu.SemaphoreType.DMA((2,))],
        compiler_params=_params(("arbitrary", "arbitrary"), VMEM_LIMIT),
    )(meta, meta, ys, pos_t, wt_t, h, gate)


def _moe(h1, xm, pos, wgt, counts, gate, layer, w1, b1, w2, b2, tme, td):
    n, d = xm.shape
    per = (n // td) // counts.shape[0]
    cnt = jnp.round(counts[:, :, :per]).astype(I32).transpose(0, 2, 1).reshape(n // td, N_EXPERTS)
    run = (cnt + ROW_ALIGN - 1) // ROW_ALIGN * ROW_ALIGN
    local = jnp.cumsum(run, axis=1) - run
    tot = jnp.sum(run, axis=0)
    cap = (tot + tme - 1) // tme * tme
    ends = jnp.cumsum(cap)
    start = ends - cap
    glob = start[None, :] + jnp.cumsum(run, axis=0) - run
    rows_used = jnp.broadcast_to(jnp.sum(run, axis=1, keepdims=True), run.shape)
    meta = jnp.stack([local, run, glob, rows_used], axis=1)
    gap = jnp.concatenate([start + tot, cap - tot]).astype(I32)
    p = (TOP_K * n + N_EXPERTS * ROW_ALIGN * (n // td) + N_EXPERTS * tme) // tme * tme
    tiles = jnp.arange(p // tme, dtype=I32)
    tile_expert = jnp.minimum(jnp.sum(tiles[:, None] >= (ends // tme)[None, :], axis=1), N_EXPERTS - 1).astype(I32)
    n_valid = (ends[-1:] // tme).astype(I32)
    xs = _moe_dispatch(gap, meta, xm, pos, p, td)
    ys = _moe_experts(tile_expert, n_valid, xs, layer, w1, b1, w2, b2, tme)
    return _moe_combine(meta, ys, pos.T, wgt.T, h1, gate, td)


def _rope_tables(s):
    pos = np.arange(s)
    lane = np.arange(LANES)
    dd = lane % NAT_HEAD_DIM
    n = NAT_HEAD_DIM // 4
    inv_freq = ROPE_BASE ** (-(dd % n).astype(np.float64) / n)
    p = np.where((dd // (NAT_HEAD_DIM // 2))[None, :] == 0, (pos // GRID_W)[:, None], (pos % GRID_W)[:, None])
    ang = (p.astype(np.float32) * inv_freq.astype(np.float32)[None, :]).astype(np.float32)
    sign = np.where((dd % (2 * n)) < n, -1.0, 1.0)[None, :]
    return jnp.asarray(np.cos(ang), F32), jnp.asarray(np.sin(ang) * sign, F32)


def _tile(n, want):
    t = min(want, n)
    while n % t:
        t //= 2
    return t


def kernel(x, c, ctx, c_ctx, ada_w, ada_b, norm1_g, norm2_g, ab_w_in, ab_w_out, nat_q_norm, nat_k_norm, nat_rpb, hgrn_lb, hgrn_o_norm, conv_w1, conv_b1, conv_dw, conv_dw_b, conv_ln_g, conv_ln_b, conv_w2, conv_b2, router_w, router_b, moe_w1, moe_b1, moe_w2, moe_b2):
    b, s, d = x.shape
    l = ctx.shape[1]
    rows = s // GRID_W
    assert ada_w.shape[0] == 2 and b < MOD_ROWS and rows >= NAT_KH and rows % NAT_KH == 0
    assert s % HGRN_CHUNK == 0 and l % HGRN_CHUNK == 0 and d % (2 * LANES) == 0
    tm = _tile(s, 512)
    td = tm
    tme = 512 if TOP_K * b * s >= 512 * N_EXPERTS else 128

    cc = jnp.zeros((MOD_ROWS, d), F32).at[:b].set(c).at[b].set(c_ctx)
    mod = _modulation(cc, ada_w, ada_b)

    def mod_vec(layer, i):
        return mod[layer, :b, i * d:(i + 1) * d].reshape(b, 1, d)

    def ctx_vec(i):
        return jnp.broadcast_to(mod[0, b, i * d:(i + 1) * d].reshape(1, 1, d), (b, 1, d))

    tok = np.arange(tm)
    strict_upper = jnp.asarray((tok[:, None] < tok[None, :]) & (tok[:, None] // td == tok[None, :] // td), BF16)

    def tail_args(layer):
        rw = router_w[layer].T.astype(F32)
        rw_hi = rw.astype(BF16)
        rw_split = jnp.concatenate([rw_hi, (rw - rw_hi.astype(F32)).astype(BF16)], axis=0)
        return (norm2_g[layer].reshape(1, d), mod_vec(layer, 3), mod_vec(layer, 4),
                rw_split, router_b[layer].reshape(N_EXPERTS, 1), strict_upper)

    lb_all = jnp.cumsum(jax.nn.softmax(hgrn_lb.astype(F32), axis=1), axis=1)[:, 0]
    w_in = ab_w_in[0].astype(BF16)
    scale = NAT_HEAD_DIM ** -0.5
    qg = jnp.tile(nat_q_norm[0] * scale, NAT_HEADS).reshape(1, NAT_WIDTH)
    kg = jnp.tile(nat_k_norm[0], NAT_HEADS).reshape(1, NAT_WIDTH)
    head_of = np.arange(NAT_WIDTH) // NAT_HEAD_DIM
    bd = jnp.asarray(head_of[:, None] == head_of[None, :], BF16)
    lbf, lbb = lb_all[0].reshape(1, HGRN_WIDTH), lb_all[1].reshape(1, HGRN_WIDTH)
    cos, sin = _rope_tables(s)
    g1 = norm1_g[0].reshape(1, d)
    lat = _inproj(x, mod_vec(0, 0), mod_vec(0, 1), g1, w_in, cos, sin, qg, kg, bd, lbf, lbb, tm)
    qa, qb, k, v, hq, kf, lff, kb, lfb, hi, sg = lat
    tl = _tile(l, 256)
    cxt = _inproj(ctx, ctx_vec(0), ctx_vec(1), g1, w_in, jnp.ones((l, LANES), F32), jnp.zeros((l, LANES), F32),
                  qg, kg, bd, lbf, lbb, tl)
    _, _, kx, vx, _, kfx, lffx, kbx, lfbx, hix, _ = cxt

    nat = _nat_attention(qa, qb, k, v, kx, vx, _nat_bias_table(nat_rpb[0]), rows)

    tri_f = jnp.asarray(np.tril(np.ones((HGRN_CHUNK, HGRN_CHUNK), np.float32)))
    s0 = jnp.zeros((b, HGRN_HEADS, HGRN_DIM, HGRN_DIM), F32)
    _, sf = _hgrn_scan(None, kfx, lffx, hix, s0, tri_f, False, False)
    of, _ = _hgrn_scan(hq, kf, lff, hi, sf, tri_f, False, True)
    _, sb = _hgrn_scan(None, kbx, lfbx, hix, s0, tri_f.T, True, False)
    ob, _ = _hgrn_scan(hq, kb, lfb, hi, sb, tri_f.T, True, True)

    h1, xm, pos, wgt, counts = _outproj_route(
        nat, of, ob, sg, x, mod_vec(0, 2), hgrn_o_norm[0].reshape(1, HGRN_DIM), ab_w_out[0].astype(BF16),
        tail_args(0), tm, td)
    h = _moe(h1, xm, pos, wgt, counts, mod_vec(0, 5), 0, moe_w1, moe_b1, moe_w2, moe_b2, tme, td)

    u = _conf_in(h, mod_vec(1, 0), mod_vec(1, 1), norm1_g[1].reshape(1, d), conv_w1[0].astype(BF16),
                 conv_b1[0].reshape(1, 2 * d), tm)
    h1, xm, pos, wgt, counts = _conf_out_route(
        u, conv_dw[0], conv_dw_b[0].reshape(1, d), conv_ln_g[0].reshape(1, d), conv_ln_b[0].reshape(1, d),
        conv_w2[0].astype(BF16), conv_b2[0].reshape(1, d), h, mod_vec(1, 2), tail_args(1), tm, td)
    return _moe(h1, xm, pos, wgt, counts, mod_vec(1, 5), 1, moe_w1, moe_b1, moe_w2, moe_b2, tme, td)
```

```python
import functools

import numpy as np
import jax
import jax.numpy as jnp
from jax import lax
from jax.experimental import pallas as pl
from jax.experimental.pallas import tpu as pltpu

F32 = jnp.float32
BF16 = jnp.bfloat16
U32 = jnp.uint32
I32 = jnp.int32
HIGHEST = lax.Precision.HIGHEST

GRID_W = 64
NAT_HEADS = 8
NAT_HEAD_DIM = 64
NAT_WIDTH = NAT_HEADS * NAT_HEAD_DIM
NAT_KH = 8
NAT_KW = 16
NAT_ROWS_IN_FLIGHT = 2
HGRN_HEADS = 4
HGRN_DIM = 128
HGRN_WIDTH = HGRN_HEADS * HGRN_DIM
HGRN_CHUNK = 64
HGRN_SUB = 16
CONV_WIDTH = 31
CONV_HALO = 16
N_EXPERTS = 32
TOP_K = 4
SWIGLU_LIMIT = 7.0
SWIGLU_ALPHA = 1.702
ROPE_BASE = 10000.0
NORM_EPS = 1e-6
MASK_VALUE = -1e30
EXP_CLAMP = 80.0

SUBLANES = 8
ROW_ALIGN = SUBLANES
RUN_CHUNK = 64
LANES = 128
MOD_ROWS = 16
VMEM_LIMIT = 56 * 1024 * 1024
NT = (((1,), (1,)), ((), ()))


def _params(sem, vmem=None):
    return pltpu.CompilerParams(dimension_semantics=sem, vmem_limit_bytes=vmem)


def _dot(a, b):
    return jnp.dot(a, b, preferred_element_type=F32)


def _dot_nt(a, b):
    return lax.dot_general(a, b, NT, preferred_element_type=F32)


def _sigmoid(x):
    return 0.5 * jnp.tanh(0.5 * x) + 0.5


def _silu(x):
    return x * _sigmoid(x)


def _rms(x, g):
    return x * lax.rsqrt(jnp.mean(x * x, axis=-1, keepdims=True) + NORM_EPS) * g


def _mod_kernel(cc_ref, w_ref, b_ref, o_ref):
    cc = cc_ref[...]
    o_ref[0] = jnp.dot(_silu(cc), w_ref[0], precision=HIGHEST, preferred_element_type=F32) + b_ref[0]


def _modulation(cc, ada_w, ada_b):
    depth, d, n = ada_w.shape
    tn = n // 4
    return pl.pallas_call(
        _mod_kernel,
        grid=(depth, n // tn),
        in_specs=[pl.BlockSpec((MOD_ROWS, d), lambda l, j: (0, 0)),
                  pl.BlockSpec((1, d, tn), lambda l, j: (l, 0, j)),
                  pl.BlockSpec((1, 1, tn), lambda l, j: (l, 0, j))],
        out_specs=pl.BlockSpec((1, MOD_ROWS, tn), lambda l, j: (l, 0, j)),
        out_shape=jax.ShapeDtypeStruct((depth, MOD_ROWS, n), F32),
        compiler_params=_params(("arbitrary", "arbitrary")),
    )(cc, ada_w, ada_b.reshape(depth, 1, n))


def _inproj_kernel(x_ref, sh_ref, sc_ref, g_ref, w_ref, cos_ref, sin_ref, qg_ref, kg_ref, bd_ref, lbf_ref, lbb_ref,
                   qa_ref, qb_ref, k_ref, v_ref, hq_ref, kf_ref, lff_ref, kb_ref, lfb_ref, hi_ref, sg_ref):
    x = x_ref[0]
    xm = (_rms(x, g_ref[...]) * (1.0 + sc_ref[0]) + sh_ref[0]).astype(BF16)
    wd = NAT_WIDTH

    def proj(i):
        return _dot(xm, w_ref[:, i * wd:(i + 1) * wd])

    def head_norm(y, g):
        sq = y * y
        hi = sq.astype(BF16)
        lo = (sq - hi.astype(F32)).astype(BF16)
        ss = _dot(hi, bd_ref[...]) + _dot(lo, bd_ref[...])
        return y * lax.rsqrt(ss * (1.0 / NAT_HEAD_DIM) + NORM_EPS) * g

    cos = cos_ref[...]
    sin = sin_ref[...]
    lane = lax.broadcasted_iota(I32, cos.shape, 1)
    first = (lane % 32) < 16

    def rope(y):
        outs = []
        for gi in range(wd // LANES):
            yg = y[:, gi * LANES:(gi + 1) * LANES]
            partner = jnp.where(first, pltpu.roll(yg, LANES - 16, 1), pltpu.roll(yg, 16, 1))
            outs.append(yg * cos + partner * sin)
        return jnp.concatenate(outs, axis=-1)

    qf = head_norm(proj(0), qg_ref[...])
    qa_ref[0] = rope(qf).astype(BF16)
    qb_ref[0] = qf.astype(BF16)
    k_ref[0] = rope(head_norm(proj(1), kg_ref[...])).astype(BF16)
    v_ref[0] = proj(2).astype(BF16)
    hq_ref[0] = (_silu(proj(3)) * (HGRN_DIM ** -0.5)).astype(BF16)
    for i, lb_ref, kk_ref, lf_ref in ((4, lbf_ref, kf_ref, lff_ref), (5, lbb_ref, kb_ref, lfb_ref)):
        lb = lb_ref[...]
        f = lb + (1.0 - lb) * _sigmoid(proj(i))
        kk_ref[0] = (1.0 - f).astype(BF16)
        lf_ref[0] = jnp.log(f)
    hi_ref[0] = proj(6).astype(BF16)
    sg_ref[0] = _silu(proj(7)).astype(BF16)


def _inproj(x, shift, scale, g, w_bf, cos, sin, qg, kg, bd, lbf, lbb, tm):
    b, t, d = x.shape
    wd = NAT_WIDTH
    tok = lambda bi, ti: (bi, ti, 0)
    vec = lambda bi, ti: (bi, 0, 0)
    const = lambda bi, ti: (0, 0)
    out_dtypes = (BF16, BF16, BF16, BF16, BF16, BF16, F32, BF16, F32, BF16, BF16)
    return pl.pallas_call(
        _inproj_kernel,
        grid=(b, t // tm),
        in_specs=[pl.BlockSpec((1, tm, d), tok), pl.BlockSpec((1, 1, d), vec), pl.BlockSpec((1, 1, d), vec),
                  pl.BlockSpec((1, d), const), pl.BlockSpec(w_bf.shape, const),
                  pl.BlockSpec((tm, LANES), lambda bi, ti: (ti, 0)), pl.BlockSpec((tm, LANES), lambda bi, ti: (ti, 0)),
                  pl.BlockSpec((1, wd), const), pl.BlockSpec((1, wd), const), pl.BlockSpec((wd, wd), const),
                  pl.BlockSpec((1, wd), const), pl.BlockSpec((1, wd), const)],
        out_specs=[pl.BlockSpec((1, tm, wd), tok)] * len(out_dtypes),
        out_shape=[jax.ShapeDtypeStruct((b, t, wd), dt) for dt in out_dtypes],
        compiler_params=_params(("arbitrary", "arbitrary"), VMEM_LIMIT),
    )(x, shift, scale, g, w_bf, cos, sin, qg, kg, bd, lbf, lbb)


def _nat_kernel(qa_ref, qb_ref, kp_ref, kc_ref, kn_ref, vp_ref, vc_ref, vn_ref, kx_ref, vx_ref, bias_ref,
                o_ref, kbuf, vbuf, s1_scr, s2_scr, p1_scr, p2_scr, *, rows, rb):
    i = pl.program_id(1)
    blk = rb * GRID_W
    for s, (kr, vr) in enumerate(((kp_ref, vp_ref), (kc_ref, vc_ref), (kn_ref, vn_ref))):
        kbuf[s * blk:(s + 1) * blk, :] = kr[0]
        vbuf[s * blk:(s + 1) * blk, :] = vr[0]
    lane = lax.broadcasted_iota(I32, (GRID_W, LANES), 1)
    low = lane < NAT_HEAD_DIM
    kwin = NAT_KH * GRID_W
    pair = 2 * GRID_W
    sm_rows = 16

    def stack_heads(q2):
        q2 = q2.astype(F32)
        return jnp.concatenate([jnp.where(low, q2, 0.0), jnp.where(low, 0.0, q2)], axis=0).astype(BF16)

    def window(j):
        r = i * rb + j
        rs = jnp.clip(r - NAT_KH // 2, 0, rows - NAT_KH)
        start = pl.multiple_of((rs - i * rb + rb) * GRID_W, GRID_W)
        return start, rs - r + NAT_KH - 1, pl.multiple_of(j * GRID_W, GRID_W)

    def scores(j, u):
        start, cls, q0 = window(j)
        for p in range(NAT_HEADS // 2):
            ls = slice(p * LANES, (p + 1) * LANES)
            bias = jnp.concatenate([bias_ref[cls, 2 * p], bias_ref[cls, 2 * p + 1]], axis=0)
            s1_scr[u, p * pair:(p + 1) * pair, :] = _dot_nt(stack_heads(qa_ref[0, pl.ds(q0, GRID_W), ls]),
                                                            kbuf[pl.ds(start, kwin), ls]) + bias
            s2_scr[u, p * pair:(p + 1) * pair, :] = _dot_nt(stack_heads(qb_ref[0, pl.ds(q0, GRID_W), ls]),
                                                            kx_ref[0, :, ls])

    def softmax(u):
        for c in range(NAT_HEADS * GRID_W // sm_rows):
            rsl = slice(c * sm_rows, (c + 1) * sm_rows)
            a = s1_scr[u, rsl, :]
            bb = s2_scr[u, rsl, :]
            m = jnp.maximum(jnp.max(a, axis=-1, keepdims=True), jnp.max(bb, axis=-1, keepdims=True))
            ea = jnp.exp(a - m)
            eb = jnp.exp(bb - m)
            inv = 1.0 / (jnp.sum(ea, axis=-1, keepdims=True) + jnp.sum(eb, axis=-1, keepdims=True))
            p1_scr[u, rsl, :] = (ea * inv).astype(BF16)
            p2_scr[u, rsl, :] = (eb * inv).astype(BF16)

    def values(j, u):
        start, _, q0 = window(j)
        for p in range(NAT_HEADS // 2):
            ls = slice(p * LANES, (p + 1) * LANES)
            o = (_dot(p1_scr[u, p * pair:(p + 1) * pair, :], vbuf[pl.ds(start, kwin), ls])
                 + _dot(p2_scr[u, p * pair:(p + 1) * pair, :], vx_ref[0, :, ls]))
            o_ref[0, pl.ds(q0, GRID_W), ls] = jnp.where(low, o[:GRID_W], o[GRID_W:]).astype(BF16)

    nset = s1_scr.shape[0]

    def row_group(jg, carry):
        for u in range(nset):
            scores(nset * jg + u, u)
        for u in range(nset):
            softmax(u)
        for u in range(nset):
            values(nset * jg + u, u)
        return carry

    lax.fori_loop(0, rb // nset, row_group, 0)


def _nat_attention(qa, qb, k, v, kx, vx, bias, rows):
    b, s, wd = qa.shape
    lx = kx.shape[1]
    rb = NAT_KH
    blk = rb * GRID_W
    nb = rows // rb
    cur = lambda bi, i: (bi, i, 0)
    prv = lambda bi, i: (bi, jnp.maximum(i - 1, 0), 0)
    nxt = lambda bi, i: (bi, jnp.minimum(i + 1, nb - 1), 0)
    ctx = lambda bi, i: (bi, 0, 0)
    tile = lambda im: pl.BlockSpec((1, blk, wd), im)
    return pl.pallas_call(
        functools.partial(_nat_kernel, rows=rows, rb=rb),
        grid=(b, nb),
        in_specs=[tile(cur), tile(cur), tile(prv), tile(cur), tile(nxt), tile(prv), tile(cur), tile(nxt),
                  pl.BlockSpec((1,) + kx.shape[1:], ctx), pl.BlockSpec((1,) + vx.shape[1:], ctx),
                  pl.BlockSpec(bias.shape, lambda bi, i: (0, 0, 0, 0))],
        out_specs=tile(cur),
        out_shape=jax.ShapeDtypeStruct((b, s, wd), BF16),
        scratch_shapes=[pltpu.VMEM((3 * blk, wd), BF16), pltpu.VMEM((3 * blk, wd), BF16),
                        pltpu.VMEM((NAT_ROWS_IN_FLIGHT, NAT_HEADS * GRID_W, NAT_KH * GRID_W), F32),
                        pltpu.VMEM((NAT_ROWS_IN_FLIGHT, NAT_HEADS * GRID_W, lx), F32),
                        pltpu.VMEM((NAT_ROWS_IN_FLIGHT, NAT_HEADS * GRID_W, NAT_KH * GRID_W), BF16),
                        pltpu.VMEM((NAT_ROWS_IN_FLIGHT, NAT_HEADS * GRID_W, lx), BF16)],
        compiler_params=_params(("arbitrary", "arbitrary"), VMEM_LIMIT),
    )(qa, qb, k, k, k, v, v, v, kx, vx, bias)


def _nat_bias_table(rpb):
    qc = np.arange(GRID_W)
    wc = np.clip(qc - NAT_KW // 2, 0, GRID_W - NAT_KW)
    kc = np.arange(GRID_W)
    valid = (kc[None, :] >= wc[:, None]) & (kc[None, :] < wc[:, None] + NAT_KW)
    cidx = np.clip(kc[None, :] - qc[:, None] + NAT_KW - 1, 0, 2 * NAT_KW - 2)
    cols = jnp.where(valid[None, None], rpb.astype(F32)[:, :, cidx], MASK_VALUE)
    tab = jnp.stack([cols[:, c:c + NAT_KH] for c in range(NAT_KH)], axis=0)
    tab = tab.transpose(0, 1, 3, 2, 4)
    return tab.reshape(NAT_KH, NAT_HEADS, GRID_W, NAT_KH * GRID_W)


def _hgrn_kernel(*refs, rev, with_out, cb):
    if with_out:
        q_ref, kk_ref, lf_ref, v_ref, s0_ref, tri_ref, o_ref, sf_ref, st = refs
    else:
        kk_ref, lf_ref, v_ref, s0_ref, tri_ref, sf_ref, st = refs
    i = pl.program_id(1)
    c_sz = HGRN_CHUNK

    @pl.when(i == 0)
    def _():
        st[...] = s0_ref[0]

    tri = tri_ref[...]
    tri_b = tri.astype(BF16)
    keep = tri > 0.5

    def chunk(ci, carry):
        c = (cb - 1 - ci) if rev else ci
        r0 = pl.multiple_of(c * c_sz, c_sz)
        lf = lf_ref[0, pl.ds(r0, c_sz), :]
        lf_hi = lf.astype(BF16)
        rest = lf - lf_hi.astype(F32)
        lf_mid = rest.astype(BF16)
        lf_lo = (rest - lf_mid.astype(F32)).astype(BF16)
        bsum = _dot(tri_b, lf_hi) + (_dot(tri_b, lf_mid) + _dot(tri_b, lf_lo))
        for h in range(HGRN_HEADS):
            ls = slice(h * HGRN_DIM, (h + 1) * HGRN_DIM)
            bh = bsum[:, ls]
            tot = bh[0:1] if rev else bh[c_sz - 1:c_sz]
            kh = kk_ref[0, pl.ds(r0, c_sz), ls].astype(F32)
            vh = v_ref[0, pl.ds(r0, c_sz), ls]
            state = st[h]
            if with_out:
                qh = q_ref[0, pl.ds(r0, c_sz), ls].astype(F32)
                blocks = []
                for sb in range(c_sz // HGRN_SUB):
                    lo = sb * HGRN_SUB
                    ref_row = lo + HGRN_SUB - 1 if rev else lo
                    cref = bh[ref_row:ref_row + 1]
                    f1 = jnp.exp(bh[lo:lo + HGRN_SUB] - cref)
                    f2 = jnp.exp(jnp.minimum(cref - bh, EXP_CLAMP))
                    blocks.append(_dot_nt((qh[lo:lo + HGRN_SUB] * f1).astype(BF16), (kh * f2).astype(BF16)))
                scores = jnp.where(keep, jnp.concatenate(blocks, axis=0), 0.0)
                o = _dot(scores.astype(BF16), vh) + _dot_nt((qh * jnp.exp(bh)).astype(BF16), state.astype(BF16))
                o_ref[0, pl.ds(r0, c_sz), ls] = o.astype(BF16)
            kd = (kh * jnp.exp(tot - bh)).astype(BF16)
            vt = vh.astype(F32).T.astype(BF16)
            st[h] = state * jnp.exp(tot) + _dot(vt, kd)
        return carry

    lax.fori_loop(0, cb, chunk, 0, unroll=4)

    @pl.when(i == pl.num_programs(1) - 1)
    def _():
        sf_ref[0] = st[...]


def _hgrn_scan(q, kk, lf, v, s0, tri, rev, with_out):
    b, t, wd = kk.shape
    nchunk = t // HGRN_CHUNK
    cb = min(8, nchunk)
    nblk = nchunk // cb
    tb = cb * HGRN_CHUNK
    tok = (lambda bi, i: (bi, nblk - 1 - i, 0)) if rev else (lambda bi, i: (bi, i, 0))
    st_map = lambda bi, i: (bi, 0, 0, 0)
    st_shape = (b, HGRN_HEADS, HGRN_DIM, HGRN_DIM)
    tile = pl.BlockSpec((1, tb, wd), tok)
    st_spec = pl.BlockSpec((1,) + st_shape[1:], st_map)
    ins = ([q] if with_out else []) + [kk, lf, v, s0, tri]
    in_specs = [tile] * (len(ins) - 2) + [st_spec, pl.BlockSpec(tri.shape, lambda bi, i: (0, 0))]
    out_specs = ([tile] if with_out else []) + [st_spec]
    out_shape = ([jax.ShapeDtypeStruct((b, t, wd), BF16)] if with_out else []) + [jax.ShapeDtypeStruct(st_shape, F32)]
    res = pl.pallas_call(
        functools.partial(_hgrn_kernel, rev=rev, with_out=with_out, cb=cb),
        grid=(b, nblk),
        in_specs=in_specs, out_specs=out_specs, out_shape=out_shape,
        scratch_shapes=[pltpu.VMEM(st_shape[1:], F32)],
        compiler_params=_params(("arbitrary", "arbitrary"), VMEM_LIMIT),
    )(*ins)
    return res if with_out else (None, res[0])


def _pack_bf16_pairs(x):
    half = x.shape[-1] // 2
    lo = pltpu.bitcast(x[:, :half].astype(BF16).astype(F32), U32) >> 16
    hi = pltpu.bitcast(x[:, half:].astype(BF16).astype(F32), U32) & jnp.uint32(0xFFFF0000)
    return hi | lo


def _unpack_bf16_pairs(w):
    lo = pltpu.bitcast(w << 16, F32)
    hi = pltpu.bitcast(w & jnp.uint32(0xFFFF0000), F32)
    return lo, hi


def _route_tail(h1, g2_ref, sh2_ref, sc2_ref, rwt_ref, rb_ref, su_ref,
                h1_ref, xm_ref, pos_ref, wgt_ref, cnt_ref, *, td):
    h1_ref[0] = h1
    xm2 = _rms(h1, g2_ref[...]) * (1.0 + sc2_ref[0]) + sh2_ref[0]
    xm_hi = xm2.astype(BF16)
    xm_ref[...] = xm_hi
    xm_lo = (xm2 - xm_hi.astype(F32)).astype(BF16)
    ne = rb_ref.shape[0]
    part = _dot_nt(rwt_ref[...], xm_hi)
    logits = part[:ne] + part[ne:] + _dot_nt(rwt_ref[:ne, :], xm_lo) + rb_ref[...]
    tm = logits.shape[1]
    eid = lax.broadcasted_iota(I32, (ne, tm), 0).astype(F32)
    vals, hots = [], []
    for k in range(TOP_K):
        m = jnp.max(logits, axis=0, keepdims=True)
        sel = jnp.min(jnp.where(logits == m, eid, float(ne)), axis=0, keepdims=True)
        hot = eid == sel
        logits = jnp.where(hot, -jnp.inf, logits)
        vals.append(m)
        hots.append(hot)
    es = [jnp.exp(vv - vals[0]) for vv in vals]
    den = es[0] + es[1] + es[2] + es[3]
    onehot = jnp.zeros((ne, tm), F32)
    for k in range(TOP_K):
        wgt_ref[k:k + 1, :] = es[k] / den
        onehot = onehot + hots[k].astype(F32)
    prefix = _dot(onehot.astype(BF16), su_ref[...])
    strict_lower = (lax.broadcasted_iota(I32, (ne, ne), 0) > lax.broadcasted_iota(I32, (ne, ne), 1)).astype(BF16)
    lane = lax.broadcasted_iota(I32, (ne, LANES), 1)
    counts = jnp.zeros((ne, LANES), F32)
    starts = []
    for j in range(tm // td):
        cj = jnp.sum(onehot[:, j * td:(j + 1) * td], axis=1, keepdims=True)
        padded = jnp.floor((cj + (ROW_ALIGN - 1)) * (1.0 / ROW_ALIGN)) * ROW_ALIGN
        group_start = _dot(strict_lower, jnp.broadcast_to(padded, (ne, LANES)).astype(BF16))
        starts.append(jnp.broadcast_to(group_start[:, 0:1], (ne, td)))
        counts = counts + jnp.where(lane == j, cj, 0.0)
    row = prefix + jnp.concatenate(starts, axis=1)
    for k in range(TOP_K):
        pos_ref[k:k + 1, :] = jnp.sum(jnp.where(hots[k], row, 0.0), axis=0, keepdims=True).astype(I32)
    cnt_ref[0] = counts


def _tail_specs(b, s, d, tm):
    nt = s // tm
    n = b * s
    vec = lambda bi, ti: (bi, 0, 0)
    const = lambda bi, ti: (0, 0)
    in_specs = [pl.BlockSpec((1, d), const), pl.BlockSpec((1, 1, d), vec), pl.BlockSpec((1, 1, d), vec),
                pl.BlockSpec((2 * N_EXPERTS, d), const), pl.BlockSpec((N_EXPERTS, 1), const),
                pl.BlockSpec((tm, tm), const)]
    flat = lambda bi, ti: (0, bi * nt + ti)
    out_specs = [pl.BlockSpec((1, tm, d), lambda bi, ti: (bi, ti, 0)),
                 pl.BlockSpec((tm, d), lambda bi, ti: (bi * nt + ti, 0)),
                 pl.BlockSpec((TOP_K, tm), flat), pl.BlockSpec((TOP_K, tm), flat),
                 pl.BlockSpec((1, N_EXPERTS, LANES), lambda bi, ti: (bi * nt + ti, 0, 0))]
    out_shape = [jax.ShapeDtypeStruct((b, s, d), F32), jax.ShapeDtypeStruct((n, d), BF16),
                 jax.ShapeDtypeStruct((TOP_K, n), I32), jax.ShapeDtypeStruct((TOP_K, n), F32),
                 jax.ShapeDtypeStruct((b * nt, N_EXPERTS, LANES), F32)]
    return in_specs, out_specs, out_shape


def _outproj_kernel(nat_ref, of_ref, ob_ref, sg_ref, x_ref, gate_ref, on_ref, w_ref,
                    g2_ref, sh2_ref, sc2_ref, rwt_ref, rb_ref, su_ref,
                    h1_ref, xm_ref, pos_ref, wgt_ref, cnt_ref, *, td):
    o = of_ref[0].astype(F32) + ob_ref[0].astype(F32)
    gated = []
    for h in range(HGRN_HEADS):
        ls = slice(h * HGRN_DIM, (h + 1) * HGRN_DIM)
        gated.append(_rms(o[:, ls], on_ref[...]) * sg_ref[0, :, ls].astype(F32))
    gated = jnp.concatenate(gated, axis=-1).astype(BF16)
    y = _dot(nat_ref[0], w_ref[:NAT_WIDTH, :]) + _dot(gated, w_ref[NAT_WIDTH:, :])
    h1 = x_ref[0] + gate_ref[0] * y
    _route_tail(h1, g2_ref, sh2_ref, sc2_ref, rwt_ref, rb_ref, su_ref,
                h1_ref, xm_ref, pos_ref, wgt_ref, cnt_ref, td=td)


def _outproj_route(nat, of, ob, sg, x, gate, on_g, w_bf, tail_args, tm, td):
    b, s, d = x.shape
    tok = lambda bi, ti: (bi, ti, 0)
    vec = lambda bi, ti: (bi, 0, 0)
    const = lambda bi, ti: (0, 0)
    t_in, out_specs, out_shape = _tail_specs(b, s, d, tm)
    wide = pl.BlockSpec((1, tm, NAT_WIDTH), tok)
    return pl.pallas_call(
        functools.partial(_outproj_kernel, td=td),
        grid=(b, s // tm),
        in_specs=[wide, wide, wide, wide, pl.BlockSpec((1, tm, d), tok), pl.BlockSpec((1, 1, d), vec),
                  pl.BlockSpec((1, HGRN_DIM), const), pl.BlockSpec(w_bf.shape, const)] + t_in,
        out_specs=out_specs, out_shape=out_shape,
        compiler_params=_params(("arbitrary", "arbitrary"), VMEM_LIMIT),
    )(nat, of, ob, sg, x, gate, on_g, w_bf, *tail_args)


def _conf_in_kernel(x_ref, sh_ref, sc_ref, g_ref, w_ref, b_ref, u_ref):
    d = x_ref.shape[-1]
    xm = (_rms(x_ref[0], g_ref[...]) * (1.0 + sc_ref[0]) + sh_ref[0]).astype(BF16)
    a = _dot(xm, w_ref[:, :d]) + b_ref[:, :d]
    gate = _dot(xm, w_ref[:, d:]) + b_ref[:, d:]
    u_ref[0] = (a * _sigmoid(gate)).astype(BF16)


def _conf_in(x, shift, scale, g, w_bf, b1, tm):
    b, s, d = x.shape
    tok = lambda bi, ti: (bi, ti, 0)
    vec = lambda bi, ti: (bi, 0, 0)
    const = lambda bi, ti: (0, 0)
    return pl.pallas_call(
        _conf_in_kernel,
        grid=(b, s // tm),
        in_specs=[pl.BlockSpec((1, tm, d), tok), pl.BlockSpec((1, 1, d), vec), pl.BlockSpec((1, 1, d), vec),
                  pl.BlockSpec((1, d), const), pl.BlockSpec(w_bf.shape, const), pl.BlockSpec((1, 2 * d), const)],
        out_specs=pl.BlockSpec((1, tm, d), tok),
        out_shape=jax.ShapeDtypeStruct((b, s, d), BF16),
        compiler_params=_params(("arbitrary", "arbitrary"), VMEM_LIMIT),
    )(x, shift, scale, g, w_bf, b1)


def _conf_out_kernel(up_ref, uc_ref, un_ref, dw_ref, dwb_ref, lg_ref, lb_ref, w_ref, b2_ref, x_ref, gate_ref,
                     g2_ref, sh2_ref, sc2_ref, rwt_ref, rb_ref, su_ref,
                     h1_ref, xm_ref, pos_ref, wgt_ref, cnt_ref, ubuf, cbuf, sbuf, *, rc, td):
    ti = pl.program_id(1)
    tm, d = uc_ref.shape[1], uc_ref.shape[2]
    hal = CONV_HALO
    prev = up_ref[0].astype(F32)
    nxt = un_ref[0].astype(F32)
    ubuf[0:hal, :] = jnp.where(ti > 0, prev, jnp.zeros_like(prev))
    ubuf[hal:hal + tm, :] = uc_ref[0].astype(F32)
    ubuf[hal + tm:, :] = jnp.where(ti < pl.num_programs(1) - 1, nxt, jnp.zeros_like(nxt))
    base = hal - CONV_WIDTH // 2
    lc = sbuf.shape[2]
    sub = sbuf.shape[0] + 1
    srows = sbuf.shape[1]
    bc = 64

    def shift_chunk(r0, nrows, l0):
        win = ubuf[pl.ds(r0, nrows + sub), l0:l0 + lc]
        for r in range(1, sub):
            sbuf[r - 1, pl.ds(r0, nrows), :] = win[r:r + nrows]

    for l0 in range(0, d, lc):
        def shift_rows(ci, carry, l0=l0):
            shift_chunk(pl.multiple_of(ci * bc, bc), bc, l0)
            return carry

        lax.fori_loop(0, srows // bc, shift_rows, 0)
        if srows % bc:
            shift_chunk(srows // bc * bc, srows % bc, l0)

        def conv_rows(ci, carry, l0=l0):
            r0 = pl.multiple_of(ci * rc, rc)
            acc = jnp.zeros((rc, lc), F32) + dwb_ref[:, l0:l0 + lc]
            for j in range(CONV_WIDTH):
                r = (base + j) % sub
                a = pl.multiple_of(r0 + (base + j - r), sub)
                src = ubuf[pl.ds(a, rc), l0:l0 + lc] if r == 0 else sbuf[r - 1, pl.ds(a, rc), :]
                acc = acc + src * dw_ref[j:j + 1, l0:l0 + lc]
            cbuf[pl.ds(r0, rc), l0:l0 + lc] = acc
            return carry

        lax.fori_loop(0, tm // rc, conv_rows, 0)
    c = cbuf[...]
    mu = jnp.mean(c, axis=-1, keepdims=True)
    cz = c - mu
    var = jnp.mean(cz * cz, axis=-1, keepdims=True)
    y = _silu(cz * lax.rsqrt(var + NORM_EPS) * lg_ref[...] + lb_ref[...]).astype(BF16)
    y = _dot(y, w_ref[...]) + b2_ref[...]
    h1 = x_ref[0] + gate_ref[0] * y
    _route_tail(h1, g2_ref, sh2_ref, sc2_ref, rwt_ref, rb_ref, su_ref,
                h1_ref, xm_ref, pos_ref, wgt_ref, cnt_ref, td=td)


def _conf_out_route(u, dw, dwb, ln_g, ln_b, w_bf, b2, x, gate, tail_args, tm, td):
    b, s, d = x.shape
    hal = CONV_HALO
    per = tm // hal
    nh = s // hal
    tok = lambda bi, ti: (bi, ti, 0)
    vec = lambda bi, ti: (bi, 0, 0)
    const = lambda bi, ti: (0, 0)
    t_in, out_specs, out_shape = _tail_specs(b, s, d, tm)
    return pl.pallas_call(
        functools.partial(_conf_out_kernel, rc=64, td=td),
        grid=(b, s // tm),
        in_specs=[pl.BlockSpec((1, hal, d), lambda bi, ti: (bi, jnp.maximum(ti * per - 1, 0), 0)),
                  pl.BlockSpec((1, tm, d), tok),
                  pl.BlockSpec((1, hal, d), lambda bi, ti: (bi, jnp.minimum((ti + 1) * per, nh - 1), 0)),
                  pl.BlockSpec(dw.shape, const), pl.BlockSpec((1, d), const), pl.BlockSpec((1, d), const),
                  pl.BlockSpec((1, d), const), pl.BlockSpec(w_bf.shape, const), pl.BlockSpec((1, d), const),
                  pl.BlockSpec((1, tm, d), tok), pl.BlockSpec((1, 1, d), vec)] + t_in,
        out_specs=out_specs, out_shape=out_shape,
        scratch_shapes=[pltpu.VMEM((tm + 2 * hal, d), F32), pltpu.VMEM((tm, d), F32),
                        pltpu.VMEM((SUBLANES - 1, tm + (CONV_WIDTH // SUBLANES) * SUBLANES, min(d, 2 * LANES)), F32)],
        compiler_params=_params(("arbitrary", "arbitrary"), VMEM_LIMIT),
    )(u, u, u, dw, dwb, ln_g, ln_b, w_bf, b2, x, gate, *tail_args)


def _copy_rows(src_ref, src0, dst_ref, dst0, length, sem, *, wait, src_fixed=False):
    def piece(off, size):
        s = src0 if src_fixed else pl.multiple_of(src0 + off, ROW_ALIGN)
        return pltpu.make_async_copy(src_ref.at[pl.ds(s, size), :],
                                     dst_ref.at[pl.ds(pl.multiple_of(dst0 + off, ROW_ALIGN), size), :], sem)

    def go(copy):
        if wait:
            copy.wait()
        else:
            copy.start()

    def big(c, carry):
        go(piece(c * RUN_CHUNK, RUN_CHUNK))
        return carry

    lax.fori_loop(0, lax.shift_right_logical(length, RUN_CHUNK.bit_length() - 1), big, 0)
    size = RUN_CHUNK // 2
    while size >= ROW_ALIGN:
        @pl.when((length & size) != 0)
        def _(size=size):
            go(piece(length & ~(2 * size - 1), size))
        size //= 2


def _wait_rows(src_ref, dst_ref, total, max_rows, sem):
    size = pl.next_power_of_2(max_rows)
    while size >= ROW_ALIGN:
        if size <= max_rows:
            @pl.when((total & size) != 0)
            def _(size=size):
                pltpu.make_async_copy(src_ref.at[pl.ds(0, size), :], dst_ref.at[pl.ds(0, size), :], sem).wait()
        size //= 2


def _moe_dispatch_kernel(gap_ref, prev_ref, meta_ref, x_ref, pos_ref, xs_ref, ybuf, zbuf, sems):
    i = pl.program_id(0)
    last = pl.num_programs(0) - 1
    slot = i % 2
    cap, td = ybuf.shape[1], x_ref.shape[0]
    pos = pos_ref[...]
    rid = lax.broadcasted_iota(I32, (cap, td), 0)
    hit = rid == pos[0:1]
    for k in range(1, TOP_K):
        hit = jnp.logical_or(hit, rid == pos[k:k + 1])
    ybuf[slot] = _pack_bf16_pairs(_dot(jnp.where(hit, 1.0, 0.0).astype(BF16), x_ref[...]))

    def wait_runs(m_ref, buf_slot):
        _wait_rows(ybuf.at[buf_slot], xs_ref, m_ref[0, 3, 0], cap, sems.at[buf_slot])

    @pl.when(i > 0)
    def _():
        wait_runs(prev_ref, 1 - slot)

    def run(e, carry):
        _copy_rows(ybuf.at[slot], meta_ref[0, 0, e], xs_ref, meta_ref[0, 2, e], meta_ref[0, 1, e], sems.at[slot],
                   wait=False)
        return carry

    lax.fori_loop(0, N_EXPERTS, run, 0, unroll=4)

    @pl.when(i == last)
    def _():
        wait_runs(meta_ref, slot)
        zbuf[...] = jnp.zeros_like(zbuf)
        for wait in (False, True):
            def fill(e, carry, wait=wait):
                _copy_rows(zbuf, 0, xs_ref, gap_ref[e], gap_ref[N_EXPERTS + e], sems.at[slot], wait=wait, src_fixed=True)
                return carry
            lax.fori_loop(0, N_EXPERTS, fill, 0)


def _moe_dispatch(gap, meta, xm, pos, p, td):
    n, d = xm.shape
    cap = TOP_K * td + N_EXPERTS * ROW_ALIGN
    meta_spec = lambda im: pl.BlockSpec((1, 4, N_EXPERTS), im, memory_space=pltpu.SMEM)
    grid_spec = pltpu.PrefetchScalarGridSpec(
        num_scalar_prefetch=1,
        grid=(n // td,),
        in_specs=[meta_spec(lambda i, g: (jnp.maximum(i - 1, 0), 0, 0)), meta_spec(lambda i, g: (i, 0, 0)),
                  pl.BlockSpec((td, d), lambda i, g: (i, 0)),
                  pl.BlockSpec((TOP_K, td), lambda i, g: (0, i))],
        out_specs=pl.BlockSpec(memory_space=pl.ANY),
        scratch_shapes=[pltpu.VMEM((2, cap, d // 2), U32), pltpu.VMEM((RUN_CHUNK, d // 2), U32),
                        pltpu.SemaphoreType.DMA((2,))],
    )
    return pl.pallas_call(
        _moe_dispatch_kernel,
        grid_spec=grid_spec,
        out_shape=jax.ShapeDtypeStruct((p, d // 2), U32),
        compiler_params=_params(("arbitrary",), VMEM_LIMIT),
    )(gap, meta, meta, xm, pos)


def _moe_expert_kernel(te_ref, nv_ref, xs_ref, w1_ref, b1_ref, w2_ref, b2_ref, ys_ref, w1b, w2b):
    i = pl.program_id(0)
    f = w2_ref.shape[2]
    changed = jnp.logical_or(i == 0, te_ref[i] != te_ref[jnp.maximum(i - 1, 0)])

    @pl.when(jnp.logical_and(changed, i < nv_ref[0]))
    def _():
        w1b[...] = w1_ref[0, 0].astype(BF16)
        w2b[...] = w2_ref[0, 0].astype(BF16)

    @pl.when(i < nv_ref[0])
    def _():
        lo, hi = _unpack_bf16_pairs(xs_ref[...])
        x = jnp.concatenate([lo, hi], axis=-1).astype(BF16)
        y = jnp.zeros((x.shape[0], w2_ref.shape[3]), F32) + b2_ref[0, 0]
        fc = f
        for c in range(f // fc):
            glu = _dot(x, w1b[:, c * fc:(c + 1) * fc]) + b1_ref[0, 0, :, c * fc:(c + 1) * fc]
            lin = _dot(x, w1b[:, f + c * fc:f + (c + 1) * fc]) + b1_ref[0, 0, :, f + c * fc:f + (c + 1) * fc]
            glu = jnp.minimum(glu, SWIGLU_LIMIT)
            lin = jnp.clip(lin, -SWIGLU_LIMIT, SWIGLU_LIMIT)
            act = glu * _sigmoid(SWIGLU_ALPHA * glu) * (lin + 1.0)
            y = y + _dot(act.astype(BF16), w2b[c * fc:(c + 1) * fc, :])
        ys_ref[...] = _pack_bf16_pairs(y)


def _moe_experts(tile_expert, n_valid, xs, layer, w1, b1, w2, b2, tme):
    p, d2 = xs.shape
    _, ne, d, f2 = w1.shape
    f = f2 // 2
    row = lambda i, te, nv: (jnp.minimum(i, nv[0] - 1), 0)
    wsel = lambda i, te, nv: (layer, te[jnp.minimum(i, nv[0] - 1)], 0, 0)
    grid_spec = pltpu.PrefetchScalarGridSpec(
        num_scalar_prefetch=2,
        grid=(p // tme,),
        in_specs=[pl.BlockSpec((tme, d2), row),
                  pl.BlockSpec((1, 1, d, f2), wsel), pl.BlockSpec((1, 1, 1, f2), wsel),
                  pl.BlockSpec((1, 1, f, d), wsel), pl.BlockSpec((1, 1, 1, d), wsel)],
        out_specs=pl.BlockSpec((tme, d2), row),
        scratch_shapes=[pltpu.VMEM((d, f2), BF16), pltpu.VMEM((f, d), BF16)],
    )
    depth = w1.shape[0]
    return pl.pallas_call(
        _moe_expert_kernel,
        grid_spec=grid_spec,
        out_shape=jax.ShapeDtypeStruct((p, d2), U32),
        compiler_params=_params(("arbitrary",), VMEM_LIMIT),
    )(tile_expert, n_valid, xs, w1, b1.reshape(depth, ne, 1, f2), w2, b2.reshape(depth, ne, 1, d))


def _moe_combine_kernel(meta_ref, next_ref, ys_ref, pos_ref, wt_ref, h_ref, gate_ref, o_ref, ybuf, sems):
    t = pl.program_id(0) * pl.num_programs(1) + pl.program_id(1)
    n_tiles = pl.num_programs(0) * pl.num_programs(1)
    slot = t % 2

    cap, td = ybuf.shape[1], pos_ref.shape[0]

    def fetch_runs(m_ref, buf_slot):
        def run(e, carry):
            _copy_rows(ys_ref, m_ref[0, 2, e], ybuf.at[buf_slot], m_ref[0, 0, e], m_ref[0, 1, e], sems.at[buf_slot],
                       wait=False)
            return carry
        lax.fori_loop(0, N_EXPERTS, run, 0, unroll=4)

    @pl.when(t == 0)
    def _():
        ybuf[...] = jnp.zeros_like(ybuf)
        fetch_runs(meta_ref, slot)

    @pl.when(t + 1 < n_tiles)
    def _():
        fetch_runs(next_ref, 1 - slot)

    _wait_rows(ys_ref, ybuf.at[slot], meta_ref[0, 3, 0], cap, sems.at[slot])

    pos = pos_ref[...]
    wt = wt_ref[...]
    cid = lax.broadcasted_iota(I32, (td, cap), 1)
    mix = jnp.zeros((td, cap), F32)
    for k in range(TOP_K):
        mix = mix + jnp.where(cid == pos[:, k:k + 1], wt[:, k:k + 1], 0.0)
    lo, hi = _unpack_bf16_pairs(ybuf[slot])
    y = jnp.concatenate([lo, hi], axis=-1).astype(BF16)
    o_ref[0] = h_ref[0] + gate_ref[0] * _dot(mix.astype(BF16), y)


def _moe_combine(meta, ys, pos_t, wt_t, h, gate, td):
    b, s, d = h.shape
    nt = s // td
    cap = TOP_K * td + N_EXPERTS * ROW_ALIGN
    flat = lambda bi, ti: (bi * nt + ti, 0)
    meta_spec = lambda im: pl.BlockSpec((1, 4, N_EXPERTS), im, memory_space=pltpu.SMEM)
    return pl.pallas_call(
        _moe_combine_kernel,
        grid=(b, nt),
        in_specs=[meta_spec(lambda bi, ti: (bi * nt + ti, 0, 0)),
                  meta_spec(lambda bi, ti: (jnp.minimum(bi * nt + ti + 1, b * nt - 1), 0, 0)),
                  pl.BlockSpec(memory_space=pl.ANY),
                  pl.BlockSpec((td, TOP_K), flat), pl.BlockSpec((td, TOP_K), flat),
                  pl.BlockSpec((1, td, d), lambda bi, ti: (bi, ti, 0)),
                  pl.BlockSpec((1, 1, d), lambda bi, ti: (bi, 0, 0))],
        out_specs=pl.BlockSpec((1, td, d), lambda bi, ti: (bi, ti, 0)),
        out_shape=jax.ShapeDtypeStruct((b, s, d), F32),
        scratch_shapes=[pltpu.VMEM((2, cap, d // 2), U32), pltpu.SemaphoreType.DMA((2,))],
        compiler_params=_params(("arbitrary", "arbitrary"), VMEM_LIMIT),
    )(meta, meta, ys, pos_t, wt_t, h, gate)


def _moe(h1, xm, pos, wgt, counts, gate, layer, w1, b1, w2, b2, tme, td):
    n, d = xm.shape
    per = (n // td) // counts.shape[0]
    cnt = jnp.round(counts[:, :, :per]).astype(I32).transpose(0, 2, 1).reshape(n // td, N_EXPERTS)
    run = (cnt + ROW_ALIGN - 1) // ROW_ALIGN * ROW_ALIGN
    local = jnp.cumsum(run, axis=1) - run
    tot = jnp.sum(run, axis=0)
    cap = (tot + tme - 1) // tme * tme
    ends = jnp.cumsum(cap)
    start = ends - cap
    glob = start[None, :] + jnp.cumsum(run, axis=0) - run
    rows_used = jnp.broadcast_to(jnp.sum(run, axis=1, keepdims=True), run.shape)
    meta = jnp.stack([local, run, glob, rows_used], axis=1)
    gap = jnp.concatenate([start + tot, cap - tot]).astype(I32)
    p = (TOP_K * n + N_EXPERTS * ROW_ALIGN * (n // td) + N_EXPERTS * tme) // tme * tme
    tiles = jnp.arange(p // tme, dtype=I32)
    tile_expert = jnp.minimum(jnp.sum(tiles[:, None] >= (ends // tme)[None, :], axis=1), N_EXPERTS - 1).astype(I32)
    n_valid = (ends[-1:] // tme).astype(I32)
    xs = _moe_dispatch(gap, meta, xm, pos, p, td)
    ys = _moe_experts(tile_expert, n_valid, xs, layer, w1, b1, w2, b2, tme)
    return _moe_combine(meta, ys, pos.T, wgt.T, h1, gate, td)


def _rope_tables(s):
    pos = np.arange(s)
    lane = np.arange(LANES)
    dd = lane % NAT_HEAD_DIM
    n = NAT_HEAD_DIM // 4
    inv_freq = ROPE_BASE ** (-(dd % n).astype(np.float64) / n)
    p = np.where((dd // (NAT_HEAD_DIM // 2))[None, :] == 0, (pos // GRID_W)[:, None], (pos % GRID_W)[:, None])
    ang = (p.astype(np.float32) * inv_freq.astype(np.float32)[None, :]).astype(np.float32)
    sign = np.where((dd % (2 * n)) < n, -1.0, 1.0)[None, :]
    return jnp.asarray(np.cos(ang), F32), jnp.asarray(np.sin(ang) * sign, F32)


def _tile(n, want):
    t = min(want, n)
    while n % t:
        t //= 2
    return t


def kernel(x, c, ctx, c_ctx, ada_w, ada_b, norm1_g, norm2_g, ab_w_in, ab_w_out, nat_q_norm, nat_k_norm, nat_rpb, hgrn_lb, hgrn_o_norm, conv_w1, conv_b1, conv_dw, conv_dw_b, conv_ln_g, conv_ln_b, conv_w2, conv_b2, router_w, router_b, moe_w1, moe_b1, moe_w2, moe_b2):
    b, s, d = x.shape
    l = ctx.shape[1]
    rows = s // GRID_W
    assert ada_w.shape[0] == 2 and b < MOD_ROWS and rows >= NAT_KH and rows % NAT_KH == 0
    assert s % HGRN_CHUNK == 0 and l % HGRN_CHUNK == 0 and d % (2 * LANES) == 0
    tm = _tile(s, 512)
    td = tm
    tme = 512 if TOP_K * b * s >= 512 * N_EXPERTS else 128

    cc = jnp.zeros((MOD_ROWS, d), F32).at[:b].set(c).at[b].set(c_ctx)
    mod = _modulation(cc, ada_w, ada_b)

    def mod_vec(layer, i):
        return mod[layer, :b, i * d:(i + 1) * d].reshape(b, 1, d)

    def ctx_vec(i):
        return jnp.broadcast_to(mod[0, b, i * d:(i + 1) * d].reshape(1, 1, d), (b, 1, d))

    tok = np.arange(tm)
    strict_upper = jnp.asarray((tok[:, None] < tok[None, :]) & (tok[:, None] // td == tok[None, :] // td), BF16)

    def tail_args(layer):
        rw = router_w[layer].T.astype(F32)
        rw_hi = rw.astype(BF16)
        rw_split = jnp.concatenate([rw_hi, (rw - rw_hi.astype(F32)).astype(BF16)], axis=0)
        return (norm2_g[layer].reshape(1, d), mod_vec(layer, 3), mod_vec(layer, 4),
                rw_split, router_b[layer].reshape(N_EXPERTS, 1), strict_upper)

    lb_all = jnp.cumsum(jax.nn.softmax(hgrn_lb.astype(F32), axis=1), axis=1)[:, 0]
    w_in = ab_w_in[0].astype(BF16)
    scale = NAT_HEAD_DIM ** -0.5
    qg = jnp.tile(nat_q_norm[0] * scale, NAT_HEADS).reshape(1, NAT_WIDTH)
    kg = jnp.tile(nat_k_norm[0], NAT_HEADS).reshape(1, NAT_WIDTH)
    head_of = np.arange(NAT_WIDTH) // NAT_HEAD_DIM
    bd = jnp.asarray(head_of[:, None] == head_of[None, :], BF16)
    lbf, lbb = lb_all[0].reshape(1, HGRN_WIDTH), lb_all[1].reshape(1, HGRN_WIDTH)
    cos, sin = _rope_tables(s)
    g1 = norm1_g[0].reshape(1, d)
    lat = _inproj(x, mod_vec(0, 0), mod_vec(0, 1), g1, w_in, cos, sin, qg, kg, bd, lbf, lbb, tm)
    qa, qb, k, v, hq, kf, lff, kb, lfb, hi, sg = lat
    tl = _tile(l, 256)
    cxt = _inproj(ctx, ctx_vec(0), ctx_vec(1), g1, w_in, jnp.ones((l, LANES), F32), jnp.zeros((l, LANES), F32),
                  qg, kg, bd, lbf, lbb, tl)
    _, _, kx, vx, _, kfx, lffx, kbx, lfbx, hix, _ = cxt

    nat = _nat_attention(qa, qb, k, v, kx, vx, _nat_bias_table(nat_rpb[0]), rows)

    tri_f = jnp.asarray(np.tril(np.ones((HGRN_CHUNK, HGRN_CHUNK), np.float32)))
    s0 = jnp.zeros((b, HGRN_HEADS, HGRN_DIM, HGRN_DIM), F32)
    _, sf = _hgrn_scan(None, kfx, lffx, hix, s0, tri_f, False, False)
    of, _ = _hgrn_scan(hq, kf, lff, hi, sf, tri_f, False, True)
    _, sb = _hgrn_scan(None, kbx, lfbx, hix, s0, tri_f.T, True, False)
    ob, _ = _hgrn_scan(hq, kb, lfb, hi, sb, tri_f.T, True, True)

    h1, xm, pos, wgt, counts = _outproj_route(
        nat, of, ob, sg, x, mod_vec(0, 2), hgrn_o_norm[0].reshape(1, HGRN_DIM), ab_w_out[0].astype(BF16),
        tail_args(0), tm, td)
    h = _moe(h1, xm, pos, wgt, counts, mod_vec(0, 5), 0, moe_w1, moe_b1, moe_w2, moe_b2, tme, td)

    u = _conf_in(h, mod_vec(1, 0), mod_vec(1, 1), norm1_g[1].reshape(1, d), conv_w1[0].astype(BF16),
                 conv_b1[0].reshape(1, 2 * d), tm)
    h1, xm, pos, wgt, counts = _conf_out_route(
        u, conv_dw[0], conv_dw_b[0].reshape(1, d), conv_ln_g[0].reshape(1, d), conv_ln_b[0].reshape(1, d),
        conv_w2[0].astype(BF16), conv_b2[0].reshape(1, d), h, mod_vec(1, 2), tail_args(1), tm, td)
    return _moe(h1, xm, pos, wgt, counts, mod_vec(1, 5), 1, moe_w1, moe_b1, moe_w2, moe_b2, tme, td)
```

```python
import functools

import numpy as np
import jax
import jax.numpy as jnp
from jax import lax
from jax.experimental import pallas as pl
from jax.experimental.pallas import tpu as pltpu

F32 = jnp.float32
BF16 = jnp.bfloat16
U32 = jnp.uint32
I32 = jnp.int32
HIGHEST = lax.Precision.HIGHEST

GRID_W = 64
NAT_HEADS = 8
NAT_HEAD_DIM = 64
NAT_WIDTH = NAT_HEADS * NAT_HEAD_DIM
NAT_KH = 8
NAT_KW = 16
NAT_ROWS_IN_FLIGHT = 2
HGRN_HEADS = 4
HGRN_DIM = 128
HGRN_WIDTH = HGRN_HEADS * HGRN_DIM
HGRN_CHUNK = 64
HGRN_SUB = 16
CONV_WIDTH = 31
CONV_HALO = 16
N_EXPERTS = 32
TOP_K = 4
SWIGLU_LIMIT = 7.0
SWIGLU_ALPHA = 1.702
ROPE_BASE = 10000.0
NORM_EPS = 1e-6
MASK_VALUE = -1e30
EXP_CLAMP = 80.0

SUBLANES = 8
ROW_ALIGN = SUBLANES
RUN_CHUNK = 64
LANES = 128
MOD_ROWS = 16
VMEM_LIMIT = 56 * 1024 * 1024
NT = (((1,), (1,)), ((), ()))


def _params(sem, vmem=None):
    return pltpu.CompilerParams(dimension_semantics=sem, vmem_limit_bytes=vmem)


def _dot(a, b):
    return jnp.dot(a, b, preferred_element_type=F32)


def _dot_nt(a, b):
    return lax.dot_general(a, b, NT, preferred_element_type=F32)


def _sigmoid(x):
    return 0.5 * jnp.tanh(0.5 * x) + 0.5


def _silu(x):
    return x * _sigmoid(x)


def _rms(x, g):
    return x * lax.rsqrt(jnp.mean(x * x, axis=-1, keepdims=True) + NORM_EPS) * g


def _mod_kernel(cc_ref, w_ref, b_ref, o_ref):
    cc = cc_ref[...]
    o_ref[0] = jnp.dot(_silu(cc), w_ref[0], precision=HIGHEST, preferred_element_type=F32) + b_ref[0]


def _modulation(cc, ada_w, ada_b):
    depth, d, n = ada_w.shape
    tn = n // 4
    return pl.pallas_call(
        _mod_kernel,
        grid=(depth, n // tn),
        in_specs=[pl.BlockSpec((MOD_ROWS, d), lambda l, j: (0, 0)),
                  pl.BlockSpec((1, d, tn), lambda l, j: (l, 0, j)),
                  pl.BlockSpec((1, 1, tn), lambda l, j: (l, 0, j))],
        out_specs=pl.BlockSpec((1, MOD_ROWS, tn), lambda l, j: (l, 0, j)),
        out_shape=jax.ShapeDtypeStruct((depth, MOD_ROWS, n), F32),
        compiler_params=_params(("arbitrary", "arbitrary")),
    )(cc, ada_w, ada_b.reshape(depth, 1, n))


def _inproj_kernel(x_ref, sh_ref, sc_ref, g_ref, w_ref, cos_ref, sin_ref, qg_ref, kg_ref, bd_ref, lbf_ref, lbb_ref,
                   qa_ref, qb_ref, k_ref, v_ref, hq_ref, kf_ref, lff_ref, kb_ref, lfb_ref, hi_ref, sg_ref):
    x = x_ref[0]
    xm = (_rms(x, g_ref[...]) * (1.0 + sc_ref[0]) + sh_ref[0]).astype(BF16)
    wd = NAT_WIDTH

    def proj(i):
        return _dot(xm, w_ref[:, i * wd:(i + 1) * wd])

    def head_norm(y, g):
        ss = _dot((y * y).astype(BF16), bd_ref[...])
        return y * lax.rsqrt(ss * (1.0 / NAT_HEAD_DIM) + NORM_EPS) * g

    cos = cos_ref[...]
    sin = sin_ref[...]
    lane = lax.broadcasted_iota(I32, cos.shape, 1)
    first = (lane % 32) < 16

    def rope(y):
        outs = []
        for gi in range(wd // LANES):
            yg = y[:, gi * LANES:(gi + 1) * LANES]
            partner = jnp.where(first, pltpu.roll(yg, LANES - 16, 1), pltpu.roll(yg, 16, 1))
            outs.append(yg * cos + partner * sin)
        return jnp.concatenate(outs, axis=-1)

    qf = head_norm(proj(0), qg_ref[...])
    qa_ref[0] = rope(qf).astype(BF16)
    qb_ref[0] = qf.astype(BF16)
    k_ref[0] = rope(head_norm(proj(1), kg_ref[...])).astype(BF16)
    v_ref[0] = proj(2).astype(BF16)
    hq_ref[0] = (_silu(proj(3)) * (HGRN_DIM ** -0.5)).astype(BF16)
    for i, lb_ref, kk_ref, lf_ref in ((4, lbf_ref, kf_ref, lff_ref), (5, lbb_ref, kb_ref, lfb_ref)):
        lb = lb_ref[...]
        f = lb + (1.0 - lb) * _sigmoid(proj(i))
        kk_ref[0] = (1.0 - f).astype(BF16)
        lf_ref[0] = jnp.log(f)
    hi_ref[0] = proj(6).astype(BF16)
    sg_ref[0] = _silu(proj(7)).astype(BF16)


def _inproj(x, shift, scale, g, w_bf, cos, sin, qg, kg, bd, lbf, lbb, tm):
    b, t, d = x.shape
    wd = NAT_WIDTH
    tok = lambda bi, ti: (bi, ti, 0)
    vec = lambda bi, ti: (bi, 0, 0)
    const = lambda bi, ti: (0, 0)
    out_dtypes = (BF16, BF16, BF16, BF16, BF16, BF16, F32, BF16, F32, BF16, BF16)
    return pl.pallas_call(
        _inproj_kernel,
        grid=(b, t // tm),
        in_specs=[pl.BlockSpec((1, tm, d), tok), pl.BlockSpec((1, 1, d), vec), pl.BlockSpec((1, 1, d), vec),
                  pl.BlockSpec((1, d), const), pl.BlockSpec(w_bf.shape, const),
                  pl.BlockSpec((tm, LANES), lambda bi, ti: (ti, 0)), pl.BlockSpec((tm, LANES), lambda bi, ti: (ti, 0)),
                  pl.BlockSpec((1, wd), const), pl.BlockSpec((1, wd), const), pl.BlockSpec((wd, wd), const),
                  pl.BlockSpec((1, wd), const), pl.BlockSpec((1, wd), const)],
        out_specs=[pl.BlockSpec((1, tm, wd), tok)] * len(out_dtypes),
        out_shape=[jax.ShapeDtypeStruct((b, t, wd), dt) for dt in out_dtypes],
        compiler_params=_params(("arbitrary", "arbitrary"), VMEM_LIMIT),
    )(x, shift, scale, g, w_bf, cos, sin, qg, kg, bd, lbf, lbb)


def _nat_kernel(qa_ref, qb_ref, kp_ref, kc_ref, kn_ref, vp_ref, vc_ref, vn_ref, kx_ref, vx_ref, bias_ref,
                o_ref, kbuf, vbuf, s1_scr, s2_scr, p1_scr, p2_scr, *, rows, rb):
    i = pl.program_id(1)
    blk = rb * GRID_W
    for s, (kr, vr) in enumerate(((kp_ref, vp_ref), (kc_ref, vc_ref), (kn_ref, vn_ref))):
        kbuf[s * blk:(s + 1) * blk, :] = kr[0]
        vbuf[s * blk:(s + 1) * blk, :] = vr[0]
    lane = lax.broadcasted_iota(I32, (GRID_W, LANES), 1)
    low = lane < NAT_HEAD_DIM
    kwin = NAT_KH * GRID_W
    pair = 2 * GRID_W
    sm_rows = 16

    def stack_heads(q2):
        q2 = q2.astype(F32)
        return jnp.concatenate([jnp.where(low, q2, 0.0), jnp.where(low, 0.0, q2)], axis=0).astype(BF16)

    def window(j):
        r = i * rb + j
        rs = jnp.clip(r - NAT_KH // 2, 0, rows - NAT_KH)
        start = pl.multiple_of((rs - i * rb + rb) * GRID_W, GRID_W)
        return start, rs - r + NAT_KH - 1, pl.multiple_of(j * GRID_W, GRID_W)

    def scores(j, u):
        start, cls, q0 = window(j)
        for p in range(NAT_HEADS // 2):
            ls = slice(p * LANES, (p + 1) * LANES)
            bias = jnp.concatenate([bias_ref[cls, 2 * p], bias_ref[cls, 2 * p + 1]], axis=0)
            s1_scr[u, p * pair:(p + 1) * pair, :] = _dot_nt(stack_heads(qa_ref[0, pl.ds(q0, GRID_W), ls]),
                                                            kbuf[pl.ds(start, kwin), ls]) + bias
            s2_scr[u, p * pair:(p + 1) * pair, :] = _dot_nt(stack_heads(qb_ref[0, pl.ds(q0, GRID_W), ls]),
                                                            kx_ref[0, :, ls])

    def softmax(u):
        for c in range(NAT_HEADS * GRID_W // sm_rows):
            rsl = slice(c * sm_rows, (c + 1) * sm_rows)
            a = s1_scr[u, rsl, :]
            bb = s2_scr[u, rsl, :]
            m = jnp.maximum(jnp.max(a, axis=-1, keepdims=True), jnp.max(bb, axis=-1, keepdims=True))
            ea = jnp.exp(a - m)
            eb = jnp.exp(bb - m)
            inv = 1.0 / (jnp.sum(ea, axis=-1, keepdims=True) + jnp.sum(eb, axis=-1, keepdims=True))
            p1_scr[u, rsl, :] = (ea * inv).astype(BF16)
            p2_scr[u, rsl, :] = (eb * inv).astype(BF16)

    def values(j, u):
        start, _, q0 = window(j)
        for p in range(NAT_HEADS // 2):
            ls = slice(p * LANES, (p + 1) * LANES)
            o = (_dot(p1_scr[u, p * pair:(p + 1) * pair, :], vbuf[pl.ds(start, kwin), ls])
                 + _dot(p2_scr[u, p * pair:(p + 1) * pair, :], vx_ref[0, :, ls]))
            o_ref[0, pl.ds(q0, GRID_W), ls] = jnp.where(low, o[:GRID_W], o[GRID_W:]).astype(BF16)

    nset = s1_scr.shape[0]

    def row_group(jg, carry):
        for u in range(nset):
            scores(nset * jg + u, u)
        for u in range(nset):
            softmax(u)
        for u in range(nset):
            values(nset * jg + u, u)
        return carry

    lax.fori_loop(0, rb // nset, row_group, 0)


def _nat_attention(qa, qb, k, v, kx, vx, bias, rows):
    b, s, wd = qa.shape
    lx = kx.shape[1]
    rb = NAT_KH
    blk = rb * GRID_W
    nb = rows // rb
    cur = lambda bi, i: (bi, i, 0)
    prv = lambda bi, i: (bi, jnp.maximum(i - 1, 0), 0)
    nxt = lambda bi, i: (bi, jnp.minimum(i + 1, nb - 1), 0)
    ctx = lambda bi, i: (bi, 0, 0)
    tile = lambda im: pl.BlockSpec((1, blk, wd), im)
    return pl.pallas_call(
        functools.partial(_nat_kernel, rows=rows, rb=rb),
        grid=(b, nb),
        in_specs=[tile(cur), tile(cur), tile(prv), tile(cur), tile(nxt), tile(prv), tile(cur), tile(nxt),
                  pl.BlockSpec((1,) + kx.shape[1:], ctx), pl.BlockSpec((1,) + vx.shape[1:], ctx),
                  pl.BlockSpec(bias.shape, lambda bi, i: (0, 0, 0, 0))],
        out_specs=tile(cur),
        out_shape=jax.ShapeDtypeStruct((b, s, wd), BF16),
        scratch_shapes=[pltpu.VMEM((3 * blk, wd), BF16), pltpu.VMEM((3 * blk, wd), BF16),
                        pltpu.VMEM((NAT_ROWS_IN_FLIGHT, NAT_HEADS * GRID_W, NAT_KH * GRID_W), F32),
                        pltpu.VMEM((NAT_ROWS_IN_FLIGHT, NAT_HEADS * GRID_W, lx), F32),
                        pltpu.VMEM((NAT_ROWS_IN_FLIGHT, NAT_HEADS * GRID_W, NAT_KH * GRID_W), BF16),
                        pltpu.VMEM((NAT_ROWS_IN_FLIGHT, NAT_HEADS * GRID_W, lx), BF16)],
        compiler_params=_params(("arbitrary", "arbitrary"), VMEM_LIMIT),
    )(qa, qb, k, k, k, v, v, v, kx, vx, bias)


def _nat_bias_table(rpb):
    qc = np.arange(GRID_W)
    wc = np.clip(qc - NAT_KW // 2, 0, GRID_W - NAT_KW)
    kc = np.arange(GRID_W)
    valid = (kc[None, :] >= wc[:, None]) & (kc[None, :] < wc[:, None] + NAT_KW)
    cidx = np.clip(kc[None, :] - qc[:, None] + NAT_KW - 1, 0, 2 * NAT_KW - 2)
    cols = jnp.where(valid[None, None], rpb.astype(F32)[:, :, cidx], MASK_VALUE)
    tab = jnp.stack([cols[:, c:c + NAT_KH] for c in range(NAT_KH)], axis=0)
    tab = tab.transpose(0, 1, 3, 2, 4)
    return tab.reshape(NAT_KH, NAT_HEADS, GRID_W, NAT_KH * GRID_W)


def _hgrn_kernel(*refs, rev, with_out, cb):
    if with_out:
        q_ref, kk_ref, lf_ref, v_ref, s0_ref, tri_ref, o_ref, sf_ref, st = refs
    else:
        kk_ref, lf_ref, v_ref, s0_ref, tri_ref, sf_ref, st = refs
    i = pl.program_id(1)
    c_sz = HGRN_CHUNK

    @pl.when(i == 0)
    def _():
        st[...] = s0_ref[0]

    tri = tri_ref[...]
    tri_b = tri.astype(BF16)
    keep = tri > 0.5

    def chunk(ci, carry):
        c = (cb - 1 - ci) if rev else ci
        r0 = pl.multiple_of(c * c_sz, c_sz)
        lf = lf_ref[0, pl.ds(r0, c_sz), :]
        lf_hi = lf.astype(BF16)
        rest = lf - lf_hi.astype(F32)
        lf_mid = rest.astype(BF16)
        lf_lo = (rest - lf_mid.astype(F32)).astype(BF16)
        bsum = _dot(tri_b, lf_hi) + (_dot(tri_b, lf_mid) + _dot(tri_b, lf_lo))
        for h in range(HGRN_HEADS):
            ls = slice(h * HGRN_DIM, (h + 1) * HGRN_DIM)
            bh = bsum[:, ls]
            tot = bh[0:1] if rev else bh[c_sz - 1:c_sz]
            kh = kk_ref[0, pl.ds(r0, c_sz), ls].astype(F32)
            vh = v_ref[0, pl.ds(r0, c_sz), ls]
            state = st[h]
            if with_out:
                qh = q_ref[0, pl.ds(r0, c_sz), ls].astype(F32)
                blocks = []
                for sb in range(c_sz // HGRN_SUB):
                    lo = sb * HGRN_SUB
                    ref_row = lo + HGRN_SUB - 1 if rev else lo
                    cref = bh[ref_row:ref_row + 1]
                    f1 = jnp.exp(bh[lo:lo + HGRN_SUB] - cref)
                    f2 = jnp.exp(jnp.minimum(cref - bh, EXP_CLAMP))
                    blocks.append(_dot_nt((qh[lo:lo + HGRN_SUB] * f1).astype(BF16), (kh * f2).astype(BF16)))
                scores = jnp.where(keep, jnp.concatenate(blocks, axis=0), 0.0)
                o = _dot(scores.astype(BF16), vh) + _dot_nt((qh * jnp.exp(bh)).astype(BF16), state.astype(BF16))
                o_ref[0, pl.ds(r0, c_sz), ls] = o.astype(BF16)
            kd = (kh * jnp.exp(tot - bh)).astype(BF16)
            vt = vh.astype(F32).T.astype(BF16)
            st[h] = state * jnp.exp(tot) + _dot(vt, kd)
        return carry

    lax.fori_loop(0, cb, chunk, 0, unroll=True)

    @pl.when(i == pl.num_programs(1) - 1)
    def _():
        sf_ref[0] = st[...]


def _hgrn_scan(q, kk, lf, v, s0, tri, rev, with_out):
    b, t, wd = kk.shape
    nchunk = t // HGRN_CHUNK
    cb = min(8, nchunk)
    nblk = nchunk // cb
    tb = cb * HGRN_CHUNK
    tok = (lambda bi, i: (bi, nblk - 1 - i, 0)) if rev else (lambda bi, i: (bi, i, 0))
    st_map = lambda bi, i: (bi, 0, 0, 0)
    st_shape = (b, HGRN_HEADS, HGRN_DIM, HGRN_DIM)
    tile = pl.BlockSpec((1, tb, wd), tok)
    st_spec = pl.BlockSpec((1,) + st_shape[1:], st_map)
    ins = ([q] if with_out else []) + [kk, lf, v, s0, tri]
    in_specs = [tile] * (len(ins) - 2) + [st_spec, pl.BlockSpec(tri.shape, lambda bi, i: (0, 0))]
    out_specs = ([tile] if with_out else []) + [st_spec]
    out_shape = ([jax.ShapeDtypeStruct((b, t, wd), BF16)] if with_out else []) + [jax.ShapeDtypeStruct(st_shape, F32)]
    res = pl.pallas_call(
        functools.partial(_hgrn_kernel, rev=rev, with_out=with_out, cb=cb),
        grid=(b, nblk),
        in_specs=in_specs, out_specs=out_specs, out_shape=out_shape,
        scratch_shapes=[pltpu.VMEM(st_shape[1:], F32)],
        compiler_params=_params(("arbitrary", "arbitrary"), VMEM_LIMIT),
    )(*ins)
    return res if with_out else (None, res[0])


def _pack_bf16_pairs(x):
    half = x.shape[-1] // 2
    lo = pltpu.bitcast(x[:, :half].astype(BF16).astype(F32), U32) >> 16
    hi = pltpu.bitcast(x[:, half:].astype(BF16).astype(F32), U32) & jnp.uint32(0xFFFF0000)
    return hi | lo


def _unpack_bf16_pairs(w):
    lo = pltpu.bitcast(w << 16, F32)
    hi = pltpu.bitcast(w & jnp.uint32(0xFFFF0000), F32)
    return lo, hi


def _route_tail(h1, g2_ref, sh2_ref, sc2_ref, rwt_ref, rb_ref, su_ref,
                h1_ref, xm_ref, pos_ref, wgt_ref, cnt_ref, *, td):
    h1_ref[0] = h1
    xm2 = _rms(h1, g2_ref[...]) * (1.0 + sc2_ref[0]) + sh2_ref[0]
    xm_hi = xm2.astype(BF16)
    xm_ref[...] = xm_hi
    xm_lo = (xm2 - xm_hi.astype(F32)).astype(BF16)
    ne = rb_ref.shape[0]
    part = _dot_nt(rwt_ref[...], xm_hi)
    logits = part[:ne] + part[ne:] + _dot_nt(rwt_ref[:ne, :], xm_lo) + rb_ref[...]
    tm = logits.shape[1]
    eid = lax.broadcasted_iota(I32, (ne, tm), 0).astype(F32)
    vals, hots = [], []
    for k in range(TOP_K):
        m = jnp.max(logits, axis=0, keepdims=True)
        sel = jnp.min(jnp.where(logits == m, eid, float(ne)), axis=0, keepdims=True)
        hot = eid == sel
        logits = jnp.where(hot, -jnp.inf, logits)
        vals.append(m)
        hots.append(hot)
    es = [jnp.exp(vv - vals[0]) for vv in vals]
    den = es[0] + es[1] + es[2] + es[3]
    onehot = jnp.zeros((ne, tm), F32)
    for k in range(TOP_K):
        wgt_ref[k:k + 1, :] = es[k] / den
        onehot = onehot + hots[k].astype(F32)
    prefix = _dot(onehot.astype(BF16), su_ref[...])
    strict_lower = (lax.broadcasted_iota(I32, (ne, ne), 0) > lax.broadcasted_iota(I32, (ne, ne), 1)).astype(BF16)
    lane = lax.broadcasted_iota(I32, (ne, LANES), 1)
    counts = jnp.zeros((ne, LANES), F32)
    starts = []
    for j in range(tm // td):
        cj = jnp.sum(onehot[:, j * td:(j + 1) * td], axis=1, keepdims=True)
        padded = jnp.floor((cj + (ROW_ALIGN - 1)) * (1.0 / ROW_ALIGN)) * ROW_ALIGN
        group_start = _dot(strict_lower, jnp.broadcast_to(padded, (ne, LANES)).astype(BF16))
        starts.append(jnp.broadcast_to(group_start[:, 0:1], (ne, td)))
        counts = counts + jnp.where(lane == j, cj, 0.0)
    row = prefix + jnp.concatenate(starts, axis=1)
    for k in range(TOP_K):
        pos_ref[k:k + 1, :] = jnp.sum(jnp.where(hots[k], row, 0.0), axis=0, keepdims=True).astype(I32)
    cnt_ref[0] = counts


def _tail_specs(b, s, d, tm):
    nt = s // tm
    n = b * s
    vec = lambda bi, ti: (bi, 0, 0)
    const = lambda bi, ti: (0, 0)
    in_specs = [pl.BlockSpec((1, d), const), pl.BlockSpec((1, 1, d), vec), pl.BlockSpec((1, 1, d), vec),
                pl.BlockSpec((2 * N_EXPERTS, d), const), pl.BlockSpec((N_EXPERTS, 1), const),
                pl.BlockSpec((tm, tm), const)]
    flat = lambda bi, ti: (0, bi * nt + ti)
    out_specs = [pl.BlockSpec((1, tm, d), lambda bi, ti: (bi, ti, 0)),
                 pl.BlockSpec((tm, d), lambda bi, ti: (bi * nt + ti, 0)),
                 pl.BlockSpec((TOP_K, tm), flat), pl.BlockSpec((TOP_K, tm), flat),
                 pl.BlockSpec((1, N_EXPERTS, LANES), lambda bi, ti: (bi * nt + ti, 0, 0))]
    out_shape = [jax.ShapeDtypeStruct((b, s, d), F32), jax.ShapeDtypeStruct((n, d), BF16),
                 jax.ShapeDtypeStruct((TOP_K, n), I32), jax.ShapeDtypeStruct((TOP_K, n), F32),
                 jax.ShapeDtypeStruct((b * nt, N_EXPERTS, LANES), F32)]
    return in_specs, out_specs, out_shape


def _outproj_kernel(nat_ref, of_ref, ob_ref, sg_ref, x_ref, gate_ref, on_ref, w_ref,
                    g2_ref, sh2_ref, sc2_ref, rwt_ref, rb_ref, su_ref,
                    h1_ref, xm_ref, pos_ref, wgt_ref, cnt_ref, *, td):
    o = of_ref[0].astype(F32) + ob_ref[0].astype(F32)
    gated = []
    for h in range(HGRN_HEADS):
        ls = slice(h * HGRN_DIM, (h + 1) * HGRN_DIM)
        gated.append(_rms(o[:, ls], on_ref[...]) * sg_ref[0, :, ls].astype(F32))
    gated = jnp.concatenate(gated, axis=-1).astype(BF16)
    y = _dot(nat_ref[0], w_ref[:NAT_WIDTH, :]) + _dot(gated, w_ref[NAT_WIDTH:, :])
    h1 = x_ref[0] + gate_ref[0] * y
    _route_tail(h1, g2_ref, sh2_ref, sc2_ref, rwt_ref, rb_ref, su_ref,
                h1_ref, xm_ref, pos_ref, wgt_ref, cnt_ref, td=td)


def _outproj_route(nat, of, ob, sg, x, gate, on_g, w_bf, tail_args, tm, td):
    b, s, d = x.shape
    tok = lambda bi, ti: (bi, ti, 0)
    vec = lambda bi, ti: (bi, 0, 0)
    const = lambda bi, ti: (0, 0)
    t_in, out_specs, out_shape = _tail_specs(b, s, d, tm)
    wide = pl.BlockSpec((1, tm, NAT_WIDTH), tok)
    return pl.pallas_call(
        functools.partial(_outproj_kernel, td=td),
        grid=(b, s // tm),
        in_specs=[wide, wide, wide, wide, pl.BlockSpec((1, tm, d), tok), pl.BlockSpec((1, 1, d), vec),
                  pl.BlockSpec((1, HGRN_DIM), const), pl.BlockSpec(w_bf.shape, const)] + t_in,
        out_specs=out_specs, out_shape=out_shape,
        compiler_params=_params(("arbitrary", "arbitrary"), VMEM_LIMIT),
    )(nat, of, ob, sg, x, gate, on_g, w_bf, *tail_args)


def _conf_in_kernel(x_ref, sh_ref, sc_ref, g_ref, w_ref, b_ref, u_ref):
    d = x_ref.shape[-1]
    xm = (_rms(x_ref[0], g_ref[...]) * (1.0 + sc_ref[0]) + sh_ref[0]).astype(BF16)
    a = _dot(xm, w_ref[:, :d]) + b_ref[:, :d]
    gate = _dot(xm, w_ref[:, d:]) + b_ref[:, d:]
    u_ref[0] = (a * _sigmoid(gate)).astype(BF16)


def _conf_in(x, shift, scale, g, w_bf, b1, tm):
    b, s, d = x.shape
    tok = lambda bi, ti: (bi, ti, 0)
    vec = lambda bi, ti: (bi, 0, 0)
    const = lambda bi, ti: (0, 0)
    return pl.pallas_call(
        _conf_in_kernel,
        grid=(b, s // tm),
        in_specs=[pl.BlockSpec((1, tm, d), tok), pl.BlockSpec((1, 1, d), vec), pl.BlockSpec((1, 1, d), vec),
                  pl.BlockSpec((1, d), const), pl.BlockSpec(w_bf.shape, const), pl.BlockSpec((1, 2 * d), const)],
        out_specs=pl.BlockSpec((1, tm, d), tok),
        out_shape=jax.ShapeDtypeStruct((b, s, d), BF16),
        compiler_params=_params(("arbitrary", "arbitrary"), VMEM_LIMIT),
    )(x, shift, scale, g, w_bf, b1)


def _conf_out_kernel(up_ref, uc_ref, un_ref, dw_ref, dwb_ref, lg_ref, lb_ref, w_ref, b2_ref, x_ref, gate_ref,
                     g2_ref, sh2_ref, sc2_ref, rwt_ref, rb_ref, su_ref,
                     h1_ref, xm_ref, pos_ref, wgt_ref, cnt_ref, ubuf, cbuf, sbuf, *, rc, td):
    ti = pl.program_id(1)
    tm, d = uc_ref.shape[1], uc_ref.shape[2]
    hal = CONV_HALO
    prev = up_ref[0].astype(F32)
    nxt = un_ref[0].astype(F32)
    ubuf[0:hal, :] = jnp.where(ti > 0, prev, jnp.zeros_like(prev))
    ubuf[hal:hal + tm, :] = uc_ref[0].astype(F32)
    ubuf[hal + tm:, :] = jnp.where(ti < pl.num_programs(1) - 1, nxt, jnp.zeros_like(nxt))
    base = hal - CONV_WIDTH // 2
    lc = sbuf.shape[2]
    sub = sbuf.shape[0] + 1
    srows = sbuf.shape[1]
    bc = 64

    def shift_chunk(r0, nrows, l0):
        win = ubuf[pl.ds(r0, nrows + sub), l0:l0 + lc]
        for r in range(1, sub):
            sbuf[r - 1, pl.ds(r0, nrows), :] = win[r:r + nrows]

    for l0 in range(0, d, lc):
        def shift_rows(ci, carry, l0=l0):
            shift_chunk(pl.multiple_of(ci * bc, bc), bc, l0)
            return carry

        lax.fori_loop(0, srows // bc, shift_rows, 0)
        if srows % bc:
            shift_chunk(srows // bc * bc, srows % bc, l0)

        def conv_rows(ci, carry, l0=l0):
            r0 = pl.multiple_of(ci * rc, rc)
            acc = jnp.zeros((rc, lc), F32) + dwb_ref[:, l0:l0 + lc]
            for j in range(CONV_WIDTH):
                r = (base + j) % sub
                a = pl.multiple_of(r0 + (base + j - r), sub)
                src = ubuf[pl.ds(a, rc), l0:l0 + lc] if r == 0 else sbuf[r - 1, pl.ds(a, rc), :]
                acc = acc + src * dw_ref[j:j + 1, l0:l0 + lc]
            cbuf[pl.ds(r0, rc), l0:l0 + lc] = acc
            return carry

        lax.fori_loop(0, tm // rc, conv_rows, 0)
    c = cbuf[...]
    mu = jnp.mean(c, axis=-1, keepdims=True)
    cz = c - mu
    var = jnp.mean(cz * cz, axis=-1, keepdims=True)
    y = _silu(cz * lax.rsqrt(var + NORM_EPS) * lg_ref[...] + lb_ref[...]).astype(BF16)
    y = _dot(y, w_ref[...]) + b2_ref[...]
    h1 = x_ref[0] + gate_ref[0] * y
    _route_tail(h1, g2_ref, sh2_ref, sc2_ref, rwt_ref, rb_ref, su_ref,
                h1_ref, xm_ref, pos_ref, wgt_ref, cnt_ref, td=td)


def _conf_out_route(u, dw, dwb, ln_g, ln_b, w_bf, b2, x, gate, tail_args, tm, td):
    b, s, d = x.shape
    hal = CONV_HALO
    per = tm // hal
    nh = s // hal
    tok = lambda bi, ti: (bi, ti, 0)
    vec = lambda bi, ti: (bi, 0, 0)
    const = lambda bi, ti: (0, 0)
    t_in, out_specs, out_shape = _tail_specs(b, s, d, tm)
    return pl.pallas_call(
        functools.partial(_conf_out_kernel, rc=64, td=td),
        grid=(b, s // tm),
        in_specs=[pl.BlockSpec((1, hal, d), lambda bi, ti: (bi, jnp.maximum(ti * per - 1, 0), 0)),
                  pl.BlockSpec((1, tm, d), tok),
                  pl.BlockSpec((1, hal, d), lambda bi, ti: (bi, jnp.minimum((ti + 1) * per, nh - 1), 0)),
                  pl.BlockSpec(dw.shape, const), pl.BlockSpec((1, d), const), pl.BlockSpec((1, d), const),
                  pl.BlockSpec((1, d), const), pl.BlockSpec(w_bf.shape, const), pl.BlockSpec((1, d), const),
                  pl.BlockSpec((1, tm, d), tok), pl.BlockSpec((1, 1, d), vec)] + t_in,
        out_specs=out_specs, out_shape=out_shape,
        scratch_shapes=[pltpu.VMEM((tm + 2 * hal, d), F32), pltpu.VMEM((tm, d), F32),
                        pltpu.VMEM((SUBLANES - 1, tm + (CONV_WIDTH // SUBLANES) * SUBLANES, min(d, 2 * LANES)), F32)],
        compiler_params=_params(("arbitrary", "arbitrary"), VMEM_LIMIT),
    )(u, u, u, dw, dwb, ln_g, ln_b, w_bf, b2, x, gate, *tail_args)


def _copy_rows(src_ref, src0, dst_ref, dst0, length, sem, *, wait, src_fixed=False):
    def piece(off, size):
        s = src0 if src_fixed else pl.multiple_of(src0 + off, ROW_ALIGN)
        return pltpu.make_async_copy(src_ref.at[pl.ds(s, size), :],
                                     dst_ref.at[pl.ds(pl.multiple_of(dst0 + off, ROW_ALIGN), size), :], sem)

    def go(copy):
        if wait:
            copy.wait()
        else:
            copy.start()

    def big(c, carry):
        go(piece(c * RUN_CHUNK, RUN_CHUNK))
        return carry

    lax.fori_loop(0, lax.shift_right_logical(length, RUN_CHUNK.bit_length() - 1), big, 0)
    size = RUN_CHUNK // 2
    while size >= ROW_ALIGN:
        @pl.when((length & size) != 0)
        def _(size=size):
            go(piece(length & ~(2 * size - 1), size))
        size //= 2


def _wait_rows(src_ref, dst_ref, total, max_rows, sem):
    size = pl.next_power_of_2(max_rows)
    while size >= ROW_ALIGN:
        if size <= max_rows:
            @pl.when((total & size) != 0)
            def _(size=size):
                pltpu.make_async_copy(src_ref.at[pl.ds(0, size), :], dst_ref.at[pl.ds(0, size), :], sem).wait()
        size //= 2


def _moe_dispatch_kernel(gap_ref, prev_ref, meta_ref, x_ref, pos_ref, xs_ref, ybuf, zbuf, sems):
    i = pl.program_id(0)
    last = pl.num_programs(0) - 1
    slot = i % 2
    cap, td = ybuf.shape[1], x_ref.shape[0]
    pos = pos_ref[...]
    rid = lax.broadcasted_iota(I32, (cap, td), 0)
    hit = rid == pos[0:1]
    for k in range(1, TOP_K):
        hit = jnp.logical_or(hit, rid == pos[k:k + 1])
    ybuf[slot] = _pack_bf16_pairs(_dot(jnp.where(hit, 1.0, 0.0).astype(BF16), x_ref[...]))

    def wait_runs(m_ref, buf_slot):
        _wait_rows(ybuf.at[buf_slot], xs_ref, m_ref[0, 3, 0], cap, sems.at[buf_slot])

    @pl.when(i > 0)
    def _():
        wait_runs(prev_ref, 1 - slot)

    def run(e, carry):
        _copy_rows(ybuf.at[slot], meta_ref[0, 0, e], xs_ref, meta_ref[0, 2, e], meta_ref[0, 1, e], sems.at[slot],
                   wait=False)
        return carry

    lax.fori_loop(0, N_EXPERTS, run, 0, unroll=4)

    @pl.when(i == last)
    def _():
        wait_runs(meta_ref, slot)
        zbuf[...] = jnp.zeros_like(zbuf)
        for wait in (False, True):
            def fill(e, carry, wait=wait):
                _copy_rows(zbuf, 0, xs_ref, gap_ref[e], gap_ref[N_EXPERTS + e], sems.at[slot], wait=wait, src_fixed=True)
                return carry
            lax.fori_loop(0, N_EXPERTS, fill, 0)


def _moe_dispatch(gap, meta, xm, pos, p, td):
    n, d = xm.shape
    cap = TOP_K * td + N_EXPERTS * ROW_ALIGN
    meta_spec = lambda im: pl.BlockSpec((1, 4, N_EXPERTS), im, memory_space=pltpu.SMEM)
    grid_spec = pltpu.PrefetchScalarGridSpec(
        num_scalar_prefetch=1,
        grid=(n // td,),
        in_specs=[meta_spec(lambda i, g: (jnp.maximum(i - 1, 0), 0, 0)), meta_spec(lambda i, g: (i, 0, 0)),
                  pl.BlockSpec((td, d), lambda i, g: (i, 0)),
                  pl.BlockSpec((TOP_K, td), lambda i, g: (0, i))],
        out_specs=pl.BlockSpec(memory_space=pl.ANY),
        scratch_shapes=[pltpu.VMEM((2, cap, d // 2), U32), pltpu.VMEM((RUN_CHUNK, d // 2), U32),
                        pltpu.SemaphoreType.DMA((2,))],
    )
    return pl.pallas_call(
        _moe_dispatch_kernel,
        grid_spec=grid_spec,
        out_shape=jax.ShapeDtypeStruct((p, d // 2), U32),
        compiler_params=_params(("arbitrary",), VMEM_LIMIT),
    )(gap, meta, meta, xm, pos)


def _moe_expert_kernel(te_ref, nv_ref, xs_ref, w1_ref, b1_ref, w2_ref, b2_ref, ys_ref, w1b, w2b):
    i = pl.program_id(0)
    f = w2_ref.shape[2]
    changed = jnp.logical_or(i == 0, te_ref[i] != te_ref[jnp.maximum(i - 1, 0)])

    @pl.when(jnp.logical_and(changed, i < nv_ref[0]))
    def _():
        w1b[...] = w1_ref[0, 0].astype(BF16)
        w2b[...] = w2_ref[0, 0].astype(BF16)

    @pl.when(i < nv_ref[0])
    def _():
        lo, hi = _unpack_bf16_pairs(xs_ref[...])
        x = jnp.concatenate([lo, hi], axis=-1).astype(BF16)
        y = jnp.zeros((x.shape[0], w2_ref.shape[3]), F32) + b2_ref[0, 0]
        fc = f
        for c in range(f // fc):
            glu = _dot(x, w1b[:, c * fc:(c + 1) * fc]) + b1_ref[0, 0, :, c * fc:(c + 1) * fc]
            lin = _dot(x, w1b[:, f + c * fc:f + (c + 1) * fc]) + b1_ref[0, 0, :, f + c * fc:f + (c + 1) * fc]
            glu = jnp.minimum(glu, SWIGLU_LIMIT)
            lin = jnp.clip(lin, -SWIGLU_LIMIT, SWIGLU_LIMIT)
            act = glu * _sigmoid(SWIGLU_ALPHA * glu) * (lin + 1.0)
            y = y + _dot(act.astype(BF16), w2b[c * fc:(c + 1) * fc, :])
        ys_ref[...] = _pack_bf16_pairs(y)


def _moe_experts(tile_expert, n_valid, xs, layer, w1, b1, w2, b2, tme):
    p, d2 = xs.shape
    _, ne, d, f2 = w1.shape
    f = f2 // 2
    row = lambda i, te, nv: (jnp.minimum(i, nv[0] - 1), 0)
    wsel = lambda i, te, nv: (layer, te[jnp.minimum(i, nv[0] - 1)], 0, 0)
    grid_spec = pltpu.PrefetchScalarGridSpec(
        num_scalar_prefetch=2,
        grid=(p // tme,),
        in_specs=[pl.BlockSpec((tme, d2), row),
                  pl.BlockSpec((1, 1, d, f2), wsel), pl.BlockSpec((1, 1, 1, f2), wsel),
                  pl.BlockSpec((1, 1, f, d), wsel), pl.BlockSpec((1, 1, 1, d), wsel)],
        out_specs=pl.BlockSpec((tme, d2), row),
        scratch_shapes=[pltpu.VMEM((d, f2), BF16), pltpu.VMEM((f, d), BF16)],
    )
    depth = w1.shape[0]
    return pl.pallas_call(
        _moe_expert_kernel,
        grid_spec=grid_spec,
        out_shape=jax.ShapeDtypeStruct((p, d2), U32),
        compiler_params=_params(("arbitrary",), VMEM_LIMIT),
    )(tile_expert, n_valid, xs, w1, b1.reshape(depth, ne, 1, f2), w2, b2.reshape(depth, ne, 1, d))


def _moe_combine_kernel(meta_ref, next_ref, ys_ref, pos_ref, wt_ref, h_ref, gate_ref, o_ref, ybuf, sems):
    t = pl.program_id(0) * pl.num_programs(1) + pl.program_id(1)
    n_tiles = pl.num_programs(0) * pl.num_programs(1)
    slot = t % 2

    cap, td = ybuf.shape[1], pos_ref.shape[0]

    def fetch_runs(m_ref, buf_slot):
        def run(e, carry):
            _copy_rows(ys_ref, m_ref[0, 2, e], ybuf.at[buf_slot], m_ref[0, 0, e], m_ref[0, 1, e], sems.at[buf_slot],
                       wait=False)
            return carry
        lax.fori_loop(0, N_EXPERTS, run, 0, unroll=4)

    @pl.when(t == 0)
    def _():
        ybuf[...] = jnp.zeros_like(ybuf)
        fetch_runs(meta_ref, slot)

    @pl.when(t + 1 < n_tiles)
    def _():
        fetch_runs(next_ref, 1 - slot)

    _wait_rows(ys_ref, ybuf.at[slot], meta_ref[0, 3, 0], cap, sems.at[slot])

    pos = pos_ref[...]
    wt = wt_ref[...]
    cid = lax.broadcasted_iota(I32, (td, cap), 1)
    mix = jnp.zeros((td, cap), F32)
    for k in range(TOP_K):
        mix = mix + jnp.where(cid == pos[:, k:k + 1], wt[:, k:k + 1], 0.0)
    lo, hi = _unpack_bf16_pairs(ybuf[slot])
    y = jnp.concatenate([lo, hi], axis=-1).astype(BF16)
    o_ref[0] = h_ref[0] + gate_ref[0] * _dot(mix.astype(BF16), y)


def _moe_combine(meta, ys, pos_t, wt_t, h, gate, td):
    b, s, d = h.shape
    nt = s // td
    cap = TOP_K * td + N_EXPERTS * ROW_ALIGN
    flat = lambda bi, ti: (bi * nt + ti, 0)
    meta_spec = lambda im: pl.BlockSpec((1, 4, N_EXPERTS), im, memory_space=pltpu.SMEM)
    return pl.pallas_call(
        _moe_combine_kernel,
        grid=(b, nt),
        in_specs=[meta_spec(lambda bi, ti: (bi * nt + ti, 0, 0)),
                  meta_spec(lambda bi, ti: (jnp.minimum(bi * nt + ti + 1, b * nt - 1), 0, 0)),
                  pl.BlockSpec(memory_space=pl.ANY),
                  pl.BlockSpec((td, TOP_K), flat), pl.BlockSpec((td, TOP_K), flat),
                  pl.BlockSpec((1, td, d), lambda bi, ti: (bi, ti, 0)),
                  pl.BlockSpec((1, 1, d), lambda bi, ti: (bi, 0, 0))],
        out_specs=pl.BlockSpec((1, td, d), lambda bi, ti: (bi, ti, 0)),
        out_shape=jax.ShapeDtypeStruct((b, s, d), F32),
        scratch_shapes=[pltpu.VMEM((2, cap, d // 2), U32), pltpu.SemaphoreType.DMA((2,))],
        compiler_params=_params(("arbitrary", "arbitrary"), VMEM_LIMIT),
    )(meta, meta, ys, pos_t, wt_t, h, gate)


def _moe(h1, xm, pos, wgt, counts, gate, layer, w1, b1, w2, b2, tme, td):
    n, d = xm.shape
    per = (n // td) // counts.shape[0]
    cnt = jnp.round(counts[:, :, :per]).astype(I32).transpose(0, 2, 1).reshape(n // td, N_EXPERTS)
    run = (cnt + ROW_ALIGN - 1) // ROW_ALIGN * ROW_ALIGN
    local = jnp.cumsum(run, axis=1) - run
    tot = jnp.sum(run, axis=0)
    cap = (tot + tme - 1) // tme * tme
    ends = jnp.cumsum(cap)
    start = ends - cap
    glob = start[None, :] + jnp.cumsum(run, axis=0) - run
    rows_used = jnp.broadcast_to(jnp.sum(run, axis=1, keepdims=True), run.shape)
    meta = jnp.stack([local, run, glob, rows_used], axis=1)
    gap = jnp.concatenate([start + tot, cap - tot]).astype(I32)
    p = (TOP_K * n + N_EXPERTS * ROW_ALIGN * (n // td) + N_EXPERTS * tme) // tme * tme
    tiles = jnp.arange(p // tme, dtype=I32)
    tile_expert = jnp.minimum(jnp.sum(tiles[:, None] >= (ends // tme)[None, :], axis=1), N_EXPERTS - 1).astype(I32)
    n_valid = (ends[-1:] // tme).astype(I32)
    xs = _moe_dispatch(gap, meta, xm, pos, p, td)
    ys = _moe_experts(tile_expert, n_valid, xs, layer, w1, b1, w2, b2, tme)
    return _moe_combine(meta, ys, pos.T, wgt.T, h1, gate, td)


def _rope_tables(s):
    pos = np.arange(s)
    lane = np.arange(LANES)
    dd = lane % NAT_HEAD_DIM
    n = NAT_HEAD_DIM // 4
    inv_freq = ROPE_BASE ** (-(dd % n).astype(np.float64) / n)
    p = np.where((dd // (NAT_HEAD_DIM // 2))[None, :] == 0, (pos // GRID_W)[:, None], (pos % GRID_W)[:, None])
    ang = (p.astype(np.float32) * inv_freq.astype(np.float32)[None, :]).astype(np.float32)
    sign = np.where((dd % (2 * n)) < n, -1.0, 1.0)[None, :]
    return jnp.asarray(np.cos(ang), F32), jnp.asarray(np.sin(ang) * sign, F32)


def _tile(n, want):
    t = min(want, n)
    while n % t:
        t //= 2
    return t


def kernel(x, c, ctx, c_ctx, ada_w, ada_b, norm1_g, norm2_g, ab_w_in, ab_w_out, nat_q_norm, nat_k_norm, nat_rpb, hgrn_lb, hgrn_o_norm, conv_w1, conv_b1, conv_dw, conv_dw_b, conv_ln_g, conv_ln_b, conv_w2, conv_b2, router_w, router_b, moe_w1, moe_b1, moe_w2, moe_b2):
    b, s, d = x.shape
    l = ctx.shape[1]
    rows = s // GRID_W
    assert ada_w.shape[0] == 2 and b < MOD_ROWS and rows >= NAT_KH and rows % NAT_KH == 0
    assert s % HGRN_CHUNK == 0 and l % HGRN_CHUNK == 0 and d % (2 * LANES) == 0
    tm = _tile(s, 512)
    td = tm
    tme = 512 if TOP_K * b * s >= 512 * N_EXPERTS else 128

    cc = jnp.zeros((MOD_ROWS, d), F32).at[:b].set(c).at[b].set(c_ctx)
    mod = _modulation(cc, ada_w, ada_b)

    def mod_vec(layer, i):
        return mod[layer, :b, i * d:(i + 1) * d].reshape(b, 1, d)

    def ctx_vec(i):
        return jnp.broadcast_to(mod[0, b, i * d:(i + 1) * d].reshape(1, 1, d), (b, 1, d))

    tok = np.arange(tm)
    strict_upper = jnp.asarray((tok[:, None] < tok[None, :]) & (tok[:, None] // td == tok[None, :] // td), BF16)

    def tail_args(layer):
        rw = router_w[layer].T.astype(F32)
        rw_hi = rw.astype(BF16)
        rw_split = jnp.concatenate([rw_hi, (rw - rw_hi.astype(F32)).astype(BF16)], axis=0)
        return (norm2_g[layer].reshape(1, d), mod_vec(layer, 3), mod_vec(layer, 4),
                rw_split, router_b[layer].reshape(N_EXPERTS, 1), strict_upper)

    lb_all = jnp.cumsum(jax.nn.softmax(hgrn_lb.astype(F32), axis=1), axis=1)[:, 0]
    w_in = ab_w_in[0].astype(BF16)
    scale = NAT_HEAD_DIM ** -0.5
    qg = jnp.tile(nat_q_norm[0] * scale, NAT_HEADS).reshape(1, NAT_WIDTH)
    kg = jnp.tile(nat_k_norm[0], NAT_HEADS).reshape(1, NAT_WIDTH)
    head_of = np.arange(NAT_WIDTH) // NAT_HEAD_DIM
    bd = jnp.asarray(head_of[:, None] == head_of[None, :], BF16)
    lbf, lbb = lb_all[0].reshape(1, HGRN_WIDTH), lb_all[1].reshape(1, HGRN_WIDTH)
    cos, sin = _rope_tables(s)
    g1 = norm1_g[0].reshape(1, d)
    lat = _inproj(x, mod_vec(0, 0), mod_vec(0, 1), g1, w_in, cos, sin, qg, kg, bd, lbf, lbb, tm)
    qa, qb, k, v, hq, kf, lff, kb, lfb, hi, sg = lat
    tl = _tile(l, 256)
    cxt = _inproj(ctx, ctx_vec(0), ctx_vec(1), g1, w_in, jnp.ones((l, LANES), F32), jnp.zeros((l, LANES), F32),
                  qg, kg, bd, lbf, lbb, tl)
    _, _, kx, vx, _, kfx, lffx, kbx, lfbx, hix, _ = cxt

    nat = _nat_attention(qa, qb, k, v, kx, vx, _nat_bias_table(nat_rpb[0]), rows)

    tri_f = jnp.asarray(np.tril(np.ones((HGRN_CHUNK, HGRN_CHUNK), np.float32)))
    s0 = jnp.zeros((b, HGRN_HEADS, HGRN_DIM, HGRN_DIM), F32)
    _, sf = _hgrn_scan(None, kfx, lffx, hix, s0, tri_f, False, False)
    of, _ = _hgrn_scan(hq, kf, lff, hi, sf, tri_f, False, True)
    _, sb = _hgrn_scan(None, kbx, lfbx, hix, s0, tri_f.T, True, False)
    ob, _ = _hgrn_scan(hq, kb, lfb, hi, sb, tri_f.T, True, True)

    h1, xm, pos, wgt, counts = _outproj_route(
        nat, of, ob, sg, x, mod_vec(0, 2), hgrn_o_norm[0].reshape(1, HGRN_DIM), ab_w_out[0].astype(BF16),
        tail_args(0), tm, td)
    h = _moe(h1, xm, pos, wgt, counts, mod_vec(0, 5), 0, moe_w1, moe_b1, moe_w2, moe_b2, tme, td)

    u = _conf_in(h, mod_vec(1, 0), mod_vec(1, 1), norm1_g[1].reshape(1, d), conv_w1[0].astype(BF16),
                 conv_b1[0].reshape(1, 2 * d), tm)
    h1, xm, pos, wgt, counts = _conf_out_route(
        u, conv_dw[0], conv_dw_b[0].reshape(1, d), conv_ln_g[0].reshape(1, d), conv_ln_b[0].reshape(1, d),
        conv_w2[0].astype(BF16), conv_b2[0].reshape(1, d), h, mod_vec(1, 2), tail_args(1), tm, td)
    return _moe(h1, xm, pos, wgt, counts, mod_vec(1, 5), 1, moe_w1, moe_b1, moe_w2, moe_b2, tme, td)
```

```python
import functools

import numpy as np
import jax
import jax.numpy as jnp
from jax import lax
from jax.experimental import pallas as pl
from jax.experimental.pallas import tpu as pltpu

F32 = jnp.float32
BF16 = jnp.bfloat16
U32 = jnp.uint32
I32 = jnp.int32
HIGHEST = lax.Precision.HIGHEST

GRID_W = 64
NAT_HEADS = 8
NAT_HEAD_DIM = 64
NAT_WIDTH = NAT_HEADS * NAT_HEAD_DIM
NAT_KH = 8
NAT_KW = 16
NAT_ROWS_IN_FLIGHT = 2
HGRN_HEADS = 4
HGRN_DIM = 128
HGRN_WIDTH = HGRN_HEADS * HGRN_DIM
HGRN_CHUNK = 64
HGRN_SUB = 16
CONV_WIDTH = 31
CONV_HALO = 16
N_EXPERTS = 32
TOP_K = 4
SWIGLU_LIMIT = 7.0
SWIGLU_ALPHA = 1.702
ROPE_BASE = 10000.0
NORM_EPS = 1e-6
MASK_VALUE = -1e30
EXP_CLAMP = 80.0

SUBLANES = 8
ROW_ALIGN = SUBLANES
RUN_CHUNK = 64
LANES = 128
MOD_ROWS = 16
VMEM_LIMIT = 56 * 1024 * 1024
NT = (((1,), (1,)), ((), ()))


def _params(sem, vmem=None):
    return pltpu.CompilerParams(dimension_semantics=sem, vmem_limit_bytes=vmem)


def _dot(a, b):
    return jnp.dot(a, b, preferred_element_type=F32)


def _dot_nt(a, b):
    return lax.dot_general(a, b, NT, preferred_element_type=F32)


def _sigmoid(x):
    return 0.5 * jnp.tanh(0.5 * x) + 0.5


def _silu(x):
    return x * _sigmoid(x)


def _rms(x, g):
    return x * lax.rsqrt(jnp.mean(x * x, axis=-1, keepdims=True) + NORM_EPS) * g


def _mod_kernel(cc_ref, w_ref, b_ref, o_ref):
    cc = cc_ref[...]
    o_ref[0] = jnp.dot(_silu(cc), w_ref[0], precision=HIGHEST, preferred_element_type=F32) + b_ref[0]


def _modulation(cc, ada_w, ada_b):
    depth, d, n = ada_w.shape
    tn = n // 4
    return pl.pallas_call(
        _mod_kernel,
        grid=(depth, n // tn),
        in_specs=[pl.BlockSpec((MOD_ROWS, d), lambda l, j: (0, 0)),
                  pl.BlockSpec((1, d, tn), lambda l, j: (l, 0, j)),
                  pl.BlockSpec((1, 1, tn), lambda l, j: (l, 0, j))],
        out_specs=pl.BlockSpec((1, MOD_ROWS, tn), lambda l, j: (l, 0, j)),
        out_shape=jax.ShapeDtypeStruct((depth, MOD_ROWS, n), F32),
        compiler_params=_params(("arbitrary", "arbitrary")),
    )(cc, ada_w, ada_b.reshape(depth, 1, n))


def _inproj_kernel(x_ref, sh_ref, sc_ref, g_ref, w_ref, cos_ref, sin_ref, qg_ref, kg_ref, bd_ref, lbf_ref, lbb_ref,
                   qa_ref, qb_ref, k_ref, v_ref, hq_ref, kf_ref, lff_ref, kb_ref, lfb_ref, hi_ref, sg_ref):
    x = x_ref[0]
    xm = (_rms(x, g_ref[...]) * (1.0 + sc_ref[0]) + sh_ref[0]).astype(BF16)
    wd = NAT_WIDTH

    def proj(i):
        return _dot(xm, w_ref[:, i * wd:(i + 1) * wd])

    def head_norm(y, g):
        ss = _dot((y * y).astype(BF16), bd_ref[...])
        return y * lax.rsqrt(ss * (1.0 / NAT_HEAD_DIM) + NORM_EPS) * g

    cos = cos_ref[...]
    sin = sin_ref[...]
    lane = lax.broadcasted_iota(I32, cos.shape, 1)
    first = (lane % 32) < 16

    def rope(y):
        outs = []
        for gi in range(wd // LANES):
            yg = y[:, gi * LANES:(gi + 1) * LANES]
            partner = jnp.where(first, pltpu.roll(yg, LANES - 16, 1), pltpu.roll(yg, 16, 1))
            outs.append(yg * cos + partner * sin)
        return jnp.concatenate(outs, axis=-1)

    qf = head_norm(proj(0), qg_ref[...])
    qa_ref[0] = rope(qf).astype(BF16)
    qb_ref[0] = qf.astype(BF16)
    k_ref[0] = rope(head_norm(proj(1), kg_ref[...])).astype(BF16)
    v_ref[0] = proj(2).astype(BF16)
    hq_ref[0] = (_silu(proj(3)) * (HGRN_DIM ** -0.5)).astype(BF16)
    for i, lb_ref, kk_ref, lf_ref in ((4, lbf_ref, kf_ref, lff_ref), (5, lbb_ref, kb_ref, lfb_ref)):
        lb = lb_ref[...]
        f = lb + (1.0 - lb) * _sigmoid(proj(i))
        kk_ref[0] = (1.0 - f).astype(BF16)
        lf_ref[0] = jnp.log(f)
    hi_ref[0] = proj(6).astype(BF16)
    sg_ref[0] = _silu(proj(7)).astype(BF16)


def _inproj(x, shift, scale, g, w_bf, cos, sin, qg, kg, bd, lbf, lbb, tm):
    b, t, d = x.shape
    wd = NAT_WIDTH
    tok = lambda bi, ti: (bi, ti, 0)
    vec = lambda bi, ti: (bi, 0, 0)
    const = lambda bi, ti: (0, 0)
    out_dtypes = (BF16, BF16, BF16, BF16, BF16, BF16, F32, BF16, F32, BF16, BF16)
    return pl.pallas_call(
        _inproj_kernel,
        grid=(b, t // tm),
        in_specs=[pl.BlockSpec((1, tm, d), tok), pl.BlockSpec((1, 1, d), vec), pl.BlockSpec((1, 1, d), vec),
                  pl.BlockSpec((1, d), const), pl.BlockSpec(w_bf.shape, const),
                  pl.BlockSpec((tm, LANES), lambda bi, ti: (ti, 0)), pl.BlockSpec((tm, LANES), lambda bi, ti: (ti, 0)),
                  pl.BlockSpec((1, wd), const), pl.BlockSpec((1, wd), const), pl.BlockSpec((wd, wd), const),
                  pl.BlockSpec((1, wd), const), pl.BlockSpec((1, wd), const)],
        out_specs=[pl.BlockSpec((1, tm, wd), tok)] * len(out_dtypes),
        out_shape=[jax.ShapeDtypeStruct((b, t, wd), dt) for dt in out_dtypes],
        compiler_params=_params(("arbitrary", "arbitrary"), VMEM_LIMIT),
    )(x, shift, scale, g, w_bf, cos, sin, qg, kg, bd, lbf, lbb)


def _nat_kernel(qa_ref, qb_ref, kp_ref, kc_ref, kn_ref, vp_ref, vc_ref, vn_ref, kx_ref, vx_ref, bias_ref,
                o_ref, kbuf, vbuf, s1_scr, s2_scr, p1_scr, p2_scr, *, rows, rb):
    i = pl.program_id(1)
    blk = rb * GRID_W
    for s, (kr, vr) in enumerate(((kp_ref, vp_ref), (kc_ref, vc_ref), (kn_ref, vn_ref))):
        kbuf[s * blk:(s + 1) * blk, :] = kr[0]
        vbuf[s * blk:(s + 1) * blk, :] = vr[0]
    lane = lax.broadcasted_iota(I32, (GRID_W, LANES), 1)
    low = lane < NAT_HEAD_DIM
    kwin = NAT_KH * GRID_W
    pair = 2 * GRID_W
    sm_rows = 16

    def stack_heads(q2):
        q2 = q2.astype(F32)
        return jnp.concatenate([jnp.where(low, q2, 0.0), jnp.where(low, 0.0, q2)], axis=0).astype(BF16)

    def window(j):
        r = i * rb + j
        rs = jnp.clip(r - NAT_KH // 2, 0, rows - NAT_KH)
        start = pl.multiple_of((rs - i * rb + rb) * GRID_W, GRID_W)
        return start, rs - r + NAT_KH - 1, pl.multiple_of(j * GRID_W, GRID_W)

    def scores(j, u):
        start, cls, q0 = window(j)
        for p in range(NAT_HEADS // 2):
            ls = slice(p * LANES, (p + 1) * LANES)
            bias = jnp.concatenate([bias_ref[cls, 2 * p], bias_ref[cls, 2 * p + 1]], axis=0)
            s1_scr[u, p * pair:(p + 1) * pair, :] = _dot_nt(stack_heads(qa_ref[0, pl.ds(q0, GRID_W), ls]),
                                                            kbuf[pl.ds(start, kwin), ls]) + bias
            s2_scr[u, p * pair:(p + 1) * pair, :] = _dot_nt(stack_heads(qb_ref[0, pl.ds(q0, GRID_W), ls]),
                                                            kx_ref[0, :, ls])

    def softmax(u):
        for c in range(NAT_HEADS * GRID_W // sm_rows):
            rsl = slice(c * sm_rows, (c + 1) * sm_rows)
            a = s1_scr[u, rsl, :]
            bb = s2_scr[u, rsl, :]
            m = jnp.maximum(jnp.max(a, axis=-1, keepdims=True), jnp.max(bb, axis=-1, keepdims=True))
            ea = jnp.exp(a - m)
            eb = jnp.exp(bb - m)
            inv = 1.0 / (jnp.sum(ea, axis=-1, keepdims=True) + jnp.sum(eb, axis=-1, keepdims=True))
            p1_scr[u, rsl, :] = (ea * inv).astype(BF16)
            p2_scr[u, rsl, :] = (eb * inv).astype(BF16)

    def values(j, u):
        start, _, q0 = window(j)
        for p in range(NAT_HEADS // 2):
            ls = slice(p * LANES, (p + 1) * LANES)
            o = (_dot(p1_scr[u, p * pair:(p + 1) * pair, :], vbuf[pl.ds(start, kwin), ls])
                 + _dot(p2_scr[u, p * pair:(p + 1) * pair, :], vx_ref[0, :, ls]))
            o_ref[0, pl.ds(q0, GRID_W), ls] = jnp.where(low, o[:GRID_W], o[GRID_W:]).astype(BF16)

    nset = s1_scr.shape[0]

    def row_group(jg, carry):
        for u in range(nset):
            scores(nset * jg + u, u)
        for u in range(nset):
            softmax(u)
        for u in range(nset):
            values(nset * jg + u, u)
        return carry

    lax.fori_loop(0, rb // nset, row_group, 0)


def _nat_attention(qa, qb, k, v, kx, vx, bias, rows):
    b, s, wd = qa.shape
    lx = kx.shape[1]
    rb = NAT_KH
    blk = rb * GRID_W
    nb = rows // rb
    cur = lambda bi, i: (bi, i, 0)
    prv = lambda bi, i: (bi, jnp.maximum(i - 1, 0), 0)
    nxt = lambda bi, i: (bi, jnp.minimum(i + 1, nb - 1), 0)
    ctx = lambda bi, i: (bi, 0, 0)
    tile = lambda im: pl.BlockSpec((1, blk, wd), im)
    return pl.pallas_call(
        functools.partial(_nat_kernel, rows=rows, rb=rb),
        grid=(b, nb),
        in_specs=[tile(cur), tile(cur), tile(prv), tile(cur), tile(nxt), tile(prv), tile(cur), tile(nxt),
                  pl.BlockSpec((1,) + kx.shape[1:], ctx), pl.BlockSpec((1,) + vx.shape[1:], ctx),
                  pl.BlockSpec(bias.shape, lambda bi, i: (0, 0, 0, 0))],
        out_specs=tile(cur),
        out_shape=jax.ShapeDtypeStruct((b, s, wd), BF16),
        scratch_shapes=[pltpu.VMEM((3 * blk, wd), BF16), pltpu.VMEM((3 * blk, wd), BF16),
                        pltpu.VMEM((NAT_ROWS_IN_FLIGHT, NAT_HEADS * GRID_W, NAT_KH * GRID_W), F32),
                        pltpu.VMEM((NAT_ROWS_IN_FLIGHT, NAT_HEADS * GRID_W, lx), F32),
                        pltpu.VMEM((NAT_ROWS_IN_FLIGHT, NAT_HEADS * GRID_W, NAT_KH * GRID_W), BF16),
                        pltpu.VMEM((NAT_ROWS_IN_FLIGHT, NAT_HEADS * GRID_W, lx), BF16)],
        compiler_params=_params(("arbitrary", "arbitrary"), VMEM_LIMIT),
    )(qa, qb, k, k, k, v, v, v, kx, vx, bias)


def _nat_bias_table(rpb):
    qc = np.arange(GRID_W)
    wc = np.clip(qc - NAT_KW // 2, 0, GRID_W - NAT_KW)
    kc = np.arange(GRID_W)
    valid = (kc[None, :] >= wc[:, None]) & (kc[None, :] < wc[:, None] + NAT_KW)
    cidx = np.clip(kc[None, :] - qc[:, None] + NAT_KW - 1, 0, 2 * NAT_KW - 2)
    cols = jnp.where(valid[None, None], rpb.astype(F32)[:, :, cidx], MASK_VALUE)
    tab = jnp.stack([cols[:, c:c + NAT_KH] for c in range(NAT_KH)], axis=0)
    tab = tab.transpose(0, 1, 3, 2, 4)
    return tab.reshape(NAT_KH, NAT_HEADS, GRID_W, NAT_KH * GRID_W)


def _hgrn_kernel(*refs, rev, with_out, cb):
    if with_out:
        q_ref, kk_ref, lf_ref, v_ref, s0_ref, tri_ref, o_ref, sf_ref, st = refs
    else:
        kk_ref, lf_ref, v_ref, s0_ref, tri_ref, sf_ref, st = refs
    i = pl.program_id(1)
    c_sz = HGRN_CHUNK

    @pl.when(i == 0)
    def _():
        st[...] = s0_ref[0]

    tri = tri_ref[...]
    tri_b = tri.astype(BF16)
    keep = tri > 0.5

    def chunk(ci, carry):
        c = (cb - 1 - ci) if rev else ci
        r0 = pl.multiple_of(c * c_sz, c_sz)
        lf = lf_ref[0, pl.ds(r0, c_sz), :]
        lf_hi = lf.astype(BF16)
        rest = lf - lf_hi.astype(F32)
        lf_mid = rest.astype(BF16)
        lf_lo = (rest - lf_mid.astype(F32)).astype(BF16)
        bsum = _dot(tri_b, lf_hi) + (_dot(tri_b, lf_mid) + _dot(tri_b, lf_lo))
        for h in range(HGRN_HEADS):
            ls = slice(h * HGRN_DIM, (h + 1) * HGRN_DIM)
            bh = bsum[:, ls]
            tot = bh[0:1] if rev else bh[c_sz - 1:c_sz]
            kh = kk_ref[0, pl.ds(r0, c_sz), ls].astype(F32)
            vh = v_ref[0, pl.ds(r0, c_sz), ls]
            state = st[h]
            if with_out:
                qh = q_ref[0, pl.ds(r0, c_sz), ls].astype(F32)
                blocks = []
                for sb in range(c_sz // HGRN_SUB):
                    lo = sb * HGRN_SUB
                    ref_row = lo + HGRN_SUB - 1 if rev else lo
                    cref = bh[ref_row:ref_row + 1]
                    f1 = jnp.exp(bh[lo:lo + HGRN_SUB] - cref)
                    f2 = jnp.exp(jnp.minimum(cref - bh, EXP_CLAMP))
                    blocks.append(_dot_nt((qh[lo:lo + HGRN_SUB] * f1).astype(BF16), (kh * f2).astype(BF16)))
                scores = jnp.where(keep, jnp.concatenate(blocks, axis=0), 0.0)
                o = _dot(scores.astype(BF16), vh) + _dot_nt((qh * jnp.exp(bh)).astype(BF16), state.astype(BF16))
                o_ref[0, pl.ds(r0, c_sz), ls] = o.astype(BF16)
            kd = (kh * jnp.exp(tot - bh)).astype(BF16)
            vt = vh.astype(F32).T.astype(BF16)
            st[h] = state * jnp.exp(tot) + _dot(vt, kd)
        return carry

    lax.fori_loop(0, cb, chunk, 0, unroll=True)

    @pl.when(i == pl.num_programs(1) - 1)
    def _():
        sf_ref[0] = st[...]


def _hgrn_scan(q, kk, lf, v, s0, tri, rev, with_out):
    b, t, wd = kk.shape
    nchunk = t // HGRN_CHUNK
    cb = min(8, nchunk)
    nblk = nchunk // cb
    tb = cb * HGRN_CHUNK
    tok = (lambda bi, i: (bi, nblk - 1 - i, 0)) if rev else (lambda bi, i: (bi, i, 0))
    st_map = lambda bi, i: (bi, 0, 0, 0)
    st_shape = (b, HGRN_HEADS, HGRN_DIM, HGRN_DIM)
    tile = pl.BlockSpec((1, tb, wd), tok)
    st_spec = pl.BlockSpec((1,) + st_shape[1:], st_map)
    ins = ([q] if with_out else []) + [kk, lf, v, s0, tri]
    in_specs = [tile] * (len(ins) - 2) + [st_spec, pl.BlockSpec(tri.shape, lambda bi, i: (0, 0))]
    out_specs = ([tile] if with_out else []) + [st_spec]
    out_shape = ([jax.ShapeDtypeStruct((b, t, wd), BF16)] if with_out else []) + [jax.ShapeDtypeStruct(st_shape, F32)]
    res = pl.pallas_call(
        functools.partial(_hgrn_kernel, rev=rev, with_out=with_out, cb=cb),
        grid=(b, nblk),
        in_specs=in_specs, out_specs=out_specs, out_shape=out_shape,
        scratch_shapes=[pltpu.VMEM(st_shape[1:], F32)],
        compiler_params=_params(("arbitrary", "arbitrary"), VMEM_LIMIT),
    )(*ins)
    return res if with_out else (None, res[0])


def _pack_bf16_pairs(x):
    half = x.shape[-1] // 2
    lo = pltpu.bitcast(x[:, :half].astype(BF16).astype(F32), U32) >> 16
    hi = pltpu.bitcast(x[:, half:].astype(BF16).astype(F32), U32) & jnp.uint32(0xFFFF0000)
    return hi | lo


def _unpack_bf16_pairs(w):
    lo = pltpu.bitcast(w << 16, F32)
    hi = pltpu.bitcast(w & jnp.uint32(0xFFFF0000), F32)
    return lo, hi


def _route_tail(h1, g2_ref, sh2_ref, sc2_ref, rwt_ref, rb_ref, su_ref,
                h1_ref, xm_ref, pos_ref, wgt_ref, cnt_ref, *, td):
    h1_ref[0] = h1
    xm2 = _rms(h1, g2_ref[...]) * (1.0 + sc2_ref[0]) + sh2_ref[0]
    xm_hi = xm2.astype(BF16)
    xm_ref[...] = xm_hi
    xm_lo = (xm2 - xm_hi.astype(F32)).astype(BF16)
    ne = rb_ref.shape[0]
    part = _dot_nt(rwt_ref[...], xm_hi)
    logits = part[:ne] + part[ne:] + _dot_nt(rwt_ref[:ne, :], xm_lo) + rb_ref[...]
    tm = logits.shape[1]
    eid = lax.broadcasted_iota(I32, (ne, tm), 0).astype(F32)
    vals, hots = [], []
    for k in range(TOP_K):
        m = jnp.max(logits, axis=0, keepdims=True)
        sel = jnp.min(jnp.where(logits == m, eid, float(ne)), axis=0, keepdims=True)
        hot = eid == sel
        logits = jnp.where(hot, -jnp.inf, logits)
        vals.append(m)
        hots.append(hot)
    es = [jnp.exp(vv - vals[0]) for vv in vals]
    den = es[0] + es[1] + es[2] + es[3]
    onehot = jnp.zeros((ne, tm), F32)
    for k in range(TOP_K):
        wgt_ref[k:k + 1, :] = es[k] / den
        onehot = onehot + hots[k].astype(F32)
    prefix = _dot(onehot.astype(BF16), su_ref[...])
    strict_lower = (lax.broadcasted_iota(I32, (ne, ne), 0) > lax.broadcasted_iota(I32, (ne, ne), 1)).astype(BF16)
    lane = lax.broadcasted_iota(I32, (ne, LANES), 1)
    counts = jnp.zeros((ne, LANES), F32)
    starts = []
    for j in range(tm // td):
        cj = jnp.sum(onehot[:, j * td:(j + 1) * td], axis=1, keepdims=True)
        padded = jnp.floor((cj + (ROW_ALIGN - 1)) * (1.0 / ROW_ALIGN)) * ROW_ALIGN
        group_start = _dot(strict_lower, jnp.broadcast_to(padded, (ne, LANES)).astype(BF16))
        starts.append(jnp.broadcast_to(group_start[:, 0:1], (ne, td)))
        counts = counts + jnp.where(lane == j, cj, 0.0)
    row = prefix + jnp.concatenate(starts, axis=1)
    for k in range(TOP_K):
        pos_ref[k:k + 1, :] = jnp.sum(jnp.where(hots[k], row, 0.0), axis=0, keepdims=True).astype(I32)
    cnt_ref[0] = counts


def _tail_specs(b, s, d, tm):
    nt = s // tm
    n = b * s
    vec = lambda bi, ti: (bi, 0, 0)
    const = lambda bi, ti: (0, 0)
    in_specs = [pl.BlockSpec((1, d), const), pl.BlockSpec((1, 1, d), vec), pl.BlockSpec((1, 1, d), vec),
                pl.BlockSpec((2 * N_EXPERTS, d), const), pl.BlockSpec((N_EXPERTS, 1), const),
                pl.BlockSpec((tm, tm), const)]
    flat = lambda bi, ti: (0, bi * nt + ti)
    out_specs = [pl.BlockSpec((1, tm, d), lambda bi, ti: (bi, ti, 0)),
                 pl.BlockSpec((tm, d), lambda bi, ti: (bi * nt + ti, 0)),
                 pl.BlockSpec((TOP_K, tm), flat), pl.BlockSpec((TOP_K, tm), flat),
                 pl.BlockSpec((1, N_EXPERTS, LANES), lambda bi, ti: (bi * nt + ti, 0, 0))]
    out_shape = [jax.ShapeDtypeStruct((b, s, d), F32), jax.ShapeDtypeStruct((n, d), BF16),
                 jax.ShapeDtypeStruct((TOP_K, n), I32), jax.ShapeDtypeStruct((TOP_K, n), F32),
                 jax.ShapeDtypeStruct((b * nt, N_EXPERTS, LANES), F32)]
    return in_specs, out_specs, out_shape


def _outproj_kernel(nat_ref, of_ref, ob_ref, sg_ref, x_ref, gate_ref, on_ref, w_ref,
                    g2_ref, sh2_ref, sc2_ref, rwt_ref, rb_ref, su_ref,
                    h1_ref, xm_ref, pos_ref, wgt_ref, cnt_ref, *, td):
    o = of_ref[0].astype(F32) + ob_ref[0].astype(F32)
    gated = []
    for h in range(HGRN_HEADS):
        ls = slice(h * HGRN_DIM, (h + 1) * HGRN_DIM)
        gated.append(_rms(o[:, ls], on_ref[...]) * sg_ref[0, :, ls].astype(F32))
    gated = jnp.concatenate(gated, axis=-1).astype(BF16)
    y = _dot(nat_ref[0], w_ref[:NAT_WIDTH, :]) + _dot(gated, w_ref[NAT_WIDTH:, :])
    h1 = x_ref[0] + gate_ref[0] * y
    _route_tail(h1, g2_ref, sh2_ref, sc2_ref, rwt_ref, rb_ref, su_ref,
                h1_ref, xm_ref, pos_ref, wgt_ref, cnt_ref, td=td)


def _outproj_route(nat, of, ob, sg, x, gate, on_g, w_bf, tail_args, tm, td):
    b, s, d = x.shape
    tok = lambda bi, ti: (bi, ti, 0)
    vec = lambda bi, ti: (bi, 0, 0)
    const = lambda bi, ti: (0, 0)
    t_in, out_specs, out_shape = _tail_specs(b, s, d, tm)
    wide = pl.BlockSpec((1, tm, NAT_WIDTH), tok)
    return pl.pallas_call(
        functools.partial(_outproj_kernel, td=td),
        grid=(b, s // tm),
        in_specs=[wide, wide, wide, wide, pl.BlockSpec((1, tm, d), tok), pl.BlockSpec((1, 1, d), vec),
                  pl.BlockSpec((1, HGRN_DIM), const), pl.BlockSpec(w_bf.shape, const)] + t_in,
        out_specs=out_specs, out_shape=out_shape,
        compiler_params=_params(("arbitrary", "arbitrary"), VMEM_LIMIT),
    )(nat, of, ob, sg, x, gate, on_g, w_bf, *tail_args)


def _conf_in_kernel(x_ref, sh_ref, sc_ref, g_ref, w_ref, b_ref, u_ref):
    d = x_ref.shape[-1]
    xm = (_rms(x_ref[0], g_ref[...]) * (1.0 + sc_ref[0]) + sh_ref[0]).astype(BF16)
    a = _dot(xm, w_ref[:, :d]) + b_ref[:, :d]
    gate = _dot(xm, w_ref[:, d:]) + b_ref[:, d:]
    u_ref[0] = (a * _sigmoid(gate)).astype(BF16)


def _conf_in(x, shift, scale, g, w_bf, b1, tm):
    b, s, d = x.shape
    tok = lambda bi, ti: (bi, ti, 0)
    vec = lambda bi, ti: (bi, 0, 0)
    const = lambda bi, ti: (0, 0)
    return pl.pallas_call(
        _conf_in_kernel,
        grid=(b, s // tm),
        in_specs=[pl.BlockSpec((1, tm, d), tok), pl.BlockSpec((1, 1, d), vec), pl.BlockSpec((1, 1, d), vec),
                  pl.BlockSpec((1, d), const), pl.BlockSpec(w_bf.shape, const), pl.BlockSpec((1, 2 * d), const)],
        out_specs=pl.BlockSpec((1, tm, d), tok),
        out_shape=jax.ShapeDtypeStruct((b, s, d), BF16),
        compiler_params=_params(("arbitrary", "arbitrary"), VMEM_LIMIT),
    )(x, shift, scale, g, w_bf, b1)


def _conf_out_kernel(up_ref, uc_ref, un_ref, dw_ref, dwb_ref, lg_ref, lb_ref, w_ref, b2_ref, x_ref, gate_ref,
                     g2_ref, sh2_ref, sc2_ref, rwt_ref, rb_ref, su_ref,
                     h1_ref, xm_ref, pos_ref, wgt_ref, cnt_ref, ubuf, cbuf, sbuf, *, rc, td):
    ti = pl.program_id(1)
    tm, d = uc_ref.shape[1], uc_ref.shape[2]
    hal = CONV_HALO
    prev = up_ref[0].astype(F32)
    nxt = un_ref[0].astype(F32)
    ubuf[0:hal, :] = jnp.where(ti > 0, prev, jnp.zeros_like(prev))
    ubuf[hal:hal + tm, :] = uc_ref[0].astype(F32)
    ubuf[hal + tm:, :] = jnp.where(ti < pl.num_programs(1) - 1, nxt, jnp.zeros_like(nxt))
    base = hal - CONV_WIDTH // 2
    lc = sbuf.shape[2]
    sub = sbuf.shape[0] + 1
    srows = sbuf.shape[1]
    bc = 64

    def shift_chunk(r0, nrows, l0):
        win = ubuf[pl.ds(r0, nrows + sub), l0:l0 + lc]
        for r in range(1, sub):
            sbuf[r - 1, pl.ds(r0, nrows), :] = win[r:r + nrows]

    for l0 in range(0, d, lc):
        def shift_rows(ci, carry, l0=l0):
            shift_chunk(pl.multiple_of(ci * bc, bc), bc, l0)
            return carry

        lax.fori_loop(0, srows // bc, shift_rows, 0)
        if srows % bc:
            shift_chunk(srows // bc * bc, srows % bc, l0)

        def conv_rows(ci, carry, l0=l0):
            r0 = pl.multiple_of(ci * rc, rc)
            acc = jnp.zeros((rc, lc), F32) + dwb_ref[:, l0:l0 + lc]
            for j in range(CONV_WIDTH):
                r = (base + j) % sub
                a = pl.multiple_of(r0 + (base + j - r), sub)
                src = ubuf[pl.ds(a, rc), l0:l0 + lc] if r == 0 else sbuf[r - 1, pl.ds(a, rc), :]
                acc = acc + src * dw_ref[j:j + 1, l0:l0 + lc]
            cbuf[pl.ds(r0, rc), l0:l0 + lc] = acc
            return carry

        lax.fori_loop(0, tm // rc, conv_rows, 0)
    c = cbuf[...]
    mu = jnp.mean(c, axis=-1, keepdims=True)
    cz = c - mu
    var = jnp.mean(cz * cz, axis=-1, keepdims=True)
    y = _silu(cz * lax.rsqrt(var + NORM_EPS) * lg_ref[...] + lb_ref[...]).astype(BF16)
    y = _dot(y, w_ref[...]) + b2_ref[...]
    h1 = x_ref[0] + gate_ref[0] * y
    _route_tail(h1, g2_ref, sh2_ref, sc2_ref, rwt_ref, rb_ref, su_ref,
                h1_ref, xm_ref, pos_ref, wgt_ref, cnt_ref, td=td)


def _conf_out_route(u, dw, dwb, ln_g, ln_b, w_bf, b2, x, gate, tail_args, tm, td):
    b, s, d = x.shape
    hal = CONV_HALO
    per = tm // hal
    nh = s // hal
    tok = lambda bi, ti: (bi, ti, 0)
    vec = lambda bi, ti: (bi, 0, 0)
    const = lambda bi, ti: (0, 0)
    t_in, out_specs, out_shape = _tail_specs(b, s, d, tm)
    return pl.pallas_call(
        functools.partial(_conf_out_kernel, rc=128, td=td),
        grid=(b, s // tm),
        in_specs=[pl.BlockSpec((1, hal, d), lambda bi, ti: (bi, jnp.maximum(ti * per - 1, 0), 0)),
                  pl.BlockSpec((1, tm, d), tok),
                  pl.BlockSpec((1, hal, d), lambda bi, ti: (bi, jnp.minimum((ti + 1) * per, nh - 1), 0)),
                  pl.BlockSpec(dw.shape, const), pl.BlockSpec((1, d), const), pl.BlockSpec((1, d), const),
                  pl.BlockSpec((1, d), const), pl.BlockSpec(w_bf.shape, const), pl.BlockSpec((1, d), const),
                  pl.BlockSpec((1, tm, d), tok), pl.BlockSpec((1, 1, d), vec)] + t_in,
        out_specs=out_specs, out_shape=out_shape,
        scratch_shapes=[pltpu.VMEM((tm + 2 * hal, d), F32), pltpu.VMEM((tm, d), F32),
                        pltpu.VMEM((SUBLANES - 1, tm + (CONV_WIDTH // SUBLANES) * SUBLANES, min(d, 2 * LANES)), F32)],
        compiler_params=_params(("arbitrary", "arbitrary"), VMEM_LIMIT),
    )(u, u, u, dw, dwb, ln_g, ln_b, w_bf, b2, x, gate, *tail_args)


def _copy_rows(src_ref, src0, dst_ref, dst0, length, sem, *, wait, src_fixed=False):
    def piece(off, size):
        s = src0 if src_fixed else pl.multiple_of(src0 + off, ROW_ALIGN)
        return pltpu.make_async_copy(src_ref.at[pl.ds(s, size), :],
                                     dst_ref.at[pl.ds(pl.multiple_of(dst0 + off, ROW_ALIGN), size), :], sem)

    def go(copy):
        if wait:
            copy.wait()
        else:
            copy.start()

    def big(c, carry):
        go(piece(c * RUN_CHUNK, RUN_CHUNK))
        return carry

    lax.fori_loop(0, lax.shift_right_logical(length, RUN_CHUNK.bit_length() - 1), big, 0)
    size = RUN_CHUNK // 2
    while size >= ROW_ALIGN:
        @pl.when((length & size) != 0)
        def _(size=size):
            go(piece(length & ~(2 * size - 1), size))
        size //= 2


def _wait_rows(src_ref, dst_ref, total, max_rows, sem):
    size = pl.next_power_of_2(max_rows)
    while size >= ROW_ALIGN:
        if size <= max_rows:
            @pl.when((total & size) != 0)
            def _(size=size):
                pltpu.make_async_copy(src_ref.at[pl.ds(0, size), :], dst_ref.at[pl.ds(0, size), :], sem).wait()
        size //= 2


def _moe_dispatch_kernel(gap_ref, prev_ref, meta_ref, x_ref, pos_ref, xs_ref, ybuf, zbuf, sems):
    i = pl.program_id(0)
    last = pl.num_programs(0) - 1
    slot = i % 2
    cap, td = ybuf.shape[1], x_ref.shape[0]
    pos = pos_ref[...]
    rid = lax.broadcasted_iota(I32, (cap, td), 0)
    hit = rid == pos[0:1]
    for k in range(1, TOP_K):
        hit = jnp.logical_or(hit, rid == pos[k:k + 1])
    ybuf[slot] = _pack_bf16_pairs(_dot(hit.astype(BF16), x_ref[...]))

    def wait_runs(m_ref, buf_slot):
        _wait_rows(ybuf.at[buf_slot], xs_ref, m_ref[0, 3, 0], cap, sems.at[buf_slot])

    @pl.when(i > 0)
    def _():
        wait_runs(prev_ref, 1 - slot)

    def run(e, carry):
        _copy_rows(ybuf.at[slot], meta_ref[0, 0, e], xs_ref, meta_ref[0, 2, e], meta_ref[0, 1, e], sems.at[slot],
                   wait=False)
        return carry

    lax.fori_loop(0, N_EXPERTS, run, 0, unroll=4)

    @pl.when(i == last)
    def _():
        wait_runs(meta_ref, slot)
        zbuf[...] = jnp.zeros_like(zbuf)
        for wait in (False, True):
            def fill(e, carry, wait=wait):
                _copy_rows(zbuf, 0, xs_ref, gap_ref[e], gap_ref[N_EXPERTS + e], sems.at[slot], wait=wait, src_fixed=True)
                return carry
            lax.fori_loop(0, N_EXPERTS, fill, 0)


def _moe_dispatch(gap, meta, xm, pos, p, td):
    n, d = xm.shape
    cap = TOP_K * td + N_EXPERTS * ROW_ALIGN
    meta_spec = lambda im: pl.BlockSpec((1, 4, N_EXPERTS), im, memory_space=pltpu.SMEM)
    grid_spec = pltpu.PrefetchScalarGridSpec(
        num_scalar_prefetch=1,
        grid=(n // td,),
        in_specs=[meta_spec(lambda i, g: (jnp.maximum(i - 1, 0), 0, 0)), meta_spec(lambda i, g: (i, 0, 0)),
                  pl.BlockSpec((td, d), lambda i, g: (i, 0)),
                  pl.BlockSpec((TOP_K, td), lambda i, g: (0, i))],
        out_specs=pl.BlockSpec(memory_space=pl.ANY),
        scratch_shapes=[pltpu.VMEM((2, cap, d // 2), U32), pltpu.VMEM((RUN_CHUNK, d // 2), U32),
                        pltpu.SemaphoreType.DMA((2,))],
    )
    return pl.pallas_call(
        _moe_dispatch_kernel,
        grid_spec=grid_spec,
        out_shape=jax.ShapeDtypeStruct((p, d // 2), U32),
        compiler_params=_params(("arbitrary",), VMEM_LIMIT),
    )(gap, meta, meta, xm, pos)


def _moe_expert_kernel(te_ref, nv_ref, xs_ref, w1_ref, b1_ref, w2_ref, b2_ref, ys_ref, w1b, w2b):
    i = pl.program_id(0)
    f = w2_ref.shape[2]
    changed = jnp.logical_or(i == 0, te_ref[i] != te_ref[jnp.maximum(i - 1, 0)])

    @pl.when(jnp.logical_and(changed, i < nv_ref[0]))
    def _():
        w1b[...] = w1_ref[0, 0].astype(BF16)
        w2b[...] = w2_ref[0, 0].astype(BF16)

    @pl.when(i < nv_ref[0])
    def _():
        lo, hi = _unpack_bf16_pairs(xs_ref[...])
        x = jnp.concatenate([lo, hi], axis=-1).astype(BF16)
        y = jnp.zeros((x.shape[0], w2_ref.shape[3]), F32) + b2_ref[0, 0]
        fc = f
        for c in range(f // fc):
            glu = _dot(x, w1b[:, c * fc:(c + 1) * fc]) + b1_ref[0, 0, :, c * fc:(c + 1) * fc]
            lin = _dot(x, w1b[:, f + c * fc:f + (c + 1) * fc]) + b1_ref[0, 0, :, f + c * fc:f + (c + 1) * fc]
            glu = jnp.minimum(glu, SWIGLU_LIMIT)
            lin = jnp.clip(lin, -SWIGLU_LIMIT, SWIGLU_LIMIT)
            act = glu * _sigmoid(SWIGLU_ALPHA * glu) * (lin + 1.0)
            y = y + _dot(act.astype(BF16), w2b[c * fc:(c + 1) * fc, :])
        ys_ref[...] = _pack_bf16_pairs(y)


def _moe_experts(tile_expert, n_valid, xs, layer, w1, b1, w2, b2, tme):
    p, d2 = xs.shape
    _, ne, d, f2 = w1.shape
    f = f2 // 2
    row = lambda i, te, nv: (jnp.minimum(i, nv[0] - 1), 0)
    wsel = lambda i, te, nv: (layer, te[jnp.minimum(i, nv[0] - 1)], 0, 0)
    grid_spec = pltpu.PrefetchScalarGridSpec(
        num_scalar_prefetch=2,
        grid=(p // tme,),
        in_specs=[pl.BlockSpec((tme, d2), row),
                  pl.BlockSpec((1, 1, d, f2), wsel), pl.BlockSpec((1, 1, 1, f2), wsel),
                  pl.BlockSpec((1, 1, f, d), wsel), pl.BlockSpec((1, 1, 1, d), wsel)],
        out_specs=pl.BlockSpec((tme, d2), row),
        scratch_shapes=[pltpu.VMEM((d, f2), BF16), pltpu.VMEM((f, d), BF16)],
    )
    depth = w1.shape[0]
    return pl.pallas_call(
        _moe_expert_kernel,
        grid_spec=grid_spec,
        out_shape=jax.ShapeDtypeStruct((p, d2), U32),
        compiler_params=_params(("arbitrary",), VMEM_LIMIT),
    )(tile_expert, n_valid, xs, w1, b1.reshape(depth, ne, 1, f2), w2, b2.reshape(depth, ne, 1, d))


def _moe_combine_kernel(meta_ref, next_ref, ys_ref, pos_ref, wt_ref, h_ref, gate_ref, o_ref, ybuf, sems):
    t = pl.program_id(0) * pl.num_programs(1) + pl.program_id(1)
    n_tiles = pl.num_programs(0) * pl.num_programs(1)
    slot = t % 2

    cap, td = ybuf.shape[1], pos_ref.shape[0]

    def fetch_runs(m_ref, buf_slot):
        def run(e, carry):
            _copy_rows(ys_ref, m_ref[0, 2, e], ybuf.at[buf_slot], m_ref[0, 0, e], m_ref[0, 1, e], sems.at[buf_slot],
                       wait=False)
            return carry
        lax.fori_loop(0, N_EXPERTS, run, 0, unroll=4)

    @pl.when(t == 0)
    def _():
        ybuf[...] = jnp.zeros_like(ybuf)
        fetch_runs(meta_ref, slot)

    @pl.when(t + 1 < n_tiles)
    def _():
        fetch_runs(next_ref, 1 - slot)

    _wait_rows(ys_ref, ybuf.at[slot], meta_ref[0, 3, 0], cap, sems.at[slot])

    pos = pos_ref[...]
    wt = wt_ref[...]
    cid = lax.broadcasted_iota(I32, (td, cap), 1)
    mix = jnp.zeros((td, cap), F32)
    for k in range(TOP_K):
        mix = mix + jnp.where(cid == pos[:, k:k + 1], wt[:, k:k + 1], 0.0)
    lo, hi = _unpack_bf16_pairs(ybuf[slot])
    y = jnp.concatenate([lo, hi], axis=-1).astype(BF16)
    o_ref[0] = h_ref[0] + gate_ref[0] * _dot(mix.astype(BF16), y)


def _moe_combine(meta, ys, pos_t, wt_t, h, gate, td):
    b, s, d = h.shape
    nt = s // td
    cap = TOP_K * td + N_EXPERTS * ROW_ALIGN
    flat = lambda bi, ti: (bi * nt + ti, 0)
    meta_spec = lambda im: pl.BlockSpec((1, 4, N_EXPERTS), im, memory_space=pltpu.SMEM)
    return pl.pallas_call(
        _moe_combine_kernel,
        grid=(b, nt),
        in_specs=[meta_spec(lambda bi, ti: (bi * nt + ti, 0, 0)),
                  meta_spec(lambda bi, ti: (jnp.minimum(bi * nt + ti + 1, b * nt - 1), 0, 0)),
                  pl.BlockSpec(memory_space=pl.ANY),
                  pl.BlockSpec((td, TOP_K), flat), pl.BlockSpec((td, TOP_K), flat),
                  pl.BlockSpec((1, td, d), lambda bi, ti: (bi, ti, 0)),
                  pl.BlockSpec((1, 1, d), lambda bi, ti: (bi, 0, 0))],
        out_specs=pl.BlockSpec((1, td, d), lambda bi, ti: (bi, ti, 0)),
        out_shape=jax.ShapeDtypeStruct((b, s, d), F32),
        scratch_shapes=[pltpu.VMEM((2, cap, d // 2), U32), pltpu.SemaphoreType.DMA((2,))],
        compiler_params=_params(("arbitrary", "arbitrary"), VMEM_LIMIT),
    )(meta, meta, ys, pos_t, wt_t, h, gate)


def _moe(h1, xm, pos, wgt, counts, gate, layer, w1, b1, w2, b2, tme, td):
    n, d = xm.shape
    per = (n // td) // counts.shape[0]
    cnt = jnp.round(counts[:, :, :per]).astype(I32).transpose(0, 2, 1).reshape(n // td, N_EXPERTS)
    run = (cnt + ROW_ALIGN - 1) // ROW_ALIGN * ROW_ALIGN
    local = jnp.cumsum(run, axis=1) - run
    tot = jnp.sum(run, axis=0)
    cap = (tot + tme - 1) // tme * tme
    ends = jnp.cumsum(cap)
    start = ends - cap
    glob = start[None, :] + jnp.cumsum(run, axis=0) - run
    rows_used = jnp.broadcast_to(jnp.sum(run, axis=1, keepdims=True), run.shape)
    meta = jnp.stack([local, run, glob, rows_used], axis=1)
    gap = jnp.concatenate([start + tot, cap - tot]).astype(I32)
    p = (TOP_K * n + N_EXPERTS * ROW_ALIGN * (n // td) + N_EXPERTS * tme) // tme * tme
    tiles = jnp.arange(p // tme, dtype=I32)
    tile_expert = jnp.minimum(jnp.sum(tiles[:, None] >= (ends // tme)[None, :], axis=1), N_EXPERTS - 1).astype(I32)
    n_valid = (ends[-1:] // tme).astype(I32)
    xs = _moe_dispatch(gap, meta, xm, pos, p, td)
    ys = _moe_experts(tile_expert, n_valid, xs, layer, w1, b1, w2, b2, tme)
    return _moe_combine(meta, ys, pos.T, wgt.T, h1, gate, td)


def _rope_tables(s):
    pos = np.arange(s)
    lane = np.arange(LANES)
    dd = lane % NAT_HEAD_DIM
    n = NAT_HEAD_DIM // 4
    inv_freq = ROPE_BASE ** (-(dd % n).astype(np.float64) / n)
    p = np.where((dd // (NAT_HEAD_DIM // 2))[None, :] == 0, (pos // GRID_W)[:, None], (pos % GRID_W)[:, None])
    ang = (p.astype(np.float32) * inv_freq.astype(np.float32)[None, :]).astype(np.float32)
    sign = np.where((dd % (2 * n)) < n, -1.0, 1.0)[None, :]
    return jnp.asarray(np.cos(ang), F32), jnp.asarray(np.sin(ang) * sign, F32)


def _tile(n, want):
    t = min(want, n)
    while n % t:
        t //= 2
    return t


def kernel(x, c, ctx, c_ctx, ada_w, ada_b, norm1_g, norm2_g, ab_w_in, ab_w_out, nat_q_norm, nat_k_norm, nat_rpb, hgrn_lb, hgrn_o_norm, conv_w1, conv_b1, conv_dw, conv_dw_b, conv_ln_g, conv_ln_b, conv_w2, conv_b2, router_w, router_b, moe_w1, moe_b1, moe_w2, moe_b2):
    b, s, d = x.shape
    l = ctx.shape[1]
    rows = s // GRID_W
    assert ada_w.shape[0] == 2 and b < MOD_ROWS and rows >= NAT_KH and rows % NAT_KH == 0
    assert s % HGRN_CHUNK == 0 and l % HGRN_CHUNK == 0 and d % (2 * LANES) == 0
    tm = _tile(s, 512)
    td = tm
    tme = 512 if TOP_K * b * s >= 512 * N_EXPERTS else 128

    cc = jnp.zeros((MOD_ROWS, d), F32).at[:b].set(c).at[b].set(c_ctx)
    mod = _modulation(cc, ada_w, ada_b)

    def mod_vec(layer, i):
        return mod[layer, :b, i * d:(i + 1) * d].reshape(b, 1, d)

    def ctx_vec(i):
        return jnp.broadcast_to(mod[0, b, i * d:(i + 1) * d].reshape(1, 1, d), (b, 1, d))

    tok = np.arange(tm)
    strict_upper = jnp.asarray((tok[:, None] < tok[None, :]) & (tok[:, None] // td == tok[None, :] // td), BF16)

    def tail_args(layer):
        rw = router_w[layer].T.astype(F32)
        rw_hi = rw.astype(BF16)
        rw_split = jnp.concatenate([rw_hi, (rw - rw_hi.astype(F32)).astype(BF16)], axis=0)
        return (norm2_g[layer].reshape(1, d), mod_vec(layer, 3), mod_vec(layer, 4),
                rw_split, router_b[layer].reshape(N_EXPERTS, 1), strict_upper)

    lb_all = jnp.cumsum(jax.nn.softmax(hgrn_lb.astype(F32), axis=1), axis=1)[:, 0]
    w_in = ab_w_in[0].astype(BF16)
    scale = NAT_HEAD_DIM ** -0.5
    qg = jnp.tile(nat_q_norm[0] * scale, NAT_HEADS).reshape(1, NAT_WIDTH)
    kg = jnp.tile(nat_k_norm[0], NAT_HEADS).reshape(1, NAT_WIDTH)
    head_of = np.arange(NAT_WIDTH) // NAT_HEAD_DIM
    bd = jnp.asarray(head_of[:, None] == head_of[None, :], BF16)
    lbf, lbb = lb_all[0].reshape(1, HGRN_WIDTH), lb_all[1].reshape(1, HGRN_WIDTH)
    cos, sin = _rope_tables(s)
    g1 = norm1_g[0].reshape(1, d)
    lat = _inproj(x, mod_vec(0, 0), mod_vec(0, 1), g1, w_in, cos, sin, qg, kg, bd, lbf, lbb, tm)
    qa, qb, k, v, hq, kf, lff, kb, lfb, hi, sg = lat
    tl = _tile(l, 256)
    cxt = _inproj(ctx, ctx_vec(0), ctx_vec(1), g1, w_in, jnp.ones((l, LANES), F32), jnp.zeros((l, LANES), F32),
                  qg, kg, bd, lbf, lbb, tl)
    _, _, kx, vx, _, kfx, lffx, kbx, lfbx, hix, _ = cxt

    nat = _nat_attention(qa, qb, k, v, kx, vx, _nat_bias_table(nat_rpb[0]), rows)

    tri_f = jnp.asarray(np.tril(np.ones((HGRN_CHUNK, HGRN_CHUNK), np.float32)))
    s0 = jnp.zeros((b, HGRN_HEADS, HGRN_DIM, HGRN_DIM), F32)
    _, sf = _hgrn_scan(None, kfx, lffx, hix, s0, tri_f, False, False)
    of, _ = _hgrn_scan(hq, kf, lff, hi, sf, tri_f, False, True)
    _, sb = _hgrn_scan(None, kbx, lfbx, hix, s0, tri_f.T, True, False)
    ob, _ = _hgrn_scan(hq, kb, lfb, hi, sb, tri_f.T, True, True)

    h1, xm, pos, wgt, counts = _outproj_route(
        nat, of, ob, sg, x, mod_vec(0, 2), hgrn_o_norm[0].reshape(1, HGRN_DIM), ab_w_out[0].astype(BF16),
        tail_args(0), tm, td)
    h = _moe(h1, xm, pos, wgt, counts, mod_vec(0, 5), 0, moe_w1, moe_b1, moe_w2, moe_b2, tme, td)

    u = _conf_in(h, mod_vec(1, 0), mod_vec(1, 1), norm1_g[1].reshape(1, d), conv_w1[0].astype(BF16),
                 conv_b1[0].reshape(1, 2 * d), tm)
    h1, xm, pos, wgt, counts = _conf_out_route(
        u, conv_dw[0], conv_dw_b[0].reshape(1, d), conv_ln_g[0].reshape(1, d), conv_ln_b[0].reshape(1, d),
        conv_w2[0].astype(BF16), conv_b2[0].reshape(1, d), h, mod_vec(1, 2), tail_args(1), tm, td)
    return _moe(h1, xm, pos, wgt, counts, mod_vec(1, 5), 1, moe_w1, moe_b1, moe_w2, moe_b2, tme, td)
```
